```python
import math
import jax, jax.numpy as jnp
from jax import lax
import numpy as np

D_MODEL = 2048
BATCH = 16
SEQ = 2048
DEPTH = 4

MIX_GROUP = D_MODEL // 4
NSA_HEAD_DIM = 64
NSA_HEADS = MIX_GROUP // NSA_HEAD_DIM
NSA_KV_GROUPS = max(1, NSA_HEADS // 4)
NSA_KV = NSA_KV_GROUPS * NSA_HEAD_DIM
CMP_BLOCK = 32
CMP_STRIDE = 16
SLC_BLOCK = 64
SLC_TOP_N = 8
WINDOW = 512
Q_BLOCK = 128
REL_BUCKETS = 32
REL_MAX_DIST = 128
SSD_HEAD_DIM = 64
SSD_HEADS = MIX_GROUP // SSD_HEAD_DIM
SSD_INNER = SSD_HEADS * SSD_HEAD_DIM
SSD_GROUPS = 2
SSD_STATE = 128
SSD_CONV = 4
SSD_XBC = SSD_INNER + 2 * SSD_GROUPS * SSD_STATE
GDN_HEAD_DIM = 128
GDN_HEADS = MIX_GROUP // GDN_HEAD_DIM
GDN_WIDTH = GDN_HEADS * GDN_HEAD_DIM
GDN_CONV = 4
GLA_DV = 128
GLA_HEADS = MIX_GROUP // GLA_DV
GLA_DK = GLA_DV // 2
GLA_KEY = GLA_HEADS * GLA_DK
GLA_VAL = GLA_HEADS * GLA_DV
GLA_GATE_RANK = 16
GLA_GATE_NORM = 16.0
CHUNK = 64
MLP_HIDDEN = 4 * D_MODEL
EPS = 1e-6
NEG_INF = -1e30
FORCE_SCORE = 1e9
IN_SPLITS = (NSA_HEADS * NSA_HEAD_DIM, NSA_KV, NSA_KV, NSA_KV, NSA_KV, NSA_KV, NSA_KV, NSA_HEADS * 3,
             SSD_INNER, SSD_XBC, SSD_HEADS,
             GDN_WIDTH, GDN_WIDTH, GDN_WIDTH, GDN_WIDTH, GDN_HEADS, GDN_HEADS,
             GLA_KEY, GLA_KEY, GLA_VAL, GLA_VAL, GLA_GATE_RANK)
IN_COLS = sum(IN_SPLITS)
MIX_OUT = NSA_HEADS * NSA_HEAD_DIM + SSD_INNER + GDN_WIDTH + GLA_VAL

kernel_name = 'hymba_nsa_ssd_gdn_gla_trunk'


def _split(t, sizes):
    cuts, acc = [], 0
    for s in sizes[:-1]:
        acc += s
        cuts.append(acc)
    return jnp.split(t, cuts, axis=-1)


def _rmsnorm(x, w):
    xf = x.astype(jnp.float32)
    y = xf * lax.rsqrt(jnp.mean(xf * xf, axis=-1, keepdims=True) + EPS)
    return (y * w.astype(jnp.float32)).astype(x.dtype)


def _l2norm(x):
    return x * lax.rsqrt(jnp.sum(x * x, axis=-1, keepdims=True) + EPS)


def _causal_conv(x, w):
    K, C = w.shape
    return lax.conv_general_dilated(x, w.astype(x.dtype)[:, None, :], window_strides=(1,), padding=((K - 1, 0),),
                                    dimension_numbers=('NWC', 'WIO', 'NWC'), feature_group_count=C)


def _masked_softmax(logits, mask):
    p = jax.nn.softmax(jnp.where(mask, logits.astype(jnp.float32), NEG_INF), axis=-1)
    return p * mask


def _t5_bucket(rel):
    n = jnp.maximum(rel, 0)
    exact = REL_BUCKETS // 2
    large = exact + (jnp.log(jnp.maximum(n, 1).astype(jnp.float32) / exact)
                     / math.log(REL_MAX_DIST / exact) * (REL_BUCKETS - exact)).astype(jnp.int32)
    return jnp.where(n < exact, n, jnp.minimum(large, REL_BUCKETS - 1))


def _rel_bias(table, rel, G):
    b = jnp.moveaxis(table[_t5_bucket(rel)], -1, 0)
    return b.reshape(G, table.shape[1] // G, *rel.shape)


def _chunk_state_scan(local, decay):
    def step(state, inp):
        u, a = inp
        return a * state + u, state
    lt = jnp.moveaxis(local, 1, 0)
    dt = jnp.moveaxis(decay, 1, 0)
    _, prev = lax.scan(step, jnp.zeros_like(lt[0]), (lt, dt))
    return jnp.moveaxis(prev, 0, 1)


def _nsa_compress(t, pos, w1, w2):
    B, S, G, hd = t.shape
    r = CMP_BLOCK // CMP_STRIDE
    ch = t.reshape(B, S // CMP_STRIDE, CMP_STRIDE, G, hd)
    n = S // CMP_STRIDE - r + 1
    blk = jnp.concatenate([ch[:, i:i + n] for i in range(r)], axis=2) + pos[:, None, :]
    blk = jnp.swapaxes(blk, 2, 3).reshape(B, n, G, CMP_BLOCK * hd)
    return jax.nn.silu(blk @ w1) @ w2


def _nsa(q, k_cmp, v_cmp, k_slc, v_slc, k_win, v_win, gates, cmp_pos, cmp_w1, cmp_w2, rel_bias):
    B, S, H, hd = q.shape
    G = NSA_KV_GROUPS
    R = H // G
    dt = q.dtype
    scale = hd ** -0.5
    kc = _nsa_compress(k_cmp, cmp_pos[0], cmp_w1[0], cmp_w2[0])
    vc = _nsa_compress(v_cmp, cmp_pos[1], cmp_w1[1], cmp_w2[1])
    nc = kc.shape[1]
    cmp_start = jnp.arange(nc) * CMP_STRIDE
    cmp_end = cmp_start + CMP_BLOCK - 1
    nsb = S // SLC_BLOCK
    top_n = min(SLC_TOP_N, nsb)
    sel_start = jnp.arange(nsb) * SLC_BLOCK
    overlap = ((cmp_start[:, None] <= sel_start[None, :] + SLC_BLOCK - 1)
               & (cmp_end[:, None] >= sel_start[None, :])).astype(jnp.float32)
    ks = k_slc.reshape(B, nsb, SLC_BLOCK, G, hd).transpose(0, 3, 1, 2, 4)
    vs = v_slc.reshape(B, nsb, SLC_BLOCK, G, hd).transpose(0, 3, 1, 2, 4)
    kw = jnp.pad(k_win, ((0, 0), (WINDOW, 0), (0, 0), (0, 0)))
    vw = jnp.pad(v_win, ((0, 0), (WINDOW, 0), (0, 0), (0, 0)))
    kw_len = Q_BLOCK + WINDOW
    win_rel = jnp.arange(Q_BLOCK)[:, None] + WINDOW - jnp.arange(kw_len)[None, :]
    win_mask = (win_rel >= 0) & (win_rel < WINDOW)
    win_bias = _rel_bias(rel_bias, win_rel, G)
    table = rel_bias.reshape(REL_BUCKETS, G, R)
    bi = jnp.arange(B)[:, None, None, None]
    gi = jnp.arange(G)[None, :, None, None]

    def block(args):
        qb, gb, qs = args
        t = qs + jnp.arange(Q_BLOCK)
        qg = (qb * scale).reshape(B, Q_BLOCK, G, R, hd)
        rel_c = t[:, None] - cmp_end[None, :]
        s_c = jnp.einsum('bqgrd,bkgd->bgrqk', qg, kc) + _rel_bias(rel_bias, rel_c, G)
        p_c = _masked_softmax(s_c, rel_c >= 0)
        o_c = jnp.einsum('bgrqk,bkgd->bqgrd', p_c.astype(dt), vc)
        imp = jnp.einsum('bgrqk,kj->bgqj', p_c, overlap)
        blk = jnp.arange(nsb)[None, :]
        cur = (t // SLC_BLOCK)[:, None]
        forced = (blk == 0) | (blk == cur) | (blk == cur - 1)
        imp = jnp.where(forced, FORCE_SCORE, imp)
        imp = jnp.where(blk <= cur, imp, NEG_INF)
        _, idx = lax.top_k(imp, top_n)
        k_sel = ks[bi, gi, idx].reshape(B, G, Q_BLOCK, top_n * SLC_BLOCK, hd)
        v_sel = vs[bi, gi, idx].reshape(B, G, Q_BLOCK, top_n * SLC_BLOCK, hd)
        pos = (idx[..., None] * SLC_BLOCK + jnp.arange(SLC_BLOCK)).reshape(B, G, Q_BLOCK, top_n * SLC_BLOCK)
        rel_s = t[:, None] - pos
        bias_s = jnp.moveaxis(table[_t5_bucket(rel_s), gi], -1, 2)
        s_s = jnp.einsum('bqgrd,bgqkd->bgrqk', qg, k_sel) + bias_s
        p_s = _masked_softmax(s_s, (rel_s >= 0)[:, :, None])
        o_s = jnp.einsum('bgrqk,bgqkd->bqgrd', p_s.astype(dt), v_sel)
        k_w = lax.dynamic_slice_in_dim(kw, qs, kw_len, axis=1)
        v_w = lax.dynamic_slice_in_dim(vw, qs, kw_len, axis=1)
        valid = win_mask & ((qs - WINDOW + jnp.arange(kw_len)) >= 0)[None, :]
        s_w = jnp.einsum('bqgrd,bkgd->bgrqk', qg, k_w) + win_bias
        p_w = _masked_softmax(s_w, valid)
        o_w = jnp.einsum('bgrqk,bkgd->bqgrd', p_w.astype(dt), v_w)
        gb = gb.reshape(B, Q_BLOCK, G, R, 3)
        o = gb[..., 0:1] * o_c + gb[..., 1:2] * o_s + gb[..., 2:3] * o_w
        return o.reshape(B, Q_BLOCK, H * hd)

    nqb = S // Q_BLOCK
    qbs = q.reshape(B, nqb, Q_BLOCK, H, hd).transpose(1, 0, 2, 3, 4)
    gbs = gates.reshape(B, nqb, Q_BLOCK, H, 3).transpose(1, 0, 2, 3, 4)
    starts = jnp.arange(nqb, dtype=jnp.int32) * Q_BLOCK
    out = lax.map(block, (qbs, gbs, starts))
    return out.transpose(1, 0, 2, 3).reshape(B, S, H * hd)


def _ssd(z, xbc, dt_raw, conv_w, conv_b, dt_bias, a_log, d_skip, norm_w):
    B, S, _ = z.shape
    f32 = jnp.float32
    G, R, P, N, L = SSD_GROUPS, SSD_HEADS // SSD_GROUPS, SSD_HEAD_DIM, SSD_STATE, CHUNK
    nc = S // L
    xbc = jax.nn.silu(_causal_conv(xbc, conv_w) + conv_b).astype(f32)
    xs, bm, cm = _split(xbc, (SSD_INNER, G * N, G * N))
    x = xs.reshape(B, nc, L, G, R, P)
    bm = bm.reshape(B, nc, L, G, N)
    cm = cm.reshape(B, nc, L, G, N)
    dt = jax.nn.softplus(dt_raw.astype(f32) + dt_bias.astype(f32)).reshape(B, nc, L, G, R)
    a_cum = jnp.cumsum(dt * (-jnp.exp(a_log.astype(f32))).reshape(G, R), axis=2)
    tril = (jnp.arange(L)[:, None] >= jnp.arange(L)[None, :])[:, :, None, None]
    seg = jnp.exp(jnp.where(tril, a_cum[:, :, :, None] - a_cum[:, :, None, :], -jnp.inf))
    xdt = x * dt[..., None]
    cb = jnp.einsum('bcign,bcjgn->bcijg', cm, bm)
    y = jnp.einsum('bcijg,bcijgr,bcjgrp->bcigrp', cb, seg, xdt)
    states = jnp.einsum('bcjgn,bcjgr,bcjgrp->bcgrpn', bm, jnp.exp(a_cum[:, :, -1:] - a_cum), xdt)
    prev = _chunk_state_scan(states, jnp.exp(a_cum[:, :, -1])[..., None, None])
    y = y + jnp.einsum('bcign,bcgrpn,bcigr->bcigrp', cm, prev, jnp.exp(a_cum))
    y = y + x * d_skip.astype(f32).reshape(G, R, 1)
    y = y.reshape(B, S, SSD_INNER) * jax.nn.silu(z.astype(f32))
    y = _rmsnorm(y.reshape(B, S, G, SSD_INNER // G), norm_w.reshape(G, SSD_INNER // G))
    return y.reshape(B, S, SSD_INNER).astype(z.dtype)


def _gdn(q, k, v, z, beta_raw, a_raw, conv_w, dt_bias, a_log, norm_w):
    B, S, _ = q.shape
    f32 = jnp.float32
    H, Dh, L = GDN_HEADS, GDN_HEAD_DIM, CHUNK
    nc = S // L
    qkv = jax.nn.silu(_causal_conv(jnp.concatenate([q, k, v], axis=-1), conv_w)).astype(f32)
    qq, kk, vv = _split(qkv, (GDN_WIDTH, GDN_WIDTH, GDN_WIDTH))
    qq = _l2norm(qq.reshape(B, S, H, Dh)) * Dh ** -0.5
    kk = _l2norm(kk.reshape(B, S, H, Dh))
    vv = vv.reshape(B, S, H, Dh)
    beta = jax.nn.sigmoid(beta_raw.astype(f32))
    g = -jnp.exp(a_log.astype(f32)) * jax.nn.softplus(a_raw.astype(f32) + dt_bias.astype(f32))
    chunk = lambda t: jnp.moveaxis(t.reshape(B, nc, L, H, *t.shape[3:]), 3, 2)
    qc, kc, vc, bc, gc = chunk(qq), chunk(kk), chunk(vv), chunk(beta), chunk(g)
    gcum = jnp.cumsum(gc, axis=-1)
    i = jnp.arange(L)
    tril = i[:, None] >= i[None, :]
    strict = i[:, None] > i[None, :]
    decay = jnp.exp(jnp.where(tril, gcum[..., :, None] - gcum[..., None, :], -jnp.inf))
    kb = kc * bc[..., None]
    a_mat = jnp.where(strict, jnp.einsum('bnhid,bnhjd->bnhij', kb, kc) * decay, 0.0)
    rhs = jnp.concatenate([vc * bc[..., None], kb * jnp.exp(gcum)[..., None]], axis=-1)
    sol = lax.linalg.triangular_solve(a_mat + jnp.eye(L, dtype=f32), rhs, left_side=True, lower=True,
                                      unit_diagonal=True)
    u, w = sol[..., :Dh], sol[..., Dh:]
    aqk = jnp.einsum('bnhid,bnhjd->bnhij', qc, kc) * decay
    q_dec = qc * jnp.exp(gcum)[..., None]
    k_end = kc * jnp.exp(gcum[..., -1:] - gcum)[..., None]
    d_last = jnp.exp(gcum[..., -1])

    def step(state, inp):
        qd, ke, uu, ww, aa, dl = inp
        v_new = uu - jnp.einsum('bhid,bhde->bhie', ww, state)
        o = jnp.einsum('bhid,bhde->bhie', qd, state) + jnp.einsum('bhij,bhje->bhie', aa, v_new)
        state = state * dl[..., None, None] + jnp.einsum('bhjd,bhje->bhde', ke, v_new)
        return state, o

    xs = tuple(jnp.moveaxis(t, 1, 0) for t in (q_dec, k_end, u, w, aqk, d_last))
    _, o = lax.scan(step, jnp.zeros((B, H, Dh, Dh), f32), xs)
    o = jnp.moveaxis(jnp.moveaxis(o, 0, 1), 3, 2).reshape(B, S, H, Dh)
    o = _rmsnorm(o, norm_w) * jax.nn.silu(z.astype(f32).reshape(B, S, H, Dh))
    return o.reshape(B, S, GDN_WIDTH).astype(q.dtype)


def _gla(q, k, v, g_out, g_lr, gate_w2, gate_b, norm_w):
    B, S, _ = q.shape
    f32 = jnp.float32
    H, Dk, Dv, L = GLA_HEADS, GLA_DK, GLA_DV, CHUNK
    nc = S // L
    gk = jax.nn.log_sigmoid((g_lr @ gate_w2 + gate_b).astype(f32)) / GLA_GATE_NORM
    chunk = lambda t, d: jnp.moveaxis(t.astype(f32).reshape(B, nc, L, H, d), 3, 2)
    qc = chunk(q, Dk) * Dk ** -0.5
    kc = chunk(k, Dk)
    vc = chunk(v, Dv)
    bcum = jnp.cumsum(chunk(gk, Dk), axis=3)
    q_dec = qc * jnp.exp(bcum)
    k_inv = kc * jnp.exp(-bcum)
    tril = jnp.arange(L)[:, None] >= jnp.arange(L)[None, :]
    attn = jnp.where(tril, jnp.einsum('bnhid,bnhjd->bnhij', q_dec, k_inv), 0.0)
    o = jnp.einsum('bnhij,bnhje->bnhie', attn, vc)
    k_end = kc * jnp.exp(bcum[..., -1:, :] - bcum)
    local = jnp.einsum('bnhjd,bnhje->bnhde', k_end, vc)
    prev = _chunk_state_scan(local, jnp.exp(bcum[..., -1, :])[..., None])
    o = o + jnp.einsum('bnhid,bnhde->bnhie', q_dec, prev)
    o = jnp.moveaxis(o, 2, 3).reshape(B, S, H, Dv)
    o = _rmsnorm(o, norm_w) * jax.nn.silu(g_out.astype(f32).reshape(B, S, H, Dv))
    return o.reshape(B, S, GLA_VAL).astype(q.dtype)


def _token_mixers(h, w_in, w_out, rel_bias, cmp_pos, cmp_w1, cmp_w2,
                  ssd_conv_w, ssd_conv_b, ssd_dt_bias, ssd_a_log, ssd_d, ssd_norm_w,
                  gdn_conv_w, gdn_dt_bias, gdn_a_log, gdn_norm_w, gla_gate_w2, gla_gate_b, gla_norm_w):
    B, S, _ = h.shape
    (nq, nkc, nvc, nks, nvs, nkw, nvw, ngate,
     sz, sxbc, sdt,
     gq, gk, gv, gz, gbeta, ga,
     lq, lk, lv, lg, llr) = _split(h @ w_in, IN_SPLITS)
    kv = lambda t: t.reshape(B, S, NSA_KV_GROUPS, NSA_HEAD_DIM)
    y_nsa = _nsa(nq.reshape(B, S, NSA_HEADS, NSA_HEAD_DIM), kv(nkc), kv(nvc), kv(nks), kv(nvs), kv(nkw), kv(nvw),
                 jax.nn.sigmoid(ngate).reshape(B, S, NSA_HEADS, 3), cmp_pos, cmp_w1, cmp_w2, rel_bias)
    y_ssd = _ssd(sz, sxbc, sdt, ssd_conv_w, ssd_conv_b, ssd_dt_bias, ssd_a_log, ssd_d, ssd_norm_w)
    y_gdn = _gdn(gq, gk, gv, gz, gbeta, ga, gdn_conv_w, gdn_dt_bias, gdn_a_log, gdn_norm_w)
    y_gla = _gla(lq, lk, lv, lg, llr, gla_gate_w2, gla_gate_b, gla_norm_w)
    return jnp.concatenate([y_nsa, y_ssd, y_gdn, y_gla], axis=-1) @ w_out


def setup_inputs(seed: int = 0) -> dict:
    key = jax.random.key(seed)
    ks = iter(jax.random.split(key, 40))
    f32 = jnp.float32
    D = D_MODEL

    def nrm(shape, s):
        return jax.random.normal(next(ks), shape, f32) * s

    def gain(shape):
        return 1.0 + nrm(shape, 0.02)

    def dt_bias(shape):
        dt = jnp.exp(jax.random.uniform(next(ks), shape, f32, math.log(1e-3), math.log(1e-1)))
        return dt + jnp.log(-jnp.expm1(-dt))

    def a_log(shape):
        return jnp.log(jax.random.uniform(next(ks), shape, f32, 1.0, 16.0))

    return {
        'x': nrm((BATCH, SEQ, D), 1.0),
        'c': nrm((BATCH, D), 1.0),
        'rel_bias': nrm((REL_BUCKETS, NSA_HEADS), 0.2),
        'norm1_w': gain((DEPTH, D)),
        'norm2_w': gain((DEPTH, D)),
        'ada_w': nrm((DEPTH, D, 6 * D), 0.3 * D ** -0.5),
        'ada_b': nrm((DEPTH, 6 * D), 0.01),
        'w_in': nrm((DEPTH, D, IN_COLS), D ** -0.5),
        'w_out': nrm((DEPTH, MIX_OUT, D), MIX_OUT ** -0.5),
        'nsa_cmp_pos': nrm((DEPTH, 2, CMP_BLOCK, NSA_HEAD_DIM), 0.1),
        'nsa_cmp_w1': nrm((DEPTH, 2, CMP_BLOCK * NSA_HEAD_DIM, NSA_HEAD_DIM), (CMP_BLOCK * NSA_HEAD_DIM) ** -0.5),
        'nsa_cmp_w2': nrm((DEPTH, 2, NSA_HEAD_DIM, NSA_HEAD_DIM), NSA_HEAD_DIM ** -0.5),
        'ssd_conv_w': nrm((DEPTH, SSD_CONV, SSD_XBC), SSD_CONV ** -0.5),
        'ssd_conv_b': nrm((DEPTH, SSD_XBC), 0.01),
        'ssd_dt_bias': dt_bias((DEPTH, SSD_HEADS)),
        'ssd_a_log': a_log((DEPTH, SSD_HEADS)),
        'ssd_d': gain((DEPTH, SSD_HEADS)),
        'ssd_norm_w': gain((DEPTH, SSD_INNER)),
        'gdn_conv_w': nrm((DEPTH, GDN_CONV, 3 * GDN_WIDTH), GDN_CONV ** -0.5),
        'gdn_dt_bias': dt_bias((DEPTH, GDN_HEADS)),
        'gdn_a_log': a_log((DEPTH, GDN_HEADS)),
        'gdn_norm_w': gain((DEPTH, GDN_HEAD_DIM)),
        'gla_gate_w2': nrm((DEPTH, GLA_GATE_RANK, GLA_KEY), GLA_GATE_RANK ** -0.5),
        'gla_gate_b': nrm((DEPTH, GLA_KEY), 0.01),
        'gla_norm_w': gain((DEPTH, GLA_DV)),
        'mlp_w1': nrm((DEPTH, D, MLP_HIDDEN), D ** -0.5),
        'mlp_w2': nrm((DEPTH, MLP_HIDDEN, D), MLP_HIDDEN ** -0.5),
        'final_norm_w': gain((D,)),
    }


def reference(x, c, rel_bias, norm1_w, norm2_w, ada_w, ada_b, w_in, w_out, nsa_cmp_pos, nsa_cmp_w1, nsa_cmp_w2,
              ssd_conv_w, ssd_conv_b, ssd_dt_bias, ssd_a_log, ssd_d, ssd_norm_w,
              gdn_conv_w, gdn_dt_bias, gdn_a_log, gdn_norm_w, gla_gate_w2, gla_gate_b, gla_norm_w,
              mlp_w1, mlp_w2, final_norm_w):
    B = x.shape[0]
    c_act = jax.nn.silu(c)
    for l in range(DEPTH):
        mod = (c_act @ ada_w[l] + ada_b[l]).reshape(B, 6, 1, D_MODEL)
        sh1, sc1, g1, sh2, sc2, g2 = (mod[:, i] for i in range(6))
        h = _rmsnorm(x, norm1_w[l]) * (1.0 + sc1) + sh1
        y = _token_mixers(h, w_in[l], w_out[l], rel_bias, nsa_cmp_pos[l], nsa_cmp_w1[l], nsa_cmp_w2[l],
                          ssd_conv_w[l], ssd_conv_b[l], ssd_dt_bias[l], ssd_a_log[l], ssd_d[l], ssd_norm_w[l],
                          gdn_conv_w[l], gdn_dt_bias[l], gdn_a_log[l], gdn_norm_w[l],
                          gla_gate_w2[l], gla_gate_b[l], gla_norm_w[l])
        x = x + g1 * y
        h = _rmsnorm(x, norm2_w[l]) * (1.0 + sc2) + sh2
        x = x + g2 * (jnp.square(jax.nn.relu(h @ mlp_w1[l])) @ mlp_w2[l])
    return _rmsnorm(x, final_norm_w)
```

```python
import math
from functools import partial

import jax
import jax.numpy as jnp
from jax import lax
from jax.experimental import pallas as pl
from jax.experimental.pallas import tpu as pltpu

D_MODEL = 2048
BATCH = 16
SEQ = 2048
DEPTH = 4

MIX_GROUP = D_MODEL // 4
NSA_HEAD_DIM = 64
NSA_HEADS = MIX_GROUP // NSA_HEAD_DIM
NSA_KV_GROUPS = max(1, NSA_HEADS // 4)
NSA_KV = NSA_KV_GROUPS * NSA_HEAD_DIM
CMP_BLOCK = 32
CMP_STRIDE = 16
SLC_BLOCK = 64
SLC_TOP_N = 8
WINDOW = 512
Q_BLOCK = 128
REL_BUCKETS = 32
REL_MAX_DIST = 128
SSD_HEAD_DIM = 64
SSD_HEADS = MIX_GROUP // SSD_HEAD_DIM
SSD_INNER = SSD_HEADS * SSD_HEAD_DIM
SSD_GROUPS = 2
SSD_STATE = 128
SSD_CONV = 4
SSD_XBC = SSD_INNER + 2 * SSD_GROUPS * SSD_STATE
GDN_HEAD_DIM = 128
GDN_HEADS = MIX_GROUP // GDN_HEAD_DIM
GDN_WIDTH = GDN_HEADS * GDN_HEAD_DIM
GDN_CONV = 4
GLA_DV = 128
GLA_HEADS = MIX_GROUP // GLA_DV
GLA_DK = GLA_DV // 2
GLA_KEY = GLA_HEADS * GLA_DK
GLA_VAL = GLA_HEADS * GLA_DV
GLA_GATE_RANK = 16
GLA_GATE_NORM = 16.0
CHUNK = 64
MLP_HIDDEN = 4 * D_MODEL
EPS = 1e-6
NEG_INF = -1e30
FORCE_SCORE = 1e9
IN_SPLITS = (NSA_HEADS * NSA_HEAD_DIM, NSA_KV, NSA_KV, NSA_KV, NSA_KV, NSA_KV, NSA_KV, NSA_HEADS * 3,
             SSD_INNER, SSD_XBC, SSD_HEADS,
             GDN_WIDTH, GDN_WIDTH, GDN_WIDTH, GDN_WIDTH, GDN_HEADS, GDN_HEADS,
             GLA_KEY, GLA_KEY, GLA_VAL, GLA_VAL, GLA_GATE_RANK)
IN_COLS = sum(IN_SPLITS)
MIX_OUT = NSA_HEADS * NSA_HEAD_DIM + SSD_INNER + GDN_WIDTH + GLA_VAL

LANE = 128
VMEM_LIMIT = 56 * 1024 * 1024
IN_COLS_PAD = -(-IN_COLS // 512) * 512
TOKENS = BATCH * SEQ

_bf16 = jnp.bfloat16
_f32 = jnp.float32


def _ada_kernel(c_ref, w_ref, b_ref, o_ref):
    c = c_ref[...]
    c_act = c * jax.nn.sigmoid(c)
    o_ref[0] = jnp.dot(c_act, w_ref[0], preferred_element_type=_f32) + b_ref[0]


def _ada_all(c, ada_w, ada_b):
    tn = 1024
    return pl.pallas_call(
        _ada_kernel,
        out_shape=jax.ShapeDtypeStruct((DEPTH, BATCH, 6 * D_MODEL), _f32),
        grid=(DEPTH, 6 * D_MODEL // tn),
        in_specs=[pl.BlockSpec((BATCH, D_MODEL), lambda l, j: (0, 0)),
                  pl.BlockSpec((1, D_MODEL, tn), lambda l, j: (l, 0, j)),
                  pl.BlockSpec((1, 1, tn), lambda l, j: (l, 0, j))],
        out_specs=pl.BlockSpec((1, BATCH, tn), lambda l, j: (l, 0, j)),
        compiler_params=pltpu.CompilerParams(dimension_semantics=("parallel", "parallel"),
                                             vmem_limit_bytes=VMEM_LIMIT),
        name="ada_mod",
    )(c, ada_w, ada_b.reshape(DEPTH, 1, 6 * D_MODEL))


def _rms_mod(x, w, sc, sh):
    y = x * lax.rsqrt(jnp.mean(x * x, axis=-1, keepdims=True) + EPS)
    return (y * w) * (1.0 + sc) + sh


def _norm_mod_kernel(x_ref, w_ref, sc_ref, sh_ref, o_ref):
    o_ref[...] = _rms_mod(x_ref[...], w_ref[...], sc_ref[0], sh_ref[0]).astype(o_ref.dtype)


def _norm_mod(x, w, sc, sh, tm=512):
    per_b = SEQ // tm
    return pl.pallas_call(
        _norm_mod_kernel,
        out_shape=jax.ShapeDtypeStruct((TOKENS, D_MODEL), _bf16),
        grid=(TOKENS // tm,),
        in_specs=[pl.BlockSpec((tm, D_MODEL), lambda i: (i, 0)),
                  pl.BlockSpec((1, D_MODEL), lambda i: (0, 0)),
                  pl.BlockSpec((1, 1, D_MODEL), lambda i: (i // per_b, 0, 0)),
                  pl.BlockSpec((1, 1, D_MODEL), lambda i: (i // per_b, 0, 0))],
        out_specs=pl.BlockSpec((tm, D_MODEL), lambda i: (i, 0)),
        compiler_params=pltpu.CompilerParams(dimension_semantics=("parallel",),
                                             vmem_limit_bytes=VMEM_LIMIT),
        name="norm_mod",
    )(x, w, sc, sh)


def _matmul_kernel(a_ref, w_ref, o_ref):
    o_ref[...] = jnp.dot(a_ref[...], w_ref[...], preferred_element_type=_f32).astype(o_ref.dtype)


def _matmul(a, w, tm=1024, tn=512, out_dtype=_f32):
    M, K = a.shape
    N = w.shape[1]
    return pl.pallas_call(
        _matmul_kernel,
        out_shape=jax.ShapeDtypeStruct((M, N), out_dtype),
        grid=(M // tm, N // tn),
        in_specs=[pl.BlockSpec((tm, K), lambda i, j: (i, 0)),
                  pl.BlockSpec((K, tn), lambda i, j: (0, j))],
        out_specs=pl.BlockSpec((tm, tn), lambda i, j: (i, j)),
        compiler_params=pltpu.CompilerParams(dimension_semantics=("parallel", "parallel"),
                                             vmem_limit_bytes=VMEM_LIMIT),
        name="in_proj",
    )(a, w)


def _out_proj_kernel(a_ref, w_ref, x_ref, g_ref, nw_ref, sc_ref, sh_ref, xo_ref, ho_ref):
    y = jnp.dot(a_ref[...], w_ref[...], preferred_element_type=_f32)
    xn = x_ref[...] + g_ref[0] * y
    xo_ref[...] = xn
    ho_ref[...] = _rms_mod(xn, nw_ref[...], sc_ref[0], sh_ref[0]).astype(ho_ref.dtype)


def _out_proj(a, w, x, g, nw, sc, sh, tm=512):
    per_b = SEQ // tm
    bspec = pl.BlockSpec((1, 1, D_MODEL), lambda i: (i // per_b, 0, 0))
    return pl.pallas_call(
        _out_proj_kernel,
        out_shape=(jax.ShapeDtypeStruct((TOKENS, D_MODEL), _f32),
                   jax.ShapeDtypeStruct((TOKENS, D_MODEL), _bf16)),
        grid=(TOKENS // tm,),
        in_specs=[pl.BlockSpec((tm, MIX_OUT), lambda i: (i, 0)),
                  pl.BlockSpec((MIX_OUT, D_MODEL), lambda i: (0, 0)),
                  pl.BlockSpec((tm, D_MODEL), lambda i: (i, 0)),
                  bspec,
                  pl.BlockSpec((1, D_MODEL), lambda i: (0, 0)),
                  bspec, bspec],
        out_specs=(pl.BlockSpec((tm, D_MODEL), lambda i: (i, 0)),
                   pl.BlockSpec((tm, D_MODEL), lambda i: (i, 0))),
        compiler_params=pltpu.CompilerParams(dimension_semantics=("parallel",),
                                             vmem_limit_bytes=VMEM_LIMIT),
        name="out_proj",
    )(a, w, x, g, nw, sc, sh)


def _mlp_kernel(h_ref, w1_ref, w2_ref, x_ref, g_ref, o_ref, acc_ref):
    j = pl.program_id(1)

    @pl.when(j == 0)
    def _():
        acc_ref[...] = jnp.zeros_like(acc_ref)

    u = jnp.dot(h_ref[...], w1_ref[...], preferred_element_type=_f32)
    u = jnp.square(jnp.maximum(u, 0.0)).astype(_bf16)
    acc_ref[...] += jnp.dot(u, w2_ref[...], preferred_element_type=_f32)

    @pl.when(j == pl.num_programs(1) - 1)
    def _():
        o_ref[...] = x_ref[...] + g_ref[0] * acc_ref[...]


def _mlp(h, w1, w2, x, g, tm=512, th=512):
    per_b = SEQ // tm
    return pl.pallas_call(
        _mlp_kernel,
        out_shape=jax.ShapeDtypeStruct((TOKENS, D_MODEL), _f32),
        grid=(TOKENS // tm, MLP_HIDDEN // th),
        in_specs=[pl.BlockSpec((tm, D_MODEL), lambda i, j: (i, 0)),
                  pl.BlockSpec((D_MODEL, th), lambda i, j: (0, j)),
                  pl.BlockSpec((th, D_MODEL), lambda i, j: (j, 0)),
                  pl.BlockSpec((tm, D_MODEL), lambda i, j: (i, 0)),
                  pl.BlockSpec((1, 1, D_MODEL), lambda i, j: (i // per_b, 0, 0))],
        out_specs=pl.BlockSpec((tm, D_MODEL), lambda i, j: (i, 0)),
        scratch_shapes=[pltpu.VMEM((tm, D_MODEL), _f32)],
        compiler_params=pltpu.CompilerParams(dimension_semantics=("parallel", "arbitrary"),
                                             vmem_limit_bytes=VMEM_LIMIT),
        name="mlp",
    )(h, w1, w2, x, g)


def _final_norm_kernel(x_ref, w_ref, o_ref):
    x = x_ref[...]
    o_ref[...] = x * lax.rsqrt(jnp.mean(x * x, axis=-1, keepdims=True) + EPS) * w_ref[...]


def _final_norm(x, w, tm=512):
    return pl.pallas_call(
        _final_norm_kernel,
        out_shape=jax.ShapeDtypeStruct((TOKENS, D_MODEL), _f32),
        grid=(TOKENS // tm,),
        in_specs=[pl.BlockSpec((tm, D_MODEL), lambda i: (i, 0)),
                  pl.BlockSpec((1, D_MODEL), lambda i: (0, 0))],
        out_specs=pl.BlockSpec((tm, D_MODEL), lambda i: (i, 0)),
        compiler_params=pltpu.CompilerParams(dimension_semantics=("parallel",),
                                             vmem_limit_bytes=VMEM_LIMIT),
        name="final_norm",
    )(x, w)


def _split(t, sizes):
    cuts, acc = [], 0
    for s in sizes[:-1]:
        acc += s
        cuts.append(acc)
    return jnp.split(t, cuts, axis=-1)


def _rmsnorm(x, w):
    xf = x.astype(jnp.float32)
    y = xf * lax.rsqrt(jnp.mean(xf * xf, axis=-1, keepdims=True) + EPS)
    return (y * w.astype(jnp.float32)).astype(x.dtype)


def _l2norm(x):
    return x * lax.rsqrt(jnp.sum(x * x, axis=-1, keepdims=True) + EPS)


def _causal_conv(x, w):
    K, C = w.shape
    return lax.conv_general_dilated(x, w.astype(x.dtype)[:, None, :], window_strides=(1,), padding=((K - 1, 0),),
                                    dimension_numbers=('NWC', 'WIO', 'NWC'), feature_group_count=C)


def _masked_softmax(logits, mask):
    p = jax.nn.softmax(jnp.where(mask, logits.astype(jnp.float32), NEG_INF), axis=-1)
    return p * mask


def _t5_bucket(rel):
    n = jnp.maximum(rel, 0)
    exact = REL_BUCKETS // 2
    large = exact + (jnp.log(jnp.maximum(n, 1).astype(jnp.float32) / exact)
                     / math.log(REL_MAX_DIST / exact) * (REL_BUCKETS - exact)).astype(jnp.int32)
    return jnp.where(n < exact, n, jnp.minimum(large, REL_BUCKETS - 1))


def _rel_bias(table, rel, G):
    b = jnp.moveaxis(table[_t5_bucket(rel)], -1, 0)
    return b.reshape(G, table.shape[1] // G, *rel.shape)


def _chunk_state_scan(local, decay):
    def step(state, inp):
        u, a = inp
        return a * state + u, state
    lt = jnp.moveaxis(local, 1, 0)
    dt = jnp.moveaxis(decay, 1, 0)
    _, prev = lax.scan(step, jnp.zeros_like(lt[0]), (lt, dt))
    return jnp.moveaxis(prev, 0, 1)


def _nsa_compress(t, pos, w1, w2):
    B, S, G, hd = t.shape
    r = CMP_BLOCK // CMP_STRIDE
    ch = t.reshape(B, S // CMP_STRIDE, CMP_STRIDE, G, hd)
    n = S // CMP_STRIDE - r + 1
    blk = jnp.concatenate([ch[:, i:i + n] for i in range(r)], axis=2) + pos[:, None, :]
    blk = jnp.swapaxes(blk, 2, 3).reshape(B, n, G, CMP_BLOCK * hd)
    return jax.nn.silu(blk @ w1) @ w2


def _nsa(q, k_cmp, v_cmp, k_slc, v_slc, k_win, v_win, gates, cmp_pos, cmp_w1, cmp_w2, rel_bias):
    B, S, H, hd = q.shape
    G = NSA_KV_GROUPS
    R = H // G
    dt = q.dtype
    scale = hd ** -0.5
    kc = _nsa_compress(k_cmp, cmp_pos[0], cmp_w1[0], cmp_w2[0])
    vc = _nsa_compress(v_cmp, cmp_pos[1], cmp_w1[1], cmp_w2[1])
    nc = kc.shape[1]
    cmp_start = jnp.arange(nc) * CMP_STRIDE
    cmp_end = cmp_start + CMP_BLOCK - 1
    nsb = S // SLC_BLOCK
    top_n = min(SLC_TOP_N, nsb)
    sel_start = jnp.arange(nsb) * SLC_BLOCK
    overlap = ((cmp_start[:, None] <= sel_start[None, :] + SLC_BLOCK - 1)
               & (cmp_end[:, None] >= sel_start[None, :])).astype(jnp.float32)
    ks = k_slc.reshape(B, nsb, SLC_BLOCK, G, hd).transpose(0, 3, 1, 2, 4)
    vs = v_slc.reshape(B, nsb, SLC_BLOCK, G, hd).transpose(0, 3, 1, 2, 4)
    kw = jnp.pad(k_win, ((0, 0), (WINDOW, 0), (0, 0), (0, 0)))
    vw = jnp.pad(v_win, ((0, 0), (WINDOW, 0), (0, 0), (0, 0)))
    kw_len = Q_BLOCK + WINDOW
    win_rel = jnp.arange(Q_BLOCK)[:, None] + WINDOW - jnp.arange(kw_len)[None, :]
    win_mask = (win_rel >= 0) & (win_rel < WINDOW)
    win_bias = _rel_bias(rel_bias, win_rel, G)
    table = rel_bias.reshape(REL_BUCKETS, G, R)
    bi = jnp.arange(B)[:, None, None, None]
    gi = jnp.arange(G)[None, :, None, None]

    def block(args):
        qb, gb, qs = args
        t = qs + jnp.arange(Q_BLOCK)
        qg = (qb * scale).reshape(B, Q_BLOCK, G, R, hd)
        rel_c = t[:, None] - cmp_end[None, :]
        s_c = jnp.einsum('bqgrd,bkgd->bgrqk', qg, kc) + _rel_bias(rel_bias, rel_c, G)
        p_c = _masked_softmax(s_c, rel_c >= 0)
        o_c = jnp.einsum('bgrqk,bkgd->bqgrd', p_c.astype(dt), vc)
        imp = jnp.einsum('bgrqk,kj->bgqj', p_c, overlap)
        blk = jnp.arange(nsb)[None, :]
        cur = (t // SLC_BLOCK)[:, None]
        forced = (blk == 0) | (blk == cur) | (blk == cur - 1)
        imp = jnp.where(forced, FORCE_SCORE, imp)
        imp = jnp.where(blk <= cur, imp, NEG_INF)
        _, idx = lax.top_k(imp, top_n)
        k_sel = ks[bi, gi, idx].reshape(B, G, Q_BLOCK, top_n * SLC_BLOCK, hd)
        v_sel = vs[bi, gi, idx].reshape(B, G, Q_BLOCK, top_n * SLC_BLOCK, hd)
        pos = (idx[..., None] * SLC_BLOCK + jnp.arange(SLC_BLOCK)).reshape(B, G, Q_BLOCK, top_n * SLC_BLOCK)
        rel_s = t[:, None] - pos
        bias_s = jnp.moveaxis(table[_t5_bucket(rel_s), gi], -1, 2)
        s_s = jnp.einsum('bqgrd,bgqkd->bgrqk', qg, k_sel) + bias_s
        p_s = _masked_softmax(s_s, (rel_s >= 0)[:, :, None])
        o_s = jnp.einsum('bgrqk,bgqkd->bqgrd', p_s.astype(dt), v_sel)
        k_w = lax.dynamic_slice_in_dim(kw, qs, kw_len, axis=1)
        v_w = lax.dynamic_slice_in_dim(vw, qs, kw_len, axis=1)
        valid = win_mask & ((qs - WINDOW + jnp.arange(kw_len)) >= 0)[None, :]
        s_w = jnp.einsum('bqgrd,bkgd->bgrqk', qg, k_w) + win_bias
        p_w = _masked_softmax(s_w, valid)
        o_w = jnp.einsum('bgrqk,bkgd->bqgrd', p_w.astype(dt), v_w)
        gb = gb.reshape(B, Q_BLOCK, G, R, 3)
        o = gb[..., 0:1] * o_c + gb[..., 1:2] * o_s + gb[..., 2:3] * o_w
        return o.reshape(B, Q_BLOCK, H * hd)

    nqb = S // Q_BLOCK
    qbs = q.reshape(B, nqb, Q_BLOCK, H, hd).transpose(1, 0, 2, 3, 4)
    gbs = gates.reshape(B, nqb, Q_BLOCK, H, 3).transpose(1, 0, 2, 3, 4)
    starts = jnp.arange(nqb, dtype=jnp.int32) * Q_BLOCK
    out = lax.map(block, (qbs, gbs, starts))
    return out.transpose(1, 0, 2, 3).reshape(B, S, H * hd)


def _ssd(z, xbc, dt_raw, conv_w, conv_b, dt_bias, a_log, d_skip, norm_w):
    B, S, _ = z.shape
    f32 = jnp.float32
    G, R, P, N, L = SSD_GROUPS, SSD_HEADS // SSD_GROUPS, SSD_HEAD_DIM, SSD_STATE, CHUNK
    nc = S // L
    xbc = jax.nn.silu(_causal_conv(xbc, conv_w) + conv_b).astype(f32)
    xs, bm, cm = _split(xbc, (SSD_INNER, G * N, G * N))
    x = xs.reshape(B, nc, L, G, R, P)
    bm = bm.reshape(B, nc, L, G, N)
    cm = cm.reshape(B, nc, L, G, N)
    dt = jax.nn.softplus(dt_raw.astype(f32) + dt_bias.astype(f32)).reshape(B, nc, L, G, R)
    a_cum = jnp.cumsum(dt * (-jnp.exp(a_log.astype(f32))).reshape(G, R), axis=2)
    tril = (jnp.arange(L)[:, None] >= jnp.arange(L)[None, :])[:, :, None, None]
    seg = jnp.exp(jnp.where(tril, a_cum[:, :, :, None] - a_cum[:, :, None, :], -jnp.inf))
    xdt = x * dt[..., None]
    cb = jnp.einsum('bcign,bcjgn->bcijg', cm, bm)
    y = jnp.einsum('bcijg,bcijgr,bcjgrp->bcigrp', cb, seg, xdt)
    states = jnp.einsum('bcjgn,bcjgr,bcjgrp->bcgrpn', bm, jnp.exp(a_cum[:, :, -1:] - a_cum), xdt)
    prev = _chunk_state_scan(states, jnp.exp(a_cum[:, :, -1])[..., None, None])
    y = y + jnp.einsum('bcign,bcgrpn,bcigr->bcigrp', cm, prev, jnp.exp(a_cum))
    y = y + x * d_skip.astype(f32).reshape(G, R, 1)
    y = y.reshape(B, S, SSD_INNER) * jax.nn.silu(z.astype(f32))
    y = _rmsnorm(y.reshape(B, S, G, SSD_INNER // G), norm_w.reshape(G, SSD_INNER // G))
    return y.reshape(B, S, SSD_INNER).astype(z.dtype)


def _gdn(q, k, v, z, beta_raw, a_raw, conv_w, dt_bias, a_log, norm_w):
    B, S, _ = q.shape
    f32 = jnp.float32
    H, Dh, L = GDN_HEADS, GDN_HEAD_DIM, CHUNK
    nc = S // L
    qkv = jax.nn.silu(_causal_conv(jnp.concatenate([q, k, v], axis=-1), conv_w)).astype(f32)
    qq, kk, vv = _split(qkv, (GDN_WIDTH, GDN_WIDTH, GDN_WIDTH))
    qq = _l2norm(qq.reshape(B, S, H, Dh)) * Dh ** -0.5
    kk = _l2norm(kk.reshape(B, S, H, Dh))
    vv = vv.reshape(B, S, H, Dh)
    beta = jax.nn.sigmoid(beta_raw.astype(f32))
    g = -jnp.exp(a_log.astype(f32)) * jax.nn.softplus(a_raw.astype(f32) + dt_bias.astype(f32))
    chunk = lambda t: jnp.moveaxis(t.reshape(B, nc, L, H, *t.shape[3:]), 3, 2)
    qc, kc, vc, bc, gc = chunk(qq), chunk(kk), chunk(vv), chunk(beta), chunk(g)
    gcum = jnp.cumsum(gc, axis=-1)
    i = jnp.arange(L)
    tril = i[:, None] >= i[None, :]
    strict = i[:, None] > i[None, :]
    decay = jnp.exp(jnp.where(tril, gcum[..., :, None] - gcum[..., None, :], -jnp.inf))
    kb = kc * bc[..., None]
    a_mat = jnp.where(strict, jnp.einsum('bnhid,bnhjd->bnhij', kb, kc) * decay, 0.0)
    rhs = jnp.concatenate([vc * bc[..., None], kb * jnp.exp(gcum)[..., None]], axis=-1)
    sol = lax.linalg.triangular_solve(a_mat + jnp.eye(L, dtype=f32), rhs, left_side=True, lower=True,
                                      unit_diagonal=True)
    u, w = sol[..., :Dh], sol[..., Dh:]
    aqk = jnp.einsum('bnhid,bnhjd->bnhij', qc, kc) * decay
    q_dec = qc * jnp.exp(gcum)[..., None]
    k_end = kc * jnp.exp(gcum[..., -1:] - gcum)[..., None]
    d_last = jnp.exp(gcum[..., -1])

    def step(state, inp):
        qd, ke, uu, ww, aa, dl = inp
        v_new = uu - jnp.einsum('bhid,bhde->bhie', ww, state)
        o = jnp.einsum('bhid,bhde->bhie', qd, state) + jnp.einsum('bhij,bhje->bhie', aa, v_new)
        state = state * dl[..., None, None] + jnp.einsum('bhjd,bhje->bhde', ke, v_new)
        return state, o

    xs = tuple(jnp.moveaxis(t, 1, 0) for t in (q_dec, k_end, u, w, aqk, d_last))
    _, o = lax.scan(step, jnp.zeros((B, H, Dh, Dh), f32), xs)
    o = jnp.moveaxis(jnp.moveaxis(o, 0, 1), 3, 2).reshape(B, S, H, Dh)
    o = _rmsnorm(o, norm_w) * jax.nn.silu(z.astype(f32).reshape(B, S, H, Dh))
    return o.reshape(B, S, GDN_WIDTH).astype(q.dtype)


def _gla(q, k, v, g_out, g_lr, gate_w2, gate_b, norm_w):
    B, S, _ = q.shape
    f32 = jnp.float32
    H, Dk, Dv, L = GLA_HEADS, GLA_DK, GLA_DV, CHUNK
    nc = S // L
    gk = jax.nn.log_sigmoid((g_lr @ gate_w2 + gate_b).astype(f32)) / GLA_GATE_NORM
    chunk = lambda t, d: jnp.moveaxis(t.astype(f32).reshape(B, nc, L, H, d), 3, 2)
    qc = chunk(q, Dk) * Dk ** -0.5
    kc = chunk(k, Dk)
    vc = chunk(v, Dv)
    bcum = jnp.cumsum(chunk(gk, Dk), axis=3)
    q_dec = qc * jnp.exp(bcum)
    k_inv = kc * jnp.exp(-bcum)
    tril = jnp.arange(L)[:, None] >= jnp.arange(L)[None, :]
    attn = jnp.where(tril, jnp.einsum('bnhid,bnhjd->bnhij', q_dec, k_inv), 0.0)
    o = jnp.einsum('bnhij,bnhje->bnhie', attn, vc)
    k_end = kc * jnp.exp(bcum[..., -1:, :] - bcum)
    local = jnp.einsum('bnhjd,bnhje->bnhde', k_end, vc)
    prev = _chunk_state_scan(local, jnp.exp(bcum[..., -1, :])[..., None])
    o = o + jnp.einsum('bnhid,bnhde->bnhie', q_dec, prev)
    o = jnp.moveaxis(o, 2, 3).reshape(B, S, H, Dv)
    o = _rmsnorm(o, norm_w) * jax.nn.silu(g_out.astype(f32).reshape(B, S, H, Dv))
    return o.reshape(B, S, GLA_VAL).astype(q.dtype)


def _token_mixers(proj, rel_bias, cmp_pos, cmp_w1, cmp_w2,
                  ssd_conv_w, ssd_conv_b, ssd_dt_bias, ssd_a_log, ssd_d, ssd_norm_w,
                  gdn_conv_w, gdn_dt_bias, gdn_a_log, gdn_norm_w, gla_gate_w2, gla_gate_b, gla_norm_w):
    B, S, _ = proj.shape
    (nq, nkc, nvc, nks, nvs, nkw, nvw, ngate,
     sz, sxbc, sdt,
     gq, gk, gv, gz, gbeta, ga,
     lq, lk, lv, lg, llr) = _split(proj, IN_SPLITS)
    kv = lambda t: t.reshape(B, S, NSA_KV_GROUPS, NSA_HEAD_DIM)
    y_nsa = _nsa(nq.reshape(B, S, NSA_HEADS, NSA_HEAD_DIM), kv(nkc), kv(nvc), kv(nks), kv(nvs), kv(nkw), kv(nvw),
                 jax.nn.sigmoid(ngate).reshape(B, S, NSA_HEADS, 3), cmp_pos, cmp_w1, cmp_w2, rel_bias)
    y_ssd = _ssd(sz, sxbc, sdt, ssd_conv_w, ssd_conv_b, ssd_dt_bias, ssd_a_log, ssd_d, ssd_norm_w)
    y_gdn = _gdn(gq, gk, gv, gz, gbeta, ga, gdn_conv_w, gdn_dt_bias, gdn_a_log, gdn_norm_w)
    y_gla = _gla(lq, lk, lv, lg, llr, gla_gate_w2, gla_gate_b, gla_norm_w)
    return jnp.concatenate([y_nsa, y_ssd, y_gdn, y_gla], axis=-1)


def kernel(x, c, rel_bias, norm1_w, norm2_w, ada_w, ada_b, w_in, w_out, nsa_cmp_pos, nsa_cmp_w1, nsa_cmp_w2, ssd_conv_w, ssd_conv_b, ssd_dt_bias, ssd_a_log, ssd_d, ssd_norm_w, gdn_conv_w, gdn_dt_bias, gdn_a_log, gdn_norm_w, gla_gate_w2, gla_gate_b, gla_norm_w, mlp_w1, mlp_w2, final_norm_w):
    B, S, D = x.shape
    mod = _ada_all(c, ada_w, ada_b).reshape(DEPTH, B, 6, 1, D)
    w_in_b = jnp.pad(w_in, ((0, 0), (0, 0), (0, IN_COLS_PAD - IN_COLS))).astype(_bf16)
    w_out_b = w_out.astype(_bf16)
    w1_b = mlp_w1.astype(_bf16)
    w2_b = mlp_w2.astype(_bf16)
    xf = x.reshape(TOKENS, D)
    for l in range(DEPTH):
        sh1, sc1, g1, sh2, sc2, g2 = (mod[l, :, i] for i in range(6))
        h = _norm_mod(xf, norm1_w[l][None], sc1, sh1)
        proj = _matmul(h, w_in_b[l])[:, :IN_COLS].reshape(B, S, IN_COLS)
        y = _token_mixers(proj, rel_bias, nsa_cmp_pos[l], nsa_cmp_w1[l], nsa_cmp_w2[l],
                          ssd_conv_w[l], ssd_conv_b[l], ssd_dt_bias[l], ssd_a_log[l], ssd_d[l], ssd_norm_w[l],
                          gdn_conv_w[l], gdn_dt_bias[l], gdn_a_log[l], gdn_norm_w[l],
                          gla_gate_w2[l], gla_gate_b[l], gla_norm_w[l])
        xf, h2 = _out_proj(y.reshape(TOKENS, MIX_OUT).astype(_bf16), w_out_b[l], xf, g1,
                           norm2_w[l][None], sc2, sh2)
        xf = _mlp(h2, w1_b[l], w2_b[l], xf, g2)
    return _final_norm(xf, final_norm_w[None]).reshape(B, S, D)
```

```python
import math
from functools import partial

import jax
import jax.numpy as jnp
from jax import lax
from jax.experimental import pallas as pl
from jax.experimental.pallas import tpu as pltpu

D_MODEL = 2048
BATCH = 16
SEQ = 2048
DEPTH = 4

MIX_GROUP = D_MODEL // 4
NSA_HEAD_DIM = 64
NSA_HEADS = MIX_GROUP // NSA_HEAD_DIM
NSA_KV_GROUPS = max(1, NSA_HEADS // 4)
NSA_KV = NSA_KV_GROUPS * NSA_HEAD_DIM
CMP_BLOCK = 32
CMP_STRIDE = 16
SLC_BLOCK = 64
SLC_TOP_N = 8
WINDOW = 512
Q_BLOCK = 128
REL_BUCKETS = 32
REL_MAX_DIST = 128
SSD_HEAD_DIM = 64
SSD_HEADS = MIX_GROUP // SSD_HEAD_DIM
SSD_INNER = SSD_HEADS * SSD_HEAD_DIM
SSD_GROUPS = 2
SSD_STATE = 128
SSD_CONV = 4
SSD_XBC = SSD_INNER + 2 * SSD_GROUPS * SSD_STATE
GDN_HEAD_DIM = 128
GDN_HEADS = MIX_GROUP // GDN_HEAD_DIM
GDN_WIDTH = GDN_HEADS * GDN_HEAD_DIM
GDN_CONV = 4
GLA_DV = 128
GLA_HEADS = MIX_GROUP // GLA_DV
GLA_DK = GLA_DV // 2
GLA_KEY = GLA_HEADS * GLA_DK
GLA_VAL = GLA_HEADS * GLA_DV
GLA_GATE_RANK = 16
GLA_GATE_NORM = 16.0
CHUNK = 64
MLP_HIDDEN = 4 * D_MODEL
EPS = 1e-6
NEG_INF = -1e30
FORCE_SCORE = 1e9
IN_SPLITS = (NSA_HEADS * NSA_HEAD_DIM, NSA_KV, NSA_KV, NSA_KV, NSA_KV, NSA_KV, NSA_KV, NSA_HEADS * 3,
             SSD_INNER, SSD_XBC, SSD_HEADS,
             GDN_WIDTH, GDN_WIDTH, GDN_WIDTH, GDN_WIDTH, GDN_HEADS, GDN_HEADS,
             GLA_KEY, GLA_KEY, GLA_VAL, GLA_VAL, GLA_GATE_RANK)
IN_COLS = sum(IN_SPLITS)
MIX_OUT = NSA_HEADS * NSA_HEAD_DIM + SSD_INNER + GDN_WIDTH + GLA_VAL

LANE = 128
VMEM_LIMIT = 56 * 1024 * 1024
TOKENS = BATCH * SEQ
PROJ_B_COLS = IN_COLS - NSA_HEADS * NSA_HEAD_DIM - 4 * NSA_KV
PROJ_B_PAD = -(-PROJ_B_COLS // 512) * 512

_bf16 = jnp.bfloat16
_f32 = jnp.float32


def _ada_kernel(c_ref, w_ref, b_ref, o_ref):
    c = c_ref[...]
    c_act = c * jax.nn.sigmoid(c)
    o_ref[0] = jnp.dot(c_act, w_ref[0], preferred_element_type=_f32) + b_ref[0]


def _ada_all(c, ada_w, ada_b):
    tn = 1024
    return pl.pallas_call(
        _ada_kernel,
        out_shape=jax.ShapeDtypeStruct((DEPTH, BATCH, 6 * D_MODEL), _f32),
        grid=(DEPTH, 6 * D_MODEL // tn),
        in_specs=[pl.BlockSpec((BATCH, D_MODEL), lambda l, j: (0, 0)),
                  pl.BlockSpec((1, D_MODEL, tn), lambda l, j: (l, 0, j)),
                  pl.BlockSpec((1, 1, tn), lambda l, j: (l, 0, j))],
        out_specs=pl.BlockSpec((1, BATCH, tn), lambda l, j: (l, 0, j)),
        compiler_params=pltpu.CompilerParams(dimension_semantics=("parallel", "parallel"),
                                             vmem_limit_bytes=VMEM_LIMIT),
        name="ada_mod",
    )(c, ada_w, ada_b.reshape(DEPTH, 1, 6 * D_MODEL))


def _rms_mod(x, w, sc, sh):
    y = x * lax.rsqrt(jnp.mean(x * x, axis=-1, keepdims=True) + EPS)
    return (y * w) * (1.0 + sc) + sh


def _norm_mod_kernel(x_ref, w_ref, sc_ref, sh_ref, o_ref):
    o_ref[...] = _rms_mod(x_ref[...], w_ref[...], sc_ref[0], sh_ref[0]).astype(o_ref.dtype)


def _norm_mod(x, w, sc, sh, tm=512):
    per_b = SEQ // tm
    return pl.pallas_call(
        _norm_mod_kernel,
        out_shape=jax.ShapeDtypeStruct((TOKENS, D_MODEL), _bf16),
        grid=(TOKENS // tm,),
        in_specs=[pl.BlockSpec((tm, D_MODEL), lambda i: (i, 0)),
                  pl.BlockSpec((1, D_MODEL), lambda i: (0, 0)),
                  pl.BlockSpec((1, 1, D_MODEL), lambda i: (i // per_b, 0, 0)),
                  pl.BlockSpec((1, 1, D_MODEL), lambda i: (i // per_b, 0, 0))],
        out_specs=pl.BlockSpec((tm, D_MODEL), lambda i: (i, 0)),
        compiler_params=pltpu.CompilerParams(dimension_semantics=("parallel",),
                                             vmem_limit_bytes=VMEM_LIMIT),
        name="norm_mod",
    )(x, w, sc, sh)


def _matmul_kernel(a_ref, w_ref, o_ref):
    o_ref[...] = jnp.dot(a_ref[...], w_ref[...], preferred_element_type=_f32).astype(o_ref.dtype)


def _matmul(a, w, tm=1024, tn=512, out_dtype=_f32):
    M, K = a.shape
    N = w.shape[1]
    return pl.pallas_call(
        _matmul_kernel,
        out_shape=jax.ShapeDtypeStruct((M, N), out_dtype),
        grid=(M // tm, N // tn),
        in_specs=[pl.BlockSpec((tm, K), lambda i, j: (i, 0)),
                  pl.BlockSpec((K, tn), lambda i, j: (0, j))],
        out_specs=pl.BlockSpec((tm, tn), lambda i, j: (i, j)),
        compiler_params=pltpu.CompilerParams(dimension_semantics=("parallel", "parallel"),
                                             vmem_limit_bytes=VMEM_LIMIT),
        name="in_proj",
    )(a, w)


def _out_proj_kernel(a_ref, w_ref, x_ref, g_ref, nw_ref, sc_ref, sh_ref, xo_ref, ho_ref):
    y = jnp.dot(a_ref[...], w_ref[...], preferred_element_type=_f32)
    xn = x_ref[...] + g_ref[0] * y
    xo_ref[...] = xn
    ho_ref[...] = _rms_mod(xn, nw_ref[...], sc_ref[0], sh_ref[0]).astype(ho_ref.dtype)


def _out_proj(a, w, x, g, nw, sc, sh, tm=512):
    per_b = SEQ // tm
    bspec = pl.BlockSpec((1, 1, D_MODEL), lambda i: (i // per_b, 0, 0))
    return pl.pallas_call(
        _out_proj_kernel,
        out_shape=(jax.ShapeDtypeStruct((TOKENS, D_MODEL), _f32),
                   jax.ShapeDtypeStruct((TOKENS, D_MODEL), _bf16)),
        grid=(TOKENS // tm,),
        in_specs=[pl.BlockSpec((tm, MIX_OUT), lambda i: (i, 0)),
                  pl.BlockSpec((MIX_OUT, D_MODEL), lambda i: (0, 0)),
                  pl.BlockSpec((tm, D_MODEL), lambda i: (i, 0)),
                  bspec,
                  pl.BlockSpec((1, D_MODEL), lambda i: (0, 0)),
                  bspec, bspec],
        out_specs=(pl.BlockSpec((tm, D_MODEL), lambda i: (i, 0)),
                   pl.BlockSpec((tm, D_MODEL), lambda i: (i, 0))),
        compiler_params=pltpu.CompilerParams(dimension_semantics=("parallel",),
                                             vmem_limit_bytes=VMEM_LIMIT),
        name="out_proj",
    )(a, w, x, g, nw, sc, sh)


def _mlp_kernel(h_ref, w1_ref, w2_ref, x_ref, g_ref, o_ref, acc_ref):
    j = pl.program_id(1)

    @pl.when(j == 0)
    def _():
        acc_ref[...] = jnp.zeros_like(acc_ref)

    u = jnp.dot(h_ref[...], w1_ref[...], preferred_element_type=_f32)
    u = jnp.square(jnp.maximum(u, 0.0)).astype(_bf16)
    acc_ref[...] += jnp.dot(u, w2_ref[...], preferred_element_type=_f32)

    @pl.when(j == pl.num_programs(1) - 1)
    def _():
        o_ref[...] = x_ref[...] + g_ref[0] * acc_ref[...]


def _mlp(h, w1, w2, x, g, tm=512, th=512):
    per_b = SEQ // tm
    return pl.pallas_call(
        _mlp_kernel,
        out_shape=jax.ShapeDtypeStruct((TOKENS, D_MODEL), _f32),
        grid=(TOKENS // tm, MLP_HIDDEN // th),
        in_specs=[pl.BlockSpec((tm, D_MODEL), lambda i, j: (i, 0)),
                  pl.BlockSpec((D_MODEL, th), lambda i, j: (0, j)),
                  pl.BlockSpec((th, D_MODEL), lambda i, j: (j, 0)),
                  pl.BlockSpec((tm, D_MODEL), lambda i, j: (i, 0)),
                  pl.BlockSpec((1, 1, D_MODEL), lambda i, j: (i // per_b, 0, 0))],
        out_specs=pl.BlockSpec((tm, D_MODEL), lambda i, j: (i, 0)),
        scratch_shapes=[pltpu.VMEM((tm, D_MODEL), _f32)],
        compiler_params=pltpu.CompilerParams(dimension_semantics=("parallel", "arbitrary"),
                                             vmem_limit_bytes=VMEM_LIMIT),
        name="mlp",
    )(h, w1, w2, x, g)


def _final_norm_kernel(x_ref, w_ref, o_ref):
    x = x_ref[...]
    o_ref[...] = x * lax.rsqrt(jnp.mean(x * x, axis=-1, keepdims=True) + EPS) * w_ref[...]


def _final_norm(x, w, tm=512):
    return pl.pallas_call(
        _final_norm_kernel,
        out_shape=jax.ShapeDtypeStruct((TOKENS, D_MODEL), _f32),
        grid=(TOKENS // tm,),
        in_specs=[pl.BlockSpec((tm, D_MODEL), lambda i: (i, 0)),
                  pl.BlockSpec((1, D_MODEL), lambda i: (0, 0))],
        out_specs=pl.BlockSpec((tm, D_MODEL), lambda i: (i, 0)),
        compiler_params=pltpu.CompilerParams(dimension_semantics=("parallel",),
                                             vmem_limit_bytes=VMEM_LIMIT),
        name="final_norm",
    )(x, w)


NSA_R = NSA_HEADS // NSA_KV_GROUPS
NSA_CMP_ROWS = SEQ // CMP_STRIDE
NSA_NC = NSA_CMP_ROWS - CMP_BLOCK // CMP_STRIDE + 1
NSA_NSB = SEQ // SLC_BLOCK
NSA_NQB = SEQ // Q_BLOCK
assert NSA_CMP_ROWS == LANE and Q_BLOCK == LANE and LANE % NSA_NSB == 0 and SLC_TOP_N <= NSA_NSB
assert NSA_KV_GROUPS * NSA_HEAD_DIM == LANE and CMP_BLOCK == 2 * CMP_STRIDE


def _bucket_value(tab_ref, h, rel):
    exact = REL_BUCKETS // 2
    n = jnp.maximum(rel, 0)
    large = exact + (jnp.log(jnp.maximum(n, 1).astype(_f32) / exact)
                     / math.log(REL_MAX_DIST / exact) * (REL_BUCKETS - exact)).astype(jnp.int32)
    bucket = jnp.where(n < exact, n, jnp.minimum(large, REL_BUCKETS - 1))
    val = jnp.full(rel.shape, tab_ref[0, h], _f32)
    for b in range(1, REL_BUCKETS):
        val = jnp.where(bucket == b, tab_ref[b, h], val)
    return val


def _nsa_bias_kernel(tab_ref, tb_ref, cb_ref):
    h = pl.program_id(0)
    ql = lax.broadcasted_iota(jnp.int32, (Q_BLOCK, LANE), 0)
    kl = lax.broadcasted_iota(jnp.int32, (Q_BLOCK, LANE), 1)
    for d in range(3):
        tb_ref[d, 0] = _bucket_value(tab_ref, h, ql - kl + d * Q_BLOCK)
    cmp_end = kl * CMP_STRIDE + (CMP_BLOCK - 1)
    for qb in range(NSA_NQB):
        cb_ref[qb, 0] = _bucket_value(tab_ref, h, qb * Q_BLOCK + ql - cmp_end)


def _nsa_bias_tiles(rel_bias):
    assert 2 * Q_BLOCK >= REL_MAX_DIST
    return pl.pallas_call(
        _nsa_bias_kernel,
        out_shape=(jax.ShapeDtypeStruct((3, NSA_HEADS, Q_BLOCK, LANE), _f32),
                   jax.ShapeDtypeStruct((NSA_NQB, NSA_HEADS, Q_BLOCK, LANE), _f32)),
        grid=(NSA_HEADS,),
        in_specs=[pl.BlockSpec(memory_space=pltpu.SMEM)],
        out_specs=(pl.BlockSpec((3, 1, Q_BLOCK, LANE), lambda h: (0, h, 0, 0)),
                   pl.BlockSpec((NSA_NQB, 1, Q_BLOCK, LANE), lambda h: (0, h, 0, 0))),
        compiler_params=pltpu.CompilerParams(dimension_semantics=("parallel",),
                                             vmem_limit_bytes=VMEM_LIMIT),
        name="nsa_bias_tiles",
    )(rel_bias)


def _nsa_cmp_kernel(tk_ref, tv_ref, pos_ref, w1_ref, w2_ref, kc_ref, vc_ref):
    rows = lax.broadcasted_iota(jnp.int32, (NSA_CMP_ROWS, LANE), 0)
    for idx, (t_ref, o_ref) in enumerate(((tk_ref, kc_ref), (tv_ref, vc_ref))):
        t = t_ref[0]
        u = jnp.dot((t + pos_ref[idx, 0]).astype(_bf16), w1_ref[idx, 0], preferred_element_type=_f32)
        v = jnp.dot((t + pos_ref[idx, 1]).astype(_bf16), w1_ref[idx, 1], preferred_element_type=_f32)
        pre = u + pltpu.roll(v, NSA_CMP_ROWS - 1, 0)
        act = pre * jax.nn.sigmoid(pre)
        out = jnp.dot(act.astype(_bf16), w2_ref[idx], preferred_element_type=_f32)
        o_ref[0] = jnp.where(rows < NSA_NC, out, 0.0).astype(_bf16)


def _nsa_compress(tk, tv, pos_x, w1_x, w2_x):
    B = tk.shape[0]
    wide = CMP_STRIDE * LANE
    tspec = pl.BlockSpec((1, NSA_CMP_ROWS, wide), lambda b: (b, 0, 0))
    ospec = pl.BlockSpec((1, NSA_CMP_ROWS, LANE), lambda b: (b, 0, 0))
    return pl.pallas_call(
        _nsa_cmp_kernel,
        out_shape=(jax.ShapeDtypeStruct((B, NSA_CMP_ROWS, LANE), _bf16),) * 2,
        grid=(B,),
        in_specs=[tspec, tspec,
                  pl.BlockSpec((2, 2, 1, wide), lambda b: (0, 0, 0, 0)),
                  pl.BlockSpec((2, 2, wide, LANE), lambda b: (0, 0, 0, 0)),
                  pl.BlockSpec((2, LANE, LANE), lambda b: (0, 0, 0))],
        out_specs=(ospec, ospec),
        compiler_params=pltpu.CompilerParams(dimension_semantics=("parallel",),
                                             vmem_limit_bytes=VMEM_LIMIT),
        name="nsa_compress",
    )(tk, tv, pos_x, w1_x, w2_x)


def _dot_nt(a, b):
    return lax.dot_general(a, b, (((1,), (1,)), ((), ())), preferred_element_type=_f32)


def _nsa_kernel(q_ref, ks_ref, vs_ref, kw_ref, vw_ref, kc_ref, vc_ref, gate_ref, tb_ref, cb_ref, ov_ref, e4_ref,
                o_ref, m_ref, l_ref, acc_ref, sel_ref):
    R, QB = NSA_R, Q_BLOCK
    qb = pl.program_id(1)
    ql = lax.broadcasted_iota(jnp.int32, (QB, LANE), 0)
    kl = lax.broadcasted_iota(jnp.int32, (QB, LANE), 1)
    t_row = qb * QB + ql
    gates = jax.nn.sigmoid(gate_ref[...])
    heads = [None] * NSA_HEADS

    for g in range(NSA_KV_GROUPS):
        qp = jnp.concatenate([q_ref[:, (R * g + r) * LANE:(R * g + r + 1) * LANE] for r in range(R)], axis=0)

        def flash(k_ref, v_ref, c_lo, c_hi, selected):
            m_ref[...] = jnp.full(m_ref.shape, NEG_INF, _f32)
            l_ref[...] = jnp.zeros(l_ref.shape, _f32)
            acc_ref[...] = jnp.zeros(acc_ref.shape, _f32)

            def body(c, carry):
                off = pl.multiple_of(c * QB, QB)
                s3 = _dot_nt(qp, k_ref[pl.ds(off, QB), :]).reshape(R, QB, LANE)
                s3 = s3 + tb_ref[jnp.minimum(qb - c, 2), R * g:R * (g + 1)]
                rel = t_row - (c * QB + kl)
                if selected:
                    mask = (rel >= 0) & (sel_ref[:, pl.ds(off, QB)] > 0.5)
                else:
                    mask = (rel >= 0) & (rel < WINDOW)
                s3 = jnp.where(mask[None], s3, NEG_INF)
                m_old = m_ref[...]
                m_new = jnp.maximum(m_old, jnp.max(s3, axis=-1, keepdims=True))
                e = jnp.where(mask[None], jnp.exp(s3 - m_new), 0.0)
                alpha = jnp.exp(m_old - m_new)
                l_ref[...] = alpha * l_ref[...] + jnp.sum(e, axis=-1, keepdims=True)
                pv = jnp.dot(e.astype(_bf16).reshape(R * QB, LANE), v_ref[pl.ds(off, QB), :],
                             preferred_element_type=_f32)
                acc_ref[...] = alpha * acc_ref[...] + pv.reshape(R, QB, LANE)
                m_ref[...] = m_new
                return carry

            lax.fori_loop(c_lo, c_hi, body, 0)
            return acc_ref[...] / l_ref[...]

        s3 = _dot_nt(qp, kc_ref[0]).reshape(R, QB, LANE) + cb_ref[0, R * g:R * (g + 1)]
        mask_c = (t_row - (kl * CMP_STRIDE + CMP_BLOCK - 1) >= 0) & (kl < NSA_NC)
        s3 = jnp.where(mask_c[None], s3, NEG_INF)
        e = jnp.where(mask_c[None], jnp.exp(s3 - jnp.max(s3, axis=-1, keepdims=True)), 0.0)
        den = jnp.sum(e, axis=-1, keepdims=True)
        p = (e / jnp.where(den > 0.0, den, 1.0)).astype(_bf16)
        o_c = jnp.dot(p.reshape(R * QB, LANE), vc_ref[0], preferred_element_type=_f32).reshape(R, QB, LANE)

        imp = jnp.dot(jnp.concatenate([p[r] for r in range(R)], axis=1), ov_ref[...], preferred_element_type=_f32)
        j = kl & (NSA_NSB - 1)
        cur = t_row // SLC_BLOCK
        imp = jnp.where((j == 0) | (j == cur) | (j == cur - 1), FORCE_SCORE, imp)
        imp = jnp.where(j <= cur, imp, NEG_INF)
        cnt = jnp.zeros((QB, LANE), _f32)
        for sft in range(1, NSA_NSB):
            other = pltpu.roll(imp, sft, 1)
            beats = (other > imp) | ((other == imp) & (j >= sft))
            cnt = cnt + jnp.where(beats, 1.0, 0.0)
        sel = jnp.where(cnt < SLC_TOP_N, 1.0, 0.0).astype(_bf16)
        sel_ref[...] = jnp.dot(sel, e4_ref[...], preferred_element_type=_f32)

        o_s = flash(ks_ref, vs_ref, 0, qb + 1, True)
        o_w = flash(kw_ref, vw_ref, jnp.maximum(qb - WINDOW // QB, 0), qb + 1, False)

        for r in range(R):
            h = R * g + r
            heads[h] = (gates[:, 3 * h:3 * h + 1] * o_c[r] + gates[:, 3 * h + 1:3 * h + 2] * o_s[r]
                        + gates[:, 3 * h + 2:3 * h + 3] * o_w[r])

    for pk in range(NSA_HEADS // 2):
        even, odd = heads[2 * pk], heads[2 * pk + 1]
        if (2 * pk) // R == 0:
            odd = pltpu.roll(odd, NSA_HEAD_DIM, 1)
        else:
            even = pltpu.roll(even, NSA_HEAD_DIM, 1)
        o_ref[:, pk * LANE:(pk + 1) * LANE] = jnp.where(kl < NSA_HEAD_DIM, even, odd).astype(o_ref.dtype)


def _nsa_attention(proj_a, proj_b, kc, vc, tb, cb, ov4, e4):
    B = kc.shape[0]
    nqb = NSA_NQB
    qcols = NSA_HEADS * LANE
    kv = lambda blk: pl.BlockSpec((SEQ, LANE), lambda b, q: (b, qcols // LANE + blk))
    cspec = pl.BlockSpec((1, NSA_CMP_ROWS, LANE), lambda b, q: (b, 0, 0))
    return pl.pallas_call(
        _nsa_kernel,
        out_shape=jax.ShapeDtypeStruct((B * SEQ, NSA_HEADS * NSA_HEAD_DIM), _bf16),
        grid=(B, nqb),
        in_specs=[pl.BlockSpec((Q_BLOCK, qcols), lambda b, q: (b * nqb + q, 0)),
                  kv(0), kv(1), kv(2), kv(3), cspec, cspec,
                  pl.BlockSpec((Q_BLOCK, LANE), lambda b, q: (b * nqb + q, 2)),
                  pl.BlockSpec((3, NSA_HEADS, Q_BLOCK, LANE), lambda b, q: (0, 0, 0, 0)),
                  pl.BlockSpec((1, NSA_HEADS, Q_BLOCK, LANE), lambda b, q: (q, 0, 0, 0)),
                  pl.BlockSpec((NSA_R * LANE, LANE), lambda b, q: (0, 0)),
                  pl.BlockSpec((LANE, SEQ), lambda b, q: (0, 0))],
        out_specs=pl.BlockSpec((Q_BLOCK, NSA_HEADS * NSA_HEAD_DIM), lambda b, q: (b * nqb + q, 0)),
        scratch_shapes=[pltpu.VMEM((NSA_R, Q_BLOCK, 1), _f32),
                        pltpu.VMEM((NSA_R, Q_BLOCK, 1), _f32),
                        pltpu.VMEM((NSA_R, Q_BLOCK, LANE), _f32),
                        pltpu.VMEM((Q_BLOCK, SEQ), _f32)],
        compiler_params=pltpu.CompilerParams(dimension_semantics=("parallel", "arbitrary"),
                                             vmem_limit_bytes=VMEM_LIMIT),
        name="nsa_attention",
    )(proj_a, proj_a, proj_a, proj_a, proj_a, kc, vc, proj_b, tb, cb, ov4, e4)


def _nsa_constants():
    n = jnp.arange(NSA_CMP_ROWS)[:, None]
    jj = jnp.arange(LANE)[None, :] % NSA_NSB
    overlap = ((n * CMP_STRIDE <= jj * SLC_BLOCK + SLC_BLOCK - 1)
               & (n * CMP_STRIDE + CMP_BLOCK - 1 >= jj * SLC_BLOCK) & (n < NSA_NC))
    ov4 = jnp.tile(overlap.astype(_bf16), (NSA_R, 1))
    row = jnp.arange(LANE)[:, None]
    e4 = ((row < NSA_NSB) & (jnp.arange(SEQ)[None, :] // SLC_BLOCK == row)).astype(_bf16)
    return ov4, e4


def _nsa_weight_prep(w_in, cmp_pos, cmp_w1, cmp_w2):
    G, hd = NSA_KV_GROUPS, NSA_HEAD_DIM
    eye = jnp.eye(G, dtype=_f32)
    nq = NSA_HEADS * hd
    wq = w_in[:, :, :nq].reshape(DEPTH, D_MODEL, NSA_HEADS, 1, hd) * (hd ** -0.5)
    head_group = (jnp.arange(NSA_HEADS)[:, None] // NSA_R == jnp.arange(G)[None, :]).astype(_f32)
    wq = (wq * head_group[None, None, :, :, None]).reshape(DEPTH, D_MODEL, NSA_HEADS * LANE)
    slc_win = w_in[:, :, nq + 2 * NSA_KV:nq + 6 * NSA_KV]
    w_a = jnp.concatenate([wq, slc_win], axis=-1).astype(_bf16)
    rest = jnp.concatenate([w_in[:, :, nq:nq + 2 * NSA_KV], w_in[:, :, nq + 6 * NSA_KV:]], axis=-1)
    w_b = jnp.pad(rest, ((0, 0), (0, 0), (0, PROJ_B_PAD - rest.shape[-1]))).astype(_bf16)
    half = CMP_BLOCK // 2
    w1 = cmp_w1.reshape(DEPTH, 2, CMP_BLOCK, hd, hd)
    w1_x = jnp.einsum('lxide,gh->lxigdhe', w1, eye).reshape(DEPTH, 2, 2, half * G * hd, G * hd).astype(_bf16)
    w2_x = jnp.einsum('lxde,gh->lxgdhe', cmp_w2, eye).reshape(DEPTH, 2, G * hd, G * hd).astype(_bf16)
    pos_x = jnp.broadcast_to(cmp_pos[:, :, :, None, :], (DEPTH, 2, CMP_BLOCK, G, hd))
    pos_x = pos_x.reshape(DEPTH, 2, 2, 1, half * G * hd)
    return w_a, w_b, w1_x, w2_x, pos_x


def _split(t, sizes):
    cuts, acc = [], 0
    for s in sizes[:-1]:
        acc += s
        cuts.append(acc)
    return jnp.split(t, cuts, axis=-1)


def _rmsnorm(x, w):
    xf = x.astype(jnp.float32)
    y = xf * lax.rsqrt(jnp.mean(xf * xf, axis=-1, keepdims=True) + EPS)
    return (y * w.astype(jnp.float32)).astype(x.dtype)


def _l2norm(x):
    return x * lax.rsqrt(jnp.sum(x * x, axis=-1, keepdims=True) + EPS)


def _causal_conv(x, w):
    K, C = w.shape
    return lax.conv_general_dilated(x, w.astype(x.dtype)[:, None, :], window_strides=(1,), padding=((K - 1, 0),),
                                    dimension_numbers=('NWC', 'WIO', 'NWC'), feature_group_count=C)


def _chunk_state_scan(local, decay):
    def step(state, inp):
        u, a = inp
        return a * state + u, state
    lt = jnp.moveaxis(local, 1, 0)
    dt = jnp.moveaxis(decay, 1, 0)
    _, prev = lax.scan(step, jnp.zeros_like(lt[0]), (lt, dt))
    return jnp.moveaxis(prev, 0, 1)


def _ssd(z, xbc, dt_raw, conv_w, conv_b, dt_bias, a_log, d_skip, norm_w):
    B, S, _ = z.shape
    f32 = jnp.float32
    G, R, P, N, L = SSD_GROUPS, SSD_HEADS // SSD_GROUPS, SSD_HEAD_DIM, SSD_STATE, CHUNK
    nc = S // L
    xbc = jax.nn.silu(_causal_conv(xbc, conv_w) + conv_b).astype(f32)
    xs, bm, cm = _split(xbc, (SSD_INNER, G * N, G * N))
    x = xs.reshape(B, nc, L, G, R, P)
    bm = bm.reshape(B, nc, L, G, N)
    cm = cm.reshape(B, nc, L, G, N)
    dt = jax.nn.softplus(dt_raw.astype(f32) + dt_bias.astype(f32)).reshape(B, nc, L, G, R)
    a_cum = jnp.cumsum(dt * (-jnp.exp(a_log.astype(f32))).reshape(G, R), axis=2)
    tril = (jnp.arange(L)[:, None] >= jnp.arange(L)[None, :])[:, :, None, None]
    seg = jnp.exp(jnp.where(tril, a_cum[:, :, :, None] - a_cum[:, :, None, :], -jnp.inf))
    xdt = x * dt[..., None]
    cb = jnp.einsum('bcign,bcjgn->bcijg', cm, bm)
    y = jnp.einsum('bcijg,bcijgr,bcjgrp->bcigrp', cb, seg, xdt)
    states = jnp.einsum('bcjgn,bcjgr,bcjgrp->bcgrpn', bm, jnp.exp(a_cum[:, :, -1:] - a_cum), xdt)
    prev = _chunk_state_scan(states, jnp.exp(a_cum[:, :, -1])[..., None, None])
    y = y + jnp.einsum('bcign,bcgrpn,bcigr->bcigrp', cm, prev, jnp.exp(a_cum))
    y = y + x * d_skip.astype(f32).reshape(G, R, 1)
    y = y.reshape(B, S, SSD_INNER) * jax.nn.silu(z.astype(f32))
    y = _rmsnorm(y.reshape(B, S, G, SSD_INNER // G), norm_w.reshape(G, SSD_INNER // G))
    return y.reshape(B, S, SSD_INNER).astype(z.dtype)


def _gdn(q, k, v, z, beta_raw, a_raw, conv_w, dt_bias, a_log, norm_w):
    B, S, _ = q.shape
    f32 = jnp.float32
    H, Dh, L = GDN_HEADS, GDN_HEAD_DIM, CHUNK
    nc = S // L
    qkv = jax.nn.silu(_causal_conv(jnp.concatenate([q, k, v], axis=-1), conv_w)).astype(f32)
    qq, kk, vv = _split(qkv, (GDN_WIDTH, GDN_WIDTH, GDN_WIDTH))
    qq = _l2norm(qq.reshape(B, S, H, Dh)) * Dh ** -0.5
    kk = _l2norm(kk.reshape(B, S, H, Dh))
    vv = vv.reshape(B, S, H, Dh)
    beta = jax.nn.sigmoid(beta_raw.astype(f32))
    g = -jnp.exp(a_log.astype(f32)) * jax.nn.softplus(a_raw.astype(f32) + dt_bias.astype(f32))
    chunk = lambda t: jnp.moveaxis(t.reshape(B, nc, L, H, *t.shape[3:]), 3, 2)
    qc, kc, vc, bc, gc = chunk(qq), chunk(kk), chunk(vv), chunk(beta), chunk(g)
    gcum = jnp.cumsum(gc, axis=-1)
    i = jnp.arange(L)
    tril = i[:, None] >= i[None, :]
    strict = i[:, None] > i[None, :]
    decay = jnp.exp(jnp.where(tril, gcum[..., :, None] - gcum[..., None, :], -jnp.inf))
    kb = kc * bc[..., None]
    a_mat = jnp.where(strict, jnp.einsum('bnhid,bnhjd->bnhij', kb, kc) * decay, 0.0)
    rhs = jnp.concatenate([vc * bc[..., None], kb * jnp.exp(gcum)[..., None]], axis=-1)
    sol = lax.linalg.triangular_solve(a_mat + jnp.eye(L, dtype=f32), rhs, left_side=True, lower=True,
                                      unit_diagonal=True)
    u, w = sol[..., :Dh], sol[..., Dh:]
    aqk = jnp.einsum('bnhid,bnhjd->bnhij', qc, kc) * decay
    q_dec = qc * jnp.exp(gcum)[..., None]
    k_end = kc * jnp.exp(gcum[..., -1:] - gcum)[..., None]
    d_last = jnp.exp(gcum[..., -1])

    def step(state, inp):
        qd, ke, uu, ww, aa, dl = inp
        v_new = uu - jnp.einsum('bhid,bhde->bhie', ww, state)
        o = jnp.einsum('bhid,bhde->bhie', qd, state) + jnp.einsum('bhij,bhje->bhie', aa, v_new)
        state = state * dl[..., None, None] + jnp.einsum('bhjd,bhje->bhde', ke, v_new)
        return state, o

    xs = tuple(jnp.moveaxis(t, 1, 0) for t in (q_dec, k_end, u, w, aqk, d_last))
    _, o = lax.scan(step, jnp.zeros((B, H, Dh, Dh), f32), xs)
    o = jnp.moveaxis(jnp.moveaxis(o, 0, 1), 3, 2).reshape(B, S, H, Dh)
    o = _rmsnorm(o, norm_w) * jax.nn.silu(z.astype(f32).reshape(B, S, H, Dh))
    return o.reshape(B, S, GDN_WIDTH).astype(q.dtype)


def _gla(q, k, v, g_out, g_lr, gate_w2, gate_b, norm_w):
    B, S, _ = q.shape
    f32 = jnp.float32
    H, Dk, Dv, L = GLA_HEADS, GLA_DK, GLA_DV, CHUNK
    nc = S // L
    gk = jax.nn.log_sigmoid((g_lr @ gate_w2 + gate_b).astype(f32)) / GLA_GATE_NORM
    chunk = lambda t, d: jnp.moveaxis(t.astype(f32).reshape(B, nc, L, H, d), 3, 2)
    qc = chunk(q, Dk) * Dk ** -0.5
    kc = chunk(k, Dk)
    vc = chunk(v, Dv)
    bcum = jnp.cumsum(chunk(gk, Dk), axis=3)
    q_dec = qc * jnp.exp(bcum)
    k_inv = kc * jnp.exp(-bcum)
    tril = jnp.arange(L)[:, None] >= jnp.arange(L)[None, :]
    attn = jnp.where(tril, jnp.einsum('bnhid,bnhjd->bnhij', q_dec, k_inv), 0.0)
    o = jnp.einsum('bnhij,bnhje->bnhie', attn, vc)
    k_end = kc * jnp.exp(bcum[..., -1:, :] - bcum)
    local = jnp.einsum('bnhjd,bnhje->bnhde', k_end, vc)
    prev = _chunk_state_scan(local, jnp.exp(bcum[..., -1, :])[..., None])
    o = o + jnp.einsum('bnhid,bnhde->bnhie', q_dec, prev)
    o = jnp.moveaxis(o, 2, 3).reshape(B, S, H, Dv)
    o = _rmsnorm(o, norm_w) * jax.nn.silu(g_out.astype(f32).reshape(B, S, H, Dv))
    return o.reshape(B, S, GLA_VAL).astype(q.dtype)


def _recurrent_mixers(proj,
                      ssd_conv_w, ssd_conv_b, ssd_dt_bias, ssd_a_log, ssd_d, ssd_norm_w,
                      gdn_conv_w, gdn_dt_bias, gdn_a_log, gdn_norm_w, gla_gate_w2, gla_gate_b, gla_norm_w):
    (sz, sxbc, sdt,
     gq, gk, gv, gz, gbeta, ga,
     lq, lk, lv, lg, llr) = _split(proj, IN_SPLITS[8:])
    y_ssd = _ssd(sz, sxbc, sdt, ssd_conv_w, ssd_conv_b, ssd_dt_bias, ssd_a_log, ssd_d, ssd_norm_w)
    y_gdn = _gdn(gq, gk, gv, gz, gbeta, ga, gdn_conv_w, gdn_dt_bias, gdn_a_log, gdn_norm_w)
    y_gla = _gla(lq, lk, lv, lg, llr, gla_gate_w2, gla_gate_b, gla_norm_w)
    return jnp.concatenate([y_ssd, y_gdn, y_gla], axis=-1)


def kernel(x, c, rel_bias, norm1_w, norm2_w, ada_w, ada_b, w_in, w_out, nsa_cmp_pos, nsa_cmp_w1, nsa_cmp_w2, ssd_conv_w, ssd_conv_b, ssd_dt_bias, ssd_a_log, ssd_d, ssd_norm_w, gdn_conv_w, gdn_dt_bias, gdn_a_log, gdn_norm_w, gla_gate_w2, gla_gate_b, gla_norm_w, mlp_w1, mlp_w2, final_norm_w):
    B, S, D = x.shape
    mod = _ada_all(c, ada_w, ada_b).reshape(DEPTH, B, 6, 1, D)
    w_a, w_b, cmp_w1_x, cmp_w2_x, cmp_pos_x = _nsa_weight_prep(w_in, nsa_cmp_pos, nsa_cmp_w1, nsa_cmp_w2)
    w_out_b = w_out.astype(_bf16)
    w1_b = mlp_w1.astype(_bf16)
    w2_b = mlp_w2.astype(_bf16)
    tb, cb = _nsa_bias_tiles(rel_bias)
    ov4, e4 = _nsa_constants()
    gate_col = 2 * NSA_KV
    rec_col = gate_col + NSA_HEADS * 3
    xf = x.reshape(TOKENS, D)
    for l in range(DEPTH):
        sh1, sc1, g1, sh2, sc2, g2 = (mod[l, :, i] for i in range(6))
        h = _norm_mod(xf, norm1_w[l][None], sc1, sh1)
        proj_a = _matmul(h, w_a[l], out_dtype=_bf16)
        proj_b = _matmul(h, w_b[l])
        tk = proj_b[:, :NSA_KV].reshape(B, NSA_CMP_ROWS, CMP_STRIDE * NSA_KV)
        tv = proj_b[:, NSA_KV:2 * NSA_KV].reshape(B, NSA_CMP_ROWS, CMP_STRIDE * NSA_KV)
        kc, vc = _nsa_compress(tk, tv, cmp_pos_x[l], cmp_w1_x[l], cmp_w2_x[l])
        y_nsa = _nsa_attention(proj_a, proj_b, kc, vc, tb, cb, ov4, e4)
        y_rec = _recurrent_mixers(proj_b[:, rec_col:PROJ_B_COLS].reshape(B, S, PROJ_B_COLS - rec_col),
                                  ssd_conv_w[l], ssd_conv_b[l], ssd_dt_bias[l], ssd_a_log[l], ssd_d[l], ssd_norm_w[l],
                                  gdn_conv_w[l], gdn_dt_bias[l], gdn_a_log[l], gdn_norm_w[l],
                                  gla_gate_w2[l], gla_gate_b[l], gla_norm_w[l])
        y = jnp.concatenate([y_nsa, y_rec.reshape(TOKENS, MIX_OUT - y_nsa.shape[1]).astype(_bf16)], axis=-1)
        xf, h2 = _out_proj(y, w_out_b[l], xf, g1,
                           norm2_w[l][None], sc2, sh2)
        xf = _mlp(h2, w1_b[l], w2_b[l], xf, g2)
    return _final_norm(xf, final_norm_w[None]).reshape(B, S, D)
```

```python
import math
from functools import partial

import jax
import jax.numpy as jnp
from jax import lax
from jax.experimental import pallas as pl
from jax.experimental.pallas import tpu as pltpu

D_MODEL = 2048
BATCH = 16
SEQ = 2048
DEPTH = 4

MIX_GROUP = D_MODEL // 4
NSA_HEAD_DIM = 64
NSA_HEADS = MIX_GROUP // NSA_HEAD_DIM
NSA_KV_GROUPS = max(1, NSA_HEADS // 4)
NSA_KV = NSA_KV_GROUPS * NSA_HEAD_DIM
CMP_BLOCK = 32
CMP_STRIDE = 16
SLC_BLOCK = 64
SLC_TOP_N = 8
WINDOW = 512
Q_BLOCK = 128
REL_BUCKETS = 32
REL_MAX_DIST = 128
SSD_HEAD_DIM = 64
SSD_HEADS = MIX_GROUP // SSD_HEAD_DIM
SSD_INNER = SSD_HEADS * SSD_HEAD_DIM
SSD_GROUPS = 2
SSD_STATE = 128
SSD_CONV = 4
SSD_XBC = SSD_INNER + 2 * SSD_GROUPS * SSD_STATE
GDN_HEAD_DIM = 128
GDN_HEADS = MIX_GROUP // GDN_HEAD_DIM
GDN_WIDTH = GDN_HEADS * GDN_HEAD_DIM
GDN_CONV = 4
GLA_DV = 128
GLA_HEADS = MIX_GROUP // GLA_DV
GLA_DK = GLA_DV // 2
GLA_KEY = GLA_HEADS * GLA_DK
GLA_VAL = GLA_HEADS * GLA_DV
GLA_GATE_RANK = 16
GLA_GATE_NORM = 16.0
CHUNK = 64
MLP_HIDDEN = 4 * D_MODEL
EPS = 1e-6
NEG_INF = -1e30
FORCE_SCORE = 1e9
IN_SPLITS = (NSA_HEADS * NSA_HEAD_DIM, NSA_KV, NSA_KV, NSA_KV, NSA_KV, NSA_KV, NSA_KV, NSA_HEADS * 3,
             SSD_INNER, SSD_XBC, SSD_HEADS,
             GDN_WIDTH, GDN_WIDTH, GDN_WIDTH, GDN_WIDTH, GDN_HEADS, GDN_HEADS,
             GLA_KEY, GLA_KEY, GLA_VAL, GLA_VAL, GLA_GATE_RANK)
IN_COLS = sum(IN_SPLITS)
MIX_OUT = NSA_HEADS * NSA_HEAD_DIM + SSD_INNER + GDN_WIDTH + GLA_VAL

LANE = 128
VMEM_LIMIT = 56 * 1024 * 1024
TOKENS = BATCH * SEQ
_IN_NAMES = ('nq', 'nkc', 'nvc', 'nks', 'nvs', 'nkw', 'nvw', 'ngate', 'sz', 'sxbc', 'sdt',
             'gq', 'gk', 'gv', 'gz', 'gbeta', 'ga', 'lq', 'lk', 'lv', 'lg', 'llr')
_IN_W = dict(zip(_IN_NAMES, IN_SPLITS))
_IN_OFF = {n: sum(IN_SPLITS[:i]) for i, n in enumerate(_IN_NAMES)}
_PB_SEGMENTS = (('sxbc', SSD_XBC, SSD_XBC), ('gq', GDN_WIDTH, GDN_WIDTH), ('gk', GDN_WIDTH, GDN_WIDTH),
                ('gv', GDN_WIDTH, GDN_WIDTH), ('gz', GDN_WIDTH, GDN_WIDTH), ('sz', SSD_INNER, SSD_INNER),
                ('lv', GLA_VAL, GLA_VAL), ('lg', GLA_VAL, GLA_VAL), ('lq', GLA_KEY, GLA_KEY), ('lk', GLA_KEY, GLA_KEY),
                ('nkc', NSA_KV, LANE), ('nvc', NSA_KV, LANE), ('ngate', NSA_HEADS * 3, LANE), ('sdt', SSD_HEADS, LANE),
                ('gbeta', 2 * GDN_HEADS, LANE), ('llr', GLA_GATE_RANK, LANE))
PB = {}
PROJ_B_COLS = 0
for _n, _w, _pw in _PB_SEGMENTS:
    assert PROJ_B_COLS % _pw == 0 and _pw % LANE == 0
    PB[_n] = PROJ_B_COLS
    PROJ_B_COLS += _pw
PROJ_B_TN = 256
assert PROJ_B_COLS % PROJ_B_TN == 0 and _IN_OFF['ga'] == _IN_OFF['gbeta'] + GDN_HEADS

_bf16 = jnp.bfloat16
_f32 = jnp.float32


def _ada_kernel(c_ref, w_ref, b_ref, o_ref):
    c = c_ref[...]
    c_act = c * jax.nn.sigmoid(c)
    o_ref[0] = jnp.dot(c_act, w_ref[0], preferred_element_type=_f32) + b_ref[0]


def _ada_all(c, ada_w, ada_b):
    tn = 1024
    return pl.pallas_call(
        _ada_kernel,
        out_shape=jax.ShapeDtypeStruct((DEPTH, BATCH, 6 * D_MODEL), _f32),
        grid=(DEPTH, 6 * D_MODEL // tn),
        in_specs=[pl.BlockSpec((BATCH, D_MODEL), lambda l, j: (0, 0)),
                  pl.BlockSpec((1, D_MODEL, tn), lambda l, j: (l, 0, j)),
                  pl.BlockSpec((1, 1, tn), lambda l, j: (l, 0, j))],
        out_specs=pl.BlockSpec((1, BATCH, tn), lambda l, j: (l, 0, j)),
        compiler_params=pltpu.CompilerParams(dimension_semantics=("parallel", "parallel"),
                                             vmem_limit_bytes=VMEM_LIMIT),
        name="ada_mod",
    )(c, ada_w, ada_b.reshape(DEPTH, 1, 6 * D_MODEL))


def _rms_mod(x, w, sc, sh):
    y = x * lax.rsqrt(jnp.mean(x * x, axis=-1, keepdims=True) + EPS)
    return (y * w) * (1.0 + sc) + sh


def _norm_mod_kernel(x_ref, w_ref, sc_ref, sh_ref, o_ref):
    o_ref[...] = _rms_mod(x_ref[...], w_ref[...], sc_ref[0], sh_ref[0]).astype(o_ref.dtype)


def _norm_mod(x, w, sc, sh, tm=512):
    per_b = SEQ // tm
    return pl.pallas_call(
        _norm_mod_kernel,
        out_shape=jax.ShapeDtypeStruct((TOKENS, D_MODEL), _bf16),
        grid=(TOKENS // tm,),
        in_specs=[pl.BlockSpec((tm, D_MODEL), lambda i: (i, 0)),
                  pl.BlockSpec((1, D_MODEL), lambda i: (0, 0)),
                  pl.BlockSpec((1, 1, D_MODEL), lambda i: (i // per_b, 0, 0)),
                  pl.BlockSpec((1, 1, D_MODEL), lambda i: (i // per_b, 0, 0))],
        out_specs=pl.BlockSpec((tm, D_MODEL), lambda i: (i, 0)),
        compiler_params=pltpu.CompilerParams(dimension_semantics=("parallel",),
                                             vmem_limit_bytes=VMEM_LIMIT),
        name="norm_mod",
    )(x, w, sc, sh)


def _matmul_kernel(a_ref, w_ref, o_ref):
    o_ref[...] = jnp.dot(a_ref[...], w_ref[...], preferred_element_type=_f32).astype(o_ref.dtype)


def _matmul(a, w, tm=1024, tn=512, out_dtype=_f32):
    M, K = a.shape
    N = w.shape[1]
    return pl.pallas_call(
        _matmul_kernel,
        out_shape=jax.ShapeDtypeStruct((M, N), out_dtype),
        grid=(M // tm, N // tn),
        in_specs=[pl.BlockSpec((tm, K), lambda i, j: (i, 0)),
                  pl.BlockSpec((K, tn), lambda i, j: (0, j))],
        out_specs=pl.BlockSpec((tm, tn), lambda i, j: (i, j)),
        compiler_params=pltpu.CompilerParams(dimension_semantics=("parallel", "parallel"),
                                             vmem_limit_bytes=VMEM_LIMIT),
        name="in_proj",
    )(a, w)


def _out_proj_kernel(a0_ref, a1_ref, a2_ref, a3_ref, w_ref, x_ref, g_ref, nw_ref, sc_ref, sh_ref, xo_ref, ho_ref):
    y = None
    for i, a_ref in enumerate((a0_ref, a1_ref, a2_ref, a3_ref)):
        part = jnp.dot(a_ref[...], w_ref[i * MIX_GROUP:(i + 1) * MIX_GROUP, :], preferred_element_type=_f32)
        y = part if y is None else y + part
    xn = x_ref[...] + g_ref[0] * y
    xo_ref[...] = xn
    ho_ref[...] = _rms_mod(xn, nw_ref[...], sc_ref[0], sh_ref[0]).astype(ho_ref.dtype)


def _out_proj(mixed, w, x, g, nw, sc, sh, tm=512):
    per_b = SEQ // tm
    bspec = pl.BlockSpec((1, 1, D_MODEL), lambda i: (i // per_b, 0, 0))
    aspec = pl.BlockSpec((tm, MIX_GROUP), lambda i: (i, 0))
    return pl.pallas_call(
        _out_proj_kernel,
        out_shape=(jax.ShapeDtypeStruct((TOKENS, D_MODEL), _f32),
                   jax.ShapeDtypeStruct((TOKENS, D_MODEL), _bf16)),
        grid=(TOKENS // tm,),
        in_specs=[aspec, aspec, aspec, aspec,
                  pl.BlockSpec((MIX_OUT, D_MODEL), lambda i: (0, 0)),
                  pl.BlockSpec((tm, D_MODEL), lambda i: (i, 0)),
                  bspec,
                  pl.BlockSpec((1, D_MODEL), lambda i: (0, 0)),
                  bspec, bspec],
        out_specs=(pl.BlockSpec((tm, D_MODEL), lambda i: (i, 0)),
                   pl.BlockSpec((tm, D_MODEL), lambda i: (i, 0))),
        compiler_params=pltpu.CompilerParams(dimension_semantics=("parallel",),
                                             vmem_limit_bytes=VMEM_LIMIT),
        name="out_proj",
    )(*mixed, w, x, g, nw, sc, sh)


def _mlp_kernel(h_ref, w1_ref, w2_ref, x_ref, g_ref, o_ref, acc_ref):
    j = pl.program_id(1)

    @pl.when(j == 0)
    def _():
        acc_ref[...] = jnp.zeros_like(acc_ref)

    u = jnp.dot(h_ref[...], w1_ref[...], preferred_element_type=_f32)
    u = jnp.square(jnp.maximum(u, 0.0)).astype(_bf16)
    acc_ref[...] += jnp.dot(u, w2_ref[...], preferred_element_type=_f32)

    @pl.when(j == pl.num_programs(1) - 1)
    def _():
        o_ref[...] = x_ref[...] + g_ref[0] * acc_ref[...]


def _mlp(h, w1, w2, x, g, tm=512, th=512):
    per_b = SEQ // tm
    return pl.pallas_call(
        _mlp_kernel,
        out_shape=jax.ShapeDtypeStruct((TOKENS, D_MODEL), _f32),
        grid=(TOKENS // tm, MLP_HIDDEN // th),
        in_specs=[pl.BlockSpec((tm, D_MODEL), lambda i, j: (i, 0)),
                  pl.BlockSpec((D_MODEL, th), lambda i, j: (0, j)),
                  pl.BlockSpec((th, D_MODEL), lambda i, j: (j, 0)),
                  pl.BlockSpec((tm, D_MODEL), lambda i, j: (i, 0)),
                  pl.BlockSpec((1, 1, D_MODEL), lambda i, j: (i // per_b, 0, 0))],
        out_specs=pl.BlockSpec((tm, D_MODEL), lambda i, j: (i, 0)),
        scratch_shapes=[pltpu.VMEM((tm, D_MODEL), _f32)],
        compiler_params=pltpu.CompilerParams(dimension_semantics=("parallel", "arbitrary"),
                                             vmem_limit_bytes=VMEM_LIMIT),
        name="mlp",
    )(h, w1, w2, x, g)


def _final_norm_kernel(x_ref, w_ref, o_ref):
    x = x_ref[...]
    o_ref[...] = x * lax.rsqrt(jnp.mean(x * x, axis=-1, keepdims=True) + EPS) * w_ref[...]


def _final_norm(x, w, tm=512):
    return pl.pallas_call(
        _final_norm_kernel,
        out_shape=jax.ShapeDtypeStruct((TOKENS, D_MODEL), _f32),
        grid=(TOKENS // tm,),
        in_specs=[pl.BlockSpec((tm, D_MODEL), lambda i: (i, 0)),
                  pl.BlockSpec((1, D_MODEL), lambda i: (0, 0))],
        out_specs=pl.BlockSpec((tm, D_MODEL), lambda i: (i, 0)),
        compiler_params=pltpu.CompilerParams(dimension_semantics=("parallel",),
                                             vmem_limit_bytes=VMEM_LIMIT),
        name="final_norm",
    )(x, w)


NSA_R = NSA_HEADS // NSA_KV_GROUPS
NSA_CMP_ROWS = SEQ // CMP_STRIDE
NSA_NC = NSA_CMP_ROWS - CMP_BLOCK // CMP_STRIDE + 1
NSA_NSB = SEQ // SLC_BLOCK
NSA_NQB = SEQ // Q_BLOCK
assert NSA_CMP_ROWS == LANE and Q_BLOCK == LANE and LANE % NSA_NSB == 0 and SLC_TOP_N <= NSA_NSB
assert NSA_KV_GROUPS * NSA_HEAD_DIM == LANE and CMP_BLOCK == 2 * CMP_STRIDE


def _bucket_value(tab_ref, h, rel):
    exact = REL_BUCKETS // 2
    n = jnp.maximum(rel, 0)
    large = exact + (jnp.log(jnp.maximum(n, 1).astype(_f32) / exact)
                     / math.log(REL_MAX_DIST / exact) * (REL_BUCKETS - exact)).astype(jnp.int32)
    bucket = jnp.where(n < exact, n, jnp.minimum(large, REL_BUCKETS - 1))
    val = jnp.full(rel.shape, tab_ref[0, h], _f32)
    for b in range(1, REL_BUCKETS):
        val = jnp.where(bucket == b, tab_ref[b, h], val)
    return val


def _nsa_bias_kernel(tab_ref, tb_ref, cb_ref):
    h = pl.program_id(0)
    ql = lax.broadcasted_iota(jnp.int32, (Q_BLOCK, LANE), 0)
    kl = lax.broadcasted_iota(jnp.int32, (Q_BLOCK, LANE), 1)
    for d in range(3):
        tb_ref[d, 0] = _bucket_value(tab_ref, h, ql - kl + d * Q_BLOCK)
    cmp_end = kl * CMP_STRIDE + (CMP_BLOCK - 1)
    for qb in range(NSA_NQB):
        cb_ref[qb, 0] = _bucket_value(tab_ref, h, qb * Q_BLOCK + ql - cmp_end)


def _nsa_bias_tiles(rel_bias):
    assert 2 * Q_BLOCK >= REL_MAX_DIST
    return pl.pallas_call(
        _nsa_bias_kernel,
        out_shape=(jax.ShapeDtypeStruct((3, NSA_HEADS, Q_BLOCK, LANE), _f32),
                   jax.ShapeDtypeStruct((NSA_NQB, NSA_HEADS, Q_BLOCK, LANE), _f32)),
        grid=(NSA_HEADS,),
        in_specs=[pl.BlockSpec(memory_space=pltpu.SMEM)],
        out_specs=(pl.BlockSpec((3, 1, Q_BLOCK, LANE), lambda h: (0, h, 0, 0)),
                   pl.BlockSpec((NSA_NQB, 1, Q_BLOCK, LANE), lambda h: (0, h, 0, 0))),
        compiler_params=pltpu.CompilerParams(dimension_semantics=("parallel",),
                                             vmem_limit_bytes=VMEM_LIMIT),
        name="nsa_bias_tiles",
    )(rel_bias)


def _nsa_cmp_kernel(tk_ref, tv_ref, pos_ref, w1_ref, w2_ref, kc_ref, vc_ref):
    rows = lax.broadcasted_iota(jnp.int32, (NSA_CMP_ROWS, LANE), 0)
    for idx, (t_ref, o_ref) in enumerate(((tk_ref, kc_ref), (tv_ref, vc_ref))):
        t = t_ref[0]
        u = jnp.dot((t + pos_ref[idx, 0]).astype(_bf16), w1_ref[idx, 0], preferred_element_type=_f32)
        v = jnp.dot((t + pos_ref[idx, 1]).astype(_bf16), w1_ref[idx, 1], preferred_element_type=_f32)
        pre = u + pltpu.roll(v, NSA_CMP_ROWS - 1, 0)
        act = pre * jax.nn.sigmoid(pre)
        out = jnp.dot(act.astype(_bf16), w2_ref[idx], preferred_element_type=_f32)
        o_ref[0] = jnp.where(rows < NSA_NC, out, 0.0).astype(_bf16)


def _nsa_compress(tk, tv, pos_x, w1_x, w2_x):
    B = tk.shape[0]
    wide = CMP_STRIDE * LANE
    tspec = pl.BlockSpec((1, NSA_CMP_ROWS, wide), lambda b: (b, 0, 0))
    ospec = pl.BlockSpec((1, NSA_CMP_ROWS, LANE), lambda b: (b, 0, 0))
    return pl.pallas_call(
        _nsa_cmp_kernel,
        out_shape=(jax.ShapeDtypeStruct((B, NSA_CMP_ROWS, LANE), _bf16),) * 2,
        grid=(B,),
        in_specs=[tspec, tspec,
                  pl.BlockSpec((2, 2, 1, wide), lambda b: (0, 0, 0, 0)),
                  pl.BlockSpec((2, 2, wide, LANE), lambda b: (0, 0, 0, 0)),
                  pl.BlockSpec((2, LANE, LANE), lambda b: (0, 0, 0))],
        out_specs=(ospec, ospec),
        compiler_params=pltpu.CompilerParams(dimension_semantics=("parallel",),
                                             vmem_limit_bytes=VMEM_LIMIT),
        name="nsa_compress",
    )(tk, tv, pos_x, w1_x, w2_x)


def _dot_nt(a, b):
    return lax.dot_general(a, b, (((1,), (1,)), ((), ())), preferred_element_type=_f32)


def _nsa_kernel(q_ref, ks_ref, vs_ref, kw_ref, vw_ref, kc_ref, vc_ref, gate_ref, tb_ref, cb_ref, ov_ref, e4_ref,
                o_ref, m_ref, l_ref, acc_ref, sel_ref):
    R, QB = NSA_R, Q_BLOCK
    qb = pl.program_id(1)
    ql = lax.broadcasted_iota(jnp.int32, (QB, LANE), 0)
    kl = lax.broadcasted_iota(jnp.int32, (QB, LANE), 1)
    t_row = qb * QB + ql
    gates = jax.nn.sigmoid(gate_ref[...])
    heads = [None] * NSA_HEADS

    for g in range(NSA_KV_GROUPS):
        qp = jnp.concatenate([q_ref[:, (R * g + r) * LANE:(R * g + r + 1) * LANE] for r in range(R)], axis=0)

        def flash(k_ref, v_ref, c_lo, c_hi, selected):
            m_ref[...] = jnp.full(m_ref.shape, NEG_INF, _f32)
            l_ref[...] = jnp.zeros(l_ref.shape, _f32)
            acc_ref[...] = jnp.zeros(acc_ref.shape, _f32)

            def body(c, carry):
                off = pl.multiple_of(c * QB, QB)
                s3 = _dot_nt(qp, k_ref[pl.ds(off, QB), :]).reshape(R, QB, LANE)
                s3 = s3 + tb_ref[jnp.minimum(qb - c, 2), R * g:R * (g + 1)]
                rel = t_row - (c * QB + kl)
                if selected:
                    mask = (rel >= 0) & (sel_ref[:, pl.ds(off, QB)] > 0.5)
                else:
                    mask = (rel >= 0) & (rel < WINDOW)
                s3 = jnp.where(mask[None], s3, NEG_INF)
                m_old = m_ref[...]
                m_new = jnp.maximum(m_old, jnp.max(s3, axis=-1, keepdims=True))
                e = jnp.where(mask[None], jnp.exp(s3 - m_new), 0.0)
                alpha = jnp.exp(m_old - m_new)
                l_ref[...] = alpha * l_ref[...] + jnp.sum(e, axis=-1, keepdims=True)
                pv = jnp.dot(e.astype(_bf16).reshape(R * QB, LANE), v_ref[pl.ds(off, QB), :],
                             preferred_element_type=_f32)
                acc_ref[...] = alpha * acc_ref[...] + pv.reshape(R, QB, LANE)
                m_ref[...] = m_new
                return carry

            lax.fori_loop(c_lo, c_hi, body, 0)
            return acc_ref[...] / l_ref[...]

        s3 = _dot_nt(qp, kc_ref[0]).reshape(R, QB, LANE) + cb_ref[0, R * g:R * (g + 1)]
        mask_c = (t_row - (kl * CMP_STRIDE + CMP_BLOCK - 1) >= 0) & (kl < NSA_NC)
        s3 = jnp.where(mask_c[None], s3, NEG_INF)
        e = jnp.where(mask_c[None], jnp.exp(s3 - jnp.max(s3, axis=-1, keepdims=True)), 0.0)
        den = jnp.sum(e, axis=-1, keepdims=True)
        p = (e / jnp.where(den > 0.0, den, 1.0)).astype(_bf16)
        o_c = jnp.dot(p.reshape(R * QB, LANE), vc_ref[0], preferred_element_type=_f32).reshape(R, QB, LANE)

        imp = jnp.dot(jnp.concatenate([p[r] for r in range(R)], axis=1), ov_ref[...], preferred_element_type=_f32)
        j = kl & (NSA_NSB - 1)
        cur = t_row // SLC_BLOCK
        imp = jnp.where((j == 0) | (j == cur) | (j == cur - 1), FORCE_SCORE, imp)
        imp = jnp.where(j <= cur, imp, NEG_INF)
        cnt = jnp.zeros((QB, LANE), _f32)
        for sft in range(1, NSA_NSB):
            other = pltpu.roll(imp, sft, 1)
            beats = (other > imp) | ((other == imp) & (j >= sft))
            cnt = cnt + jnp.where(beats, 1.0, 0.0)
        sel = jnp.where(cnt < SLC_TOP_N, 1.0, 0.0).astype(_bf16)
        sel_ref[...] = jnp.dot(sel, e4_ref[...], preferred_element_type=_f32)

        o_s = flash(ks_ref, vs_ref, 0, qb + 1, True)
        o_w = flash(kw_ref, vw_ref, jnp.maximum(qb - WINDOW // QB, 0), qb + 1, False)

        for r in range(R):
            h = R * g + r
            heads[h] = (gates[:, 3 * h:3 * h + 1] * o_c[r] + gates[:, 3 * h + 1:3 * h + 2] * o_s[r]
                        + gates[:, 3 * h + 2:3 * h + 3] * o_w[r])

    for pk in range(NSA_HEADS // 2):
        even, odd = heads[2 * pk], heads[2 * pk + 1]
        if (2 * pk) // R == 0:
            odd = pltpu.roll(odd, NSA_HEAD_DIM, 1)
        else:
            even = pltpu.roll(even, NSA_HEAD_DIM, 1)
        o_ref[:, pk * LANE:(pk + 1) * LANE] = jnp.where(kl < NSA_HEAD_DIM, even, odd).astype(o_ref.dtype)


def _nsa_attention(proj_a, proj_b, kc, vc, tb, cb, ov4, e4):
    B = kc.shape[0]
    nqb = NSA_NQB
    qcols = NSA_HEADS * LANE
    kv = lambda blk: pl.BlockSpec((SEQ, LANE), lambda b, q: (b, qcols // LANE + blk))
    cspec = pl.BlockSpec((1, NSA_CMP_ROWS, LANE), lambda b, q: (b, 0, 0))
    return pl.pallas_call(
        _nsa_kernel,
        out_shape=jax.ShapeDtypeStruct((B * SEQ, NSA_HEADS * NSA_HEAD_DIM), _bf16),
        grid=(B, nqb),
        in_specs=[pl.BlockSpec((Q_BLOCK, qcols), lambda b, q: (b * nqb + q, 0)),
                  kv(0), kv(1), kv(2), kv(3), cspec, cspec,
                  pl.BlockSpec((Q_BLOCK, LANE), lambda b, q: (b * nqb + q, PB['ngate'] // LANE)),
                  pl.BlockSpec((3, NSA_HEADS, Q_BLOCK, LANE), lambda b, q: (0, 0, 0, 0)),
                  pl.BlockSpec((1, NSA_HEADS, Q_BLOCK, LANE), lambda b, q: (q, 0, 0, 0)),
                  pl.BlockSpec((NSA_R * LANE, LANE), lambda b, q: (0, 0)),
                  pl.BlockSpec((LANE, SEQ), lambda b, q: (0, 0))],
        out_specs=pl.BlockSpec((Q_BLOCK, NSA_HEADS * NSA_HEAD_DIM), lambda b, q: (b * nqb + q, 0)),
        scratch_shapes=[pltpu.VMEM((NSA_R, Q_BLOCK, 1), _f32),
                        pltpu.VMEM((NSA_R, Q_BLOCK, 1), _f32),
                        pltpu.VMEM((NSA_R, Q_BLOCK, LANE), _f32),
                        pltpu.VMEM((Q_BLOCK, SEQ), _f32)],
        compiler_params=pltpu.CompilerParams(dimension_semantics=("parallel", "arbitrary"),
                                             vmem_limit_bytes=VMEM_LIMIT),
        name="nsa_attention",
    )(proj_a, proj_a, proj_a, proj_a, proj_a, kc, vc, proj_b, tb, cb, ov4, e4)


def _nsa_constants():
    n = jnp.arange(NSA_CMP_ROWS)[:, None]
    jj = jnp.arange(LANE)[None, :] % NSA_NSB
    overlap = ((n * CMP_STRIDE <= jj * SLC_BLOCK + SLC_BLOCK - 1)
               & (n * CMP_STRIDE + CMP_BLOCK - 1 >= jj * SLC_BLOCK) & (n < NSA_NC))
    ov4 = jnp.tile(overlap.astype(_bf16), (NSA_R, 1))
    row = jnp.arange(LANE)[:, None]
    e4 = ((row < NSA_NSB) & (jnp.arange(SEQ)[None, :] // SLC_BLOCK == row)).astype(_bf16)
    return ov4, e4


def _nsa_weight_prep(w_in, cmp_pos, cmp_w1, cmp_w2):
    G, hd = NSA_KV_GROUPS, NSA_HEAD_DIM
    eye = jnp.eye(G, dtype=_f32)
    nq = NSA_HEADS * hd
    wq = w_in[:, :, :nq].reshape(DEPTH, D_MODEL, NSA_HEADS, 1, hd) * (hd ** -0.5)
    head_group = (jnp.arange(NSA_HEADS)[:, None] // NSA_R == jnp.arange(G)[None, :]).astype(_f32)
    wq = (wq * head_group[None, None, :, :, None]).reshape(DEPTH, D_MODEL, NSA_HEADS * LANE)
    slc_win = w_in[:, :, nq + 2 * NSA_KV:nq + 6 * NSA_KV]
    w_a = jnp.concatenate([wq, slc_win], axis=-1).astype(_bf16)
    w_b = jnp.concatenate([jnp.pad(w_in[:, :, _IN_OFF[n]:_IN_OFF[n] + w], ((0, 0), (0, 0), (0, pw - w)))
                           for n, w, pw in _PB_SEGMENTS], axis=-1).astype(_bf16)
    half = CMP_BLOCK // 2
    w1 = cmp_w1.reshape(DEPTH, 2, CMP_BLOCK, hd, hd)
    w1_x = jnp.einsum('lxide,gh->lxigdhe', w1, eye).reshape(DEPTH, 2, 2, half * G * hd, G * hd).astype(_bf16)
    w2_x = jnp.einsum('lxde,gh->lxgdhe', cmp_w2, eye).reshape(DEPTH, 2, G * hd, G * hd).astype(_bf16)
    pos_x = jnp.broadcast_to(cmp_pos[:, :, :, None, :], (DEPTH, 2, CMP_BLOCK, G, hd))
    pos_x = pos_x.reshape(DEPTH, 2, 2, 1, half * G * hd)
    return w_a, w_b, w1_x, w2_x, pos_x


REC_BLOCK = 2 * CHUNK
assert REC_BLOCK == LANE and SEQ % REC_BLOCK == 0


def _split3(x):
    hi = x.astype(_bf16)
    r = x - hi.astype(_f32)
    mid = r.astype(_bf16)
    lo = (r - mid.astype(_f32)).astype(_bf16)
    return hi, mid, lo


def _dot_exact_rhs(m, x):
    return sum(jnp.dot(m, p, preferred_element_type=_f32) for p in _split3(x))


def _dot_exact_lhs(x, m):
    return sum(jnp.dot(p, m, preferred_element_type=_f32) for p in _split3(x))


def _dot_tn(a, b):
    return lax.dot_general(a, b, (((0,), (0,)), ((), ())), preferred_element_type=_f32)


def _mm_hp(x, y):
    xh = x.astype(_bf16)
    xl = (x - xh.astype(_f32)).astype(_bf16)
    yh = y.astype(_bf16)
    yl = (y - yh.astype(_f32)).astype(_bf16)
    return (jnp.dot(xh, yh, preferred_element_type=_f32) + jnp.dot(xh, yl, preferred_element_type=_f32)
            + jnp.dot(xl, yh, preferred_element_type=_f32))


def _chunk_masks():
    ri = lax.broadcasted_iota(jnp.int32, (REC_BLOCK, REC_BLOCK), 0)
    ci = lax.broadcasted_iota(jnp.int32, (REC_BLOCK, REC_BLOCK), 1)
    same = (ri // CHUNK) == (ci // CHUNK)
    return ri, ci, (ci <= ri) & same, (ci < ri) & same, (ri <= ci) & same


def _as_mxu(mask):
    return jnp.where(mask, 1.0, 0.0).astype(_bf16)


def _softplus(x):
    return jnp.maximum(x, 0.0) + jnp.log1p(jnp.exp(-jnp.abs(x)))


def _silu(x):
    return x * jax.nn.sigmoid(x)


def _chunk_last(x):
    ri = lax.broadcasted_iota(jnp.int32, x.shape, 0)
    return jnp.where(ri < CHUNK, x[CHUNK - 1:CHUNK], x[2 * CHUNK - 1:2 * CHUNK])


def _expand_heads(x, nheads, width):
    per = LANE // width
    lane = lax.broadcasted_iota(jnp.int32, (x.shape[0], LANE), 1)
    pieces = []
    for p0 in range(0, nheads, per):
        piece = jnp.broadcast_to(x[:, p0:p0 + 1], (x.shape[0], LANE))
        for k in range(1, per):
            piece = jnp.where(lane < k * width, piece, jnp.broadcast_to(x[:, p0 + k:p0 + k + 1], (x.shape[0], LANE)))
        pieces.append(piece)
    return jnp.concatenate(pieces, axis=1) if len(pieces) > 1 else pieces[0]


def _conv_silu(x, prev, w_ref, c0, bias=None):
    n, C = x.shape
    ntap = w_ref.shape[0]
    rows = lax.broadcasted_iota(jnp.int32, (n, C), 0)
    acc = x * w_ref[ntap - 1:ntap, c0:c0 + C]
    for s in range(1, ntap):
        xs = jnp.where(rows < s, pltpu.roll(prev, s, 0), pltpu.roll(x, s, 0))
        acc = acc + xs * w_ref[ntap - 1 - s:ntap - s, c0:c0 + C]
    if bias is not None:
        acc = acc + bias
    return _silu(acc)


def _rms(x, w):
    return x * lax.rsqrt(jnp.mean(x * x, axis=-1, keepdims=True) + EPS) * w


def _ssd_kernel(z_ref, xbc_ref, dt_ref, cw_ref, cb_ref, dtb_ref, alog_ref, dskip_ref, nw_ref, o_ref,
                prev_ref, state_ref):
    G, R, P, N = SSD_GROUPS, SSD_HEADS // SSD_GROUPS, SSD_HEAD_DIM, SSD_STATE

    @pl.when(pl.program_id(1) == 0)
    def _():
        prev_ref[...] = jnp.zeros(prev_ref.shape, _f32)
        state_ref[...] = jnp.zeros(state_ref.shape, _f32)

    x_in = xbc_ref[...]
    xc = _conv_silu(x_in, prev_ref[...], cw_ref, 0, cb_ref[...])
    prev_ref[...] = x_in
    _, _, tril, _, triu = _chunk_masks()
    lane = lax.broadcasted_iota(jnp.int32, (REC_BLOCK, LANE), 1)
    dt = _softplus(dt_ref[...] + dtb_ref[...])
    da = dt * (-jnp.exp(alog_ref[...]))
    a_cum = _dot_exact_rhs(_as_mxu(tril), da)
    a_cum_t = _dot_exact_lhs(da.T, _as_mxu(triu))
    a_last = _chunk_last(a_cum)
    xs = xc[:, :SSD_INNER]
    xdt = xs * _expand_heads(dt, SSD_HEADS, P)
    xdtd = (xdt * _expand_heads(jnp.exp(a_last - a_cum), SSD_HEADS, P)).astype(_bf16)
    xdt_b = xdt.astype(_bf16)
    ea = _expand_heads(jnp.exp(a_cum), SSD_HEADS, P)
    y_groups = []
    for g in range(G):
        bm = xc[:, SSD_INNER + g * N:SSD_INNER + (g + 1) * N].astype(_bf16)
        cm = xc[:, SSD_INNER + (G + g) * N:SSD_INNER + (G + g + 1) * N].astype(_bf16)
        cbm = _dot_nt(cm, bm)
        intra = []
        for pr in range(R // 2):
            both = []
            for k in range(2):
                h = g * R + 2 * pr + k
                seg = jnp.exp(jnp.where(tril, a_cum[:, h:h + 1] - a_cum_t[h:h + 1, :], -jnp.inf))
                both.append(jnp.dot((cbm * seg).astype(_bf16), xdt_b[:, (h - k) * P:(h - k + 2) * P],
                                    preferred_element_type=_f32))
            intra.append(jnp.where(lane < P, both[0], both[1]))
        y_intra = jnp.concatenate(intra, axis=1)
        prev_rows = []
        for c in range(REC_BLOCK // CHUNK):
            rows = slice(c * CHUNK, (c + 1) * CHUNK)
            st = state_ref[g]
            prev_rows.append(jnp.dot(cm[rows], st.astype(_bf16), preferred_element_type=_f32))
            dec = _expand_heads(jnp.exp(a_cum[(c + 1) * CHUNK - 1:(c + 1) * CHUNK]), SSD_HEADS, P)
            state_ref[g] = (st * dec[:, g * R * P:(g + 1) * R * P]
                            + _dot_tn(bm[rows], xdtd[rows, g * R * P:(g + 1) * R * P]))
        y_groups.append(y_intra + jnp.concatenate(prev_rows, axis=0) * ea[:, g * R * P:(g + 1) * R * P])
    y = jnp.concatenate(y_groups, axis=1) + xs * dskip_ref[...]
    y = y * _silu(z_ref[...])
    gw = SSD_INNER // G
    for g in range(G):
        o_ref[:, g * gw:(g + 1) * gw] = _rms(y[:, g * gw:(g + 1) * gw], nw_ref[:, g * gw:(g + 1) * gw]).astype(o_ref.dtype)


def _gdn_kernel(q_ref, k_ref, v_ref, z_ref, ba_ref, cw_ref, dtb_ref, alog_ref, nw_ref, o_ref,
                pq_ref, pk_ref, pv_ref, state_ref):
    H, Dh = GDN_HEADS, GDN_HEAD_DIM

    @pl.when(pl.program_id(1) == 0)
    def _():
        for r in (pq_ref, pk_ref, pv_ref, state_ref):
            r[...] = jnp.zeros(r.shape, _f32)

    conv = []
    for i, (x_ref, p_ref) in enumerate(((q_ref, pq_ref), (k_ref, pk_ref), (v_ref, pv_ref))):
        x_in = x_ref[...]
        conv.append(_conv_silu(x_in, p_ref[...], cw_ref, i * GDN_WIDTH))
        p_ref[...] = x_in
    q, k, v = conv
    _, _, tril, strict, triu = _chunk_masks()
    ba = ba_ref[...]
    beta = jax.nn.sigmoid(ba)
    gl = -jnp.exp(alog_ref[...]) * _softplus(ba + dtb_ref[...])
    gcum = _dot_exact_rhs(_as_mxu(tril), gl)
    gcum_t = _dot_exact_lhs(gl.T, _as_mxu(triu))
    glast = _chunk_last(gcum)
    zero_rows = jnp.zeros((CHUNK, Dh), _f32)
    eye = jnp.where(tril & jnp.logical_not(strict), 1.0, 0.0)
    for h in range(H):
        sl = slice(h * Dh, (h + 1) * Dh)
        qh, kh, vv = q[:, sl], k[:, sl], v[:, sl]
        qq = qh * lax.rsqrt(jnp.sum(qh * qh, axis=-1, keepdims=True) + EPS) * (Dh ** -0.5)
        kk = kh * lax.rsqrt(jnp.sum(kh * kh, axis=-1, keepdims=True) + EPS)
        bcol = beta[:, h:h + 1]
        gcol = gcum[:, H + h:H + h + 1]
        decay = jnp.exp(jnp.where(tril, gcol - gcum_t[H + h:H + h + 1, :], -jnp.inf))
        kb = kk * bcol
        s = _dot_nt(jnp.concatenate([kb, qq], axis=0).astype(_bf16), kk.astype(_bf16))
        a_mat = jnp.where(strict, s[:REC_BLOCK] * decay, 0.0)
        aqk = (s[REC_BLOCK:] * decay).astype(_bf16)
        tinv = eye - a_mat
        pw = a_mat
        for _ in range(CHUNK.bit_length() - 2):
            pw = _mm_hp(pw, pw)
            tinv = tinv + _mm_hp(tinv, pw)
        sol = _mm_hp(tinv, jnp.concatenate([vv * bcol, kb * jnp.exp(gcol)], axis=1))
        u, w = sol[:, :Dh], sol[:, Dh:].astype(_bf16)
        q_dec = (qq * jnp.exp(gcol)).astype(_bf16)
        k_end = (kk * jnp.exp(glast[:, H + h:H + h + 1] - gcol)).astype(_bf16)
        o_rows = []
        for c in range(REC_BLOCK // CHUNK):
            rows = slice(c * CHUNK, (c + 1) * CHUNK)
            st = state_ref[h]
            st_b = st.astype(_bf16)
            v_new = u[rows] - jnp.dot(w[rows], st_b, preferred_element_type=_f32)
            v_full = jnp.concatenate([v_new, zero_rows] if c == 0 else [zero_rows, v_new], axis=0).astype(_bf16)
            o_rows.append(jnp.dot(q_dec[rows], st_b, preferred_element_type=_f32)
                          + jnp.dot(aqk[rows], v_full, preferred_element_type=_f32))
            d_last = jnp.exp(gcum[(c + 1) * CHUNK - 1:(c + 1) * CHUNK, H + h:H + h + 1])
            state_ref[h] = st * d_last + _dot_tn(k_end[rows], v_new.astype(_bf16))
        o = _rms(jnp.concatenate(o_rows, axis=0), nw_ref[...]) * _silu(z_ref[:, sl])
        o_ref[:, sl] = o.astype(o_ref.dtype)


def _gla_kernel(q_ref, k_ref, v_ref, go_ref, lr_ref, w2_ref, gb_ref, nw_ref, o_ref, state_ref):
    H, Dk, Dv = GLA_HEADS, GLA_DK, GLA_DV

    @pl.when(pl.program_id(1) == 0)
    def _():
        state_ref[...] = jnp.zeros(state_ref.shape, _f32)

    _, _, tril, _, _ = _chunk_masks()
    lane = lax.broadcasted_iota(jnp.int32, (REC_BLOCK, LANE), 1)
    pre = jnp.dot(lr_ref[...].astype(_bf16), w2_ref[...], preferred_element_type=_f32) + gb_ref[...]
    gk = (jnp.minimum(pre, 0.0) - jnp.log1p(jnp.exp(-jnp.abs(pre)))) / GLA_GATE_NORM
    bcum = _dot_exact_rhs(_as_mxu(tril), gk)
    blast = _chunk_last(bcum)
    q_dec = q_ref[...] * (Dk ** -0.5) * jnp.exp(bcum)
    k_inv = (k_ref[...] * jnp.exp(-bcum)).astype(_bf16)
    k_end = (k_ref[...] * jnp.exp(blast - bcum)).astype(_bf16)
    per = LANE // Dk
    for pr in range(H // per):
        psl = slice(pr * LANE, (pr + 1) * LANE)
        qd, ki, ke = q_dec[:, psl], k_inv[:, psl], k_end[:, psl]
        qm, vh, o_intra = [], [], []
        for k in range(per):
            h = pr * per + k
            qm.append(jnp.where(lane // Dk == k, qd, 0.0).astype(_bf16))
            vh.append(v_ref[:, h * Dv:(h + 1) * Dv].astype(_bf16))
            attn = jnp.where(tril, _dot_nt(qm[k], ki), 0.0).astype(_bf16)
            o_intra.append(jnp.dot(attn, vh[k], preferred_element_type=_f32))
        o_prev = [[] for _ in range(per)]
        for c in range(REC_BLOCK // CHUNK):
            rows = slice(c * CHUNK, (c + 1) * CHUNK)
            st = state_ref[pr]
            st_b = st.astype(_bf16)
            loc = None
            for k in range(per):
                o_prev[k].append(_dot_nt(qm[k][rows], st_b))
                lk = _dot_tn(vh[k][rows], ke[rows])
                loc = lk if loc is None else jnp.where(lane < k * Dk, loc, lk)
            state_ref[pr] = st * jnp.exp(bcum[(c + 1) * CHUNK - 1:(c + 1) * CHUNK, psl]) + loc
        for k in range(per):
            h = pr * per + k
            o = o_intra[k] + jnp.concatenate(o_prev[k], axis=0)
            o = _rms(o, nw_ref[...]) * _silu(go_ref[:, h * Dv:(h + 1) * Dv])
            o_ref[:, h * Dv:(h + 1) * Dv] = o.astype(o_ref.dtype)


def _rec_call(body, proj_b, col_blocks, params, out_cols, scratch, name):
    B = proj_b.shape[0] // SEQ
    nblk = SEQ // REC_BLOCK
    in_specs = [pl.BlockSpec((REC_BLOCK, w), (lambda b, t, c=c0 // w: (b * nblk + t, c))) for c0, w in col_blocks]
    for p in params:
        in_specs.append(pl.BlockSpec(p.shape, lambda b, t, nd=p.ndim: (0,) * nd))
    return pl.pallas_call(
        body,
        out_shape=jax.ShapeDtypeStruct((B * SEQ, out_cols), _bf16),
        grid=(B, nblk),
        in_specs=in_specs,
        out_specs=pl.BlockSpec((REC_BLOCK, out_cols), lambda b, t: (b * nblk + t, 0)),
        scratch_shapes=scratch,
        compiler_params=pltpu.CompilerParams(dimension_semantics=("parallel", "arbitrary"),
                                             vmem_limit_bytes=VMEM_LIMIT),
        name=name,
    )(*([proj_b] * len(col_blocks)), *params)


def _lane_pad(v, lane0=0):
    return jnp.pad(v.astype(_f32), (lane0, LANE - lane0 - v.shape[0]))[None]


def _ssd_call(proj_b, conv_w, conv_b, dt_bias, a_log, d_skip, norm_w):
    cols = [(PB['sz'], SSD_INNER), (PB['sxbc'], SSD_XBC), (PB['sdt'], LANE)]
    params = [conv_w, conv_b[None], _lane_pad(dt_bias), _lane_pad(a_log),
              jnp.repeat(d_skip, SSD_HEAD_DIM)[None], norm_w[None]]
    scratch = [pltpu.VMEM((REC_BLOCK, SSD_XBC), _f32),
               pltpu.VMEM((SSD_GROUPS, SSD_STATE, SSD_INNER // SSD_GROUPS), _f32)]
    return _rec_call(_ssd_kernel, proj_b, cols, params, SSD_INNER, scratch, "ssd")


def _gdn_call(proj_b, conv_w, dt_bias, a_log, norm_w):
    W = GDN_WIDTH
    cols = [(PB['gq'], W), (PB['gk'], W), (PB['gv'], W), (PB['gz'], W), (PB['gbeta'], LANE)]
    params = [conv_w, _lane_pad(dt_bias, GDN_HEADS), _lane_pad(a_log, GDN_HEADS), norm_w[None]]
    scratch = [pltpu.VMEM((REC_BLOCK, W), _f32)] * 3 + [pltpu.VMEM((GDN_HEADS, GDN_HEAD_DIM, GDN_HEAD_DIM), _f32)]
    return _rec_call(_gdn_kernel, proj_b, cols, params, W, scratch, "gdn")


def _gla_call(proj_b, gate_w2, gate_b, norm_w):
    cols = [(PB['lq'], GLA_KEY), (PB['lk'], GLA_KEY), (PB['lv'], GLA_VAL), (PB['lg'], GLA_VAL), (PB['llr'], LANE)]
    w2 = jnp.pad(gate_w2, ((0, LANE - GLA_GATE_RANK), (0, 0))).astype(_bf16)
    params = [w2, gate_b[None], norm_w[None]]
    scratch = [pltpu.VMEM((GLA_HEADS * GLA_DK // LANE, GLA_DV, LANE), _f32)]
    return _rec_call(_gla_kernel, proj_b, cols, params, GLA_VAL, scratch, "gla")


def kernel(x, c, rel_bias, norm1_w, norm2_w, ada_w, ada_b, w_in, w_out, nsa_cmp_pos, nsa_cmp_w1, nsa_cmp_w2, ssd_conv_w, ssd_conv_b, ssd_dt_bias, ssd_a_log, ssd_d, ssd_norm_w, gdn_conv_w, gdn_dt_bias, gdn_a_log, gdn_norm_w, gla_gate_w2, gla_gate_b, gla_norm_w, mlp_w1, mlp_w2, final_norm_w):
    B, S, D = x.shape
    mod = _ada_all(c, ada_w, ada_b).reshape(DEPTH, B, 6, 1, D)
    w_a, w_b, cmp_w1_x, cmp_w2_x, cmp_pos_x = _nsa_weight_prep(w_in, nsa_cmp_pos, nsa_cmp_w1, nsa_cmp_w2)
    w_out_b = w_out.astype(_bf16)
    w1_b = mlp_w1.astype(_bf16)
    w2_b = mlp_w2.astype(_bf16)
    tb, cb = _nsa_bias_tiles(rel_bias)
    ov4, e4 = _nsa_constants()
    xf = x.reshape(TOKENS, D)
    for l in range(DEPTH):
        sh1, sc1, g1, sh2, sc2, g2 = (mod[l, :, i] for i in range(6))
        h = _norm_mod(xf, norm1_w[l][None], sc1, sh1)
        proj_a = _matmul(h, w_a[l], out_dtype=_bf16)
        proj_b = _matmul(h, w_b[l], tn=PROJ_B_TN)
        tk = proj_b[:, PB['nkc']:PB['nkc'] + NSA_KV].reshape(B, NSA_CMP_ROWS, CMP_STRIDE * NSA_KV)
        tv = proj_b[:, PB['nvc']:PB['nvc'] + NSA_KV].reshape(B, NSA_CMP_ROWS, CMP_STRIDE * NSA_KV)
        kc, vc = _nsa_compress(tk, tv, cmp_pos_x[l], cmp_w1_x[l], cmp_w2_x[l])
        y_nsa = _nsa_attention(proj_a, proj_b, kc, vc, tb, cb, ov4, e4)
        y_ssd = _ssd_call(proj_b, ssd_conv_w[l], ssd_conv_b[l], ssd_dt_bias[l], ssd_a_log[l], ssd_d[l], ssd_norm_w[l])
        y_gdn = _gdn_call(proj_b, gdn_conv_w[l], gdn_dt_bias[l], gdn_a_log[l], gdn_norm_w[l])
        y_gla = _gla_call(proj_b, gla_gate_w2[l], gla_gate_b[l], gla_norm_w[l])
        xf, h2 = _out_proj((y_nsa, y_ssd, y_gdn, y_gla), w_out_b[l], xf, g1,
                           norm2_w[l][None], sc2, sh2)
        xf = _mlp(h2, w1_b[l], w2_b[l], xf, g2)
    return _final_norm(xf, final_norm_w[None]).reshape(B, S, D)
```

```python
import math
from functools import partial

import jax
import jax.numpy as jnp
from jax import lax
from jax.experimental import pallas as pl
from jax.experimental.pallas import tpu as pltpu

D_MODEL = 2048
BATCH = 16
SEQ = 2048
DEPTH = 4

MIX_GROUP = D_MODEL // 4
NSA_HEAD_DIM = 64
NSA_HEADS = MIX_GROUP // NSA_HEAD_DIM
NSA_KV_GROUPS = max(1, NSA_HEADS // 4)
NSA_KV = NSA_KV_GROUPS * NSA_HEAD_DIM
CMP_BLOCK = 32
CMP_STRIDE = 16
SLC_BLOCK = 64
SLC_TOP_N = 8
WINDOW = 512
Q_BLOCK = 128
REL_BUCKETS = 32
REL_MAX_DIST = 128
SSD_HEAD_DIM = 64
SSD_HEADS = MIX_GROUP // SSD_HEAD_DIM
SSD_INNER = SSD_HEADS * SSD_HEAD_DIM
SSD_GROUPS = 2
SSD_STATE = 128
SSD_CONV = 4
SSD_XBC = SSD_INNER + 2 * SSD_GROUPS * SSD_STATE
GDN_HEAD_DIM = 128
GDN_HEADS = MIX_GROUP // GDN_HEAD_DIM
GDN_WIDTH = GDN_HEADS * GDN_HEAD_DIM
GDN_CONV = 4
GLA_DV = 128
GLA_HEADS = MIX_GROUP // GLA_DV
GLA_DK = GLA_DV // 2
GLA_KEY = GLA_HEADS * GLA_DK
GLA_VAL = GLA_HEADS * GLA_DV
GLA_GATE_RANK = 16
GLA_GATE_NORM = 16.0
CHUNK = 64
MLP_HIDDEN = 4 * D_MODEL
EPS = 1e-6
NEG_INF = -1e30
FORCE_SCORE = 1e9
IN_SPLITS = (NSA_HEADS * NSA_HEAD_DIM, NSA_KV, NSA_KV, NSA_KV, NSA_KV, NSA_KV, NSA_KV, NSA_HEADS * 3,
             SSD_INNER, SSD_XBC, SSD_HEADS,
             GDN_WIDTH, GDN_WIDTH, GDN_WIDTH, GDN_WIDTH, GDN_HEADS, GDN_HEADS,
             GLA_KEY, GLA_KEY, GLA_VAL, GLA_VAL, GLA_GATE_RANK)
IN_COLS = sum(IN_SPLITS)
MIX_OUT = NSA_HEADS * NSA_HEAD_DIM + SSD_INNER + GDN_WIDTH + GLA_VAL

LANE = 128
VMEM_LIMIT = 56 * 1024 * 1024
TOKENS = BATCH * SEQ
_IN_NAMES = ('nq', 'nkc', 'nvc', 'nks', 'nvs', 'nkw', 'nvw', 'ngate', 'sz', 'sxbc', 'sdt',
             'gq', 'gk', 'gv', 'gz', 'gbeta', 'ga', 'lq', 'lk', 'lv', 'lg', 'llr')
_IN_W = dict(zip(_IN_NAMES, IN_SPLITS))
_IN_OFF = {n: sum(IN_SPLITS[:i]) for i, n in enumerate(_IN_NAMES)}
_PB_SEGMENTS = (('sxbc', SSD_XBC), ('gq', GDN_WIDTH), ('gk', GDN_WIDTH), ('gv', GDN_WIDTH), ('gz', GDN_WIDTH),
                ('sz', SSD_INNER), ('lv', GLA_VAL), ('lg', GLA_VAL), ('lq', GLA_KEY), ('lk', GLA_KEY),
                ('nkc', NSA_KV), ('nvc', NSA_KV),
                ('ngate', NSA_HEADS * 3), ('sdt', SSD_HEADS), ('gbeta', 2 * GDN_HEADS), ('llr', GLA_GATE_RANK))
_PB_MISC = ('ngate', 'sdt', 'gbeta', 'llr')
PB, PBL = {}, {}
PROJ_B_USED = 0
for _n, _w in _PB_SEGMENTS:
    if _n in _PB_MISC:
        PB.setdefault('misc', PROJ_B_USED - PROJ_B_USED % LANE)
        PBL[_n] = PROJ_B_USED - PB['misc']
    else:
        assert PROJ_B_USED % _w == 0 and _w % LANE == 0
        PB[_n] = PROJ_B_USED
    PROJ_B_USED += _w
assert PROJ_B_USED - PB['misc'] <= LANE and PBL['ngate'] == 0 and _IN_OFF['ga'] == _IN_OFF['gbeta'] + GDN_HEADS
PROJ_B_TN = 512
PROJ_B_COLS = -(-PROJ_B_USED // PROJ_B_TN) * PROJ_B_TN

_bf16 = jnp.bfloat16
_f32 = jnp.float32


def _ada_kernel(c_ref, w_ref, b_ref, o_ref):
    c = c_ref[...]
    c_act = c * jax.nn.sigmoid(c)
    o_ref[0] = jnp.dot(c_act, w_ref[0], preferred_element_type=_f32) + b_ref[0]


def _ada_all(c, ada_w, ada_b):
    tn = 1024
    return pl.pallas_call(
        _ada_kernel,
        out_shape=jax.ShapeDtypeStruct((DEPTH, BATCH, 6 * D_MODEL), _f32),
        grid=(DEPTH, 6 * D_MODEL // tn),
        in_specs=[pl.BlockSpec((BATCH, D_MODEL), lambda l, j: (0, 0)),
                  pl.BlockSpec((1, D_MODEL, tn), lambda l, j: (l, 0, j)),
                  pl.BlockSpec((1, 1, tn), lambda l, j: (l, 0, j))],
        out_specs=pl.BlockSpec((1, BATCH, tn), lambda l, j: (l, 0, j)),
        compiler_params=pltpu.CompilerParams(dimension_semantics=("parallel", "parallel"),
                                             vmem_limit_bytes=VMEM_LIMIT),
        name="ada_mod",
    )(c, ada_w, ada_b.reshape(DEPTH, 1, 6 * D_MODEL))


def _rms_mod(x, w, sc, sh):
    y = x * lax.rsqrt(jnp.mean(x * x, axis=-1, keepdims=True) + EPS)
    return (y * w) * (1.0 + sc) + sh


def _norm_mod_kernel(x_ref, w_ref, sc_ref, sh_ref, o_ref):
    o_ref[...] = _rms_mod(x_ref[...], w_ref[...], sc_ref[0], sh_ref[0]).astype(o_ref.dtype)


def _norm_mod(x, w, sc, sh, tm=512):
    per_b = SEQ // tm
    return pl.pallas_call(
        _norm_mod_kernel,
        out_shape=jax.ShapeDtypeStruct((TOKENS, D_MODEL), _bf16),
        grid=(TOKENS // tm,),
        in_specs=[pl.BlockSpec((tm, D_MODEL), lambda i: (i, 0)),
                  pl.BlockSpec((1, D_MODEL), lambda i: (0, 0)),
                  pl.BlockSpec((1, 1, D_MODEL), lambda i: (i // per_b, 0, 0)),
                  pl.BlockSpec((1, 1, D_MODEL), lambda i: (i // per_b, 0, 0))],
        out_specs=pl.BlockSpec((tm, D_MODEL), lambda i: (i, 0)),
        compiler_params=pltpu.CompilerParams(dimension_semantics=("parallel",),
                                             vmem_limit_bytes=VMEM_LIMIT),
        name="norm_mod",
    )(x, w, sc, sh)


def _matmul_kernel(a_ref, w_ref, o_ref):
    o_ref[...] = jnp.dot(a_ref[...], w_ref[...], preferred_element_type=_f32).astype(o_ref.dtype)


def _matmul(a, w, tm=1024, tn=512, out_dtype=_f32):
    M, K = a.shape
    N = w.shape[1]
    return pl.pallas_call(
        _matmul_kernel,
        out_shape=jax.ShapeDtypeStruct((M, N), out_dtype),
        grid=(M // tm, N // tn),
        in_specs=[pl.BlockSpec((tm, K), lambda i, j: (i, 0)),
                  pl.BlockSpec((K, tn), lambda i, j: (0, j))],
        out_specs=pl.BlockSpec((tm, tn), lambda i, j: (i, j)),
        compiler_params=pltpu.CompilerParams(dimension_semantics=("parallel", "parallel"),
                                             vmem_limit_bytes=VMEM_LIMIT),
        name="in_proj",
    )(a, w)


def _out_proj_kernel(a0_ref, a1_ref, a2_ref, a3_ref, w_ref, x_ref, g_ref, nw_ref, sc_ref, sh_ref, xo_ref, ho_ref):
    y = None
    for i, a_ref in enumerate((a0_ref, a1_ref, a2_ref, a3_ref)):
        part = jnp.dot(a_ref[...], w_ref[i * MIX_GROUP:(i + 1) * MIX_GROUP, :], preferred_element_type=_f32)
        y = part if y is None else y + part
    xn = x_ref[...] + g_ref[0] * y
    xo_ref[...] = xn
    ho_ref[...] = _rms_mod(xn, nw_ref[...], sc_ref[0], sh_ref[0]).astype(ho_ref.dtype)


def _out_proj(mixed, w, x, g, nw, sc, sh, tm=512):
    per_b = SEQ // tm
    bspec = pl.BlockSpec((1, 1, D_MODEL), lambda i: (i // per_b, 0, 0))
    aspec = pl.BlockSpec((tm, MIX_GROUP), lambda i: (i, 0))
    return pl.pallas_call(
        _out_proj_kernel,
        out_shape=(jax.ShapeDtypeStruct((TOKENS, D_MODEL), _f32),
                   jax.ShapeDtypeStruct((TOKENS, D_MODEL), _bf16)),
        grid=(TOKENS // tm,),
        in_specs=[aspec, aspec, aspec, aspec,
                  pl.BlockSpec((MIX_OUT, D_MODEL), lambda i: (0, 0)),
                  pl.BlockSpec((tm, D_MODEL), lambda i: (i, 0)),
                  bspec,
                  pl.BlockSpec((1, D_MODEL), lambda i: (0, 0)),
                  bspec, bspec],
        out_specs=(pl.BlockSpec((tm, D_MODEL), lambda i: (i, 0)),
                   pl.BlockSpec((tm, D_MODEL), lambda i: (i, 0))),
        compiler_params=pltpu.CompilerParams(dimension_semantics=("parallel",),
                                             vmem_limit_bytes=VMEM_LIMIT),
        name="out_proj",
    )(*mixed, w, x, g, nw, sc, sh)


def _mlp_kernel(h_ref, w1_ref, w2_ref, x_ref, g_ref, nw_ref, sc_ref, sh_ref, o_ref, hn_ref, acc_ref):
    j = pl.program_id(1)

    @pl.when(j == 0)
    def _():
        acc_ref[...] = jnp.zeros_like(acc_ref)

    u = jnp.dot(h_ref[...], w1_ref[...], preferred_element_type=_f32)
    u = jnp.square(jnp.maximum(u, 0.0)).astype(_bf16)
    acc_ref[...] += jnp.dot(u, w2_ref[...], preferred_element_type=_f32)

    @pl.when(j == pl.num_programs(1) - 1)
    def _():
        xn = x_ref[...] + g_ref[0] * acc_ref[...]
        o_ref[...] = xn
        hn_ref[...] = _rms_mod(xn, nw_ref[...], sc_ref[0], sh_ref[0]).astype(hn_ref.dtype)


def _mlp(h, w1, w2, x, g, nw, sc, sh, next_dtype, tm=512, th=512):
    per_b = SEQ // tm
    bspec = pl.BlockSpec((1, 1, D_MODEL), lambda i, j: (i // per_b, 0, 0))
    xspec = pl.BlockSpec((tm, D_MODEL), lambda i, j: (i, 0))
    return pl.pallas_call(
        _mlp_kernel,
        out_shape=(jax.ShapeDtypeStruct((TOKENS, D_MODEL), _f32),
                   jax.ShapeDtypeStruct((TOKENS, D_MODEL), next_dtype)),
        grid=(TOKENS // tm, MLP_HIDDEN // th),
        in_specs=[xspec,
                  pl.BlockSpec((D_MODEL, th), lambda i, j: (0, j)),
                  pl.BlockSpec((th, D_MODEL), lambda i, j: (j, 0)),
                  xspec, bspec,
                  pl.BlockSpec((1, D_MODEL), lambda i, j: (0, 0)),
                  bspec, bspec],
        out_specs=(xspec, xspec),
        scratch_shapes=[pltpu.VMEM((tm, D_MODEL), _f32)],
        compiler_params=pltpu.CompilerParams(dimension_semantics=("parallel", "arbitrary"),
                                             vmem_limit_bytes=VMEM_LIMIT),
        name="mlp",
    )(h, w1, w2, x, g, nw, sc, sh)


NSA_R = NSA_HEADS // NSA_KV_GROUPS
NSA_CMP_ROWS = SEQ // CMP_STRIDE
NSA_NC = NSA_CMP_ROWS - CMP_BLOCK // CMP_STRIDE + 1
NSA_NSB = SEQ // SLC_BLOCK
NSA_NQB = SEQ // Q_BLOCK
assert NSA_CMP_ROWS == LANE and Q_BLOCK == LANE and LANE % NSA_NSB == 0 and SLC_TOP_N <= NSA_NSB
assert NSA_KV_GROUPS * NSA_HEAD_DIM == LANE and CMP_BLOCK == 2 * CMP_STRIDE


def _bucket_value(tab_ref, h, rel):
    exact = REL_BUCKETS // 2
    n = jnp.maximum(rel, 0)
    large = exact + (jnp.log(jnp.maximum(n, 1).astype(_f32) / exact)
                     / math.log(REL_MAX_DIST / exact) * (REL_BUCKETS - exact)).astype(jnp.int32)
    bucket = jnp.where(n < exact, n, jnp.minimum(large, REL_BUCKETS - 1))
    val = jnp.full(rel.shape, tab_ref[0, h], _f32)
    for b in range(1, REL_BUCKETS):
        val = jnp.where(bucket == b, tab_ref[b, h], val)
    return val


def _nsa_bias_kernel(tab_ref, tb_ref, cb_ref):
    h = pl.program_id(0)
    kl = lax.broadcasted_iota(jnp.int32, (Q_BLOCK, LANE), 0)
    ql = lax.broadcasted_iota(jnp.int32, (Q_BLOCK, LANE), 1)
    for d in range(3):
        tb_ref[d, 0] = _bucket_value(tab_ref, h, ql - kl + d * Q_BLOCK)
    cmp_end = kl * CMP_STRIDE + (CMP_BLOCK - 1)
    for qb in range(NSA_NQB):
        cb_ref[qb, 0] = _bucket_value(tab_ref, h, qb * Q_BLOCK + ql - cmp_end)


def _nsa_bias_tiles(rel_bias):
    assert 2 * Q_BLOCK >= REL_MAX_DIST
    return pl.pallas_call(
        _nsa_bias_kernel,
        out_shape=(jax.ShapeDtypeStruct((3, NSA_HEADS, Q_BLOCK, LANE), _f32),
                   jax.ShapeDtypeStruct((NSA_NQB, NSA_HEADS, Q_BLOCK, LANE), _f32)),
        grid=(NSA_HEADS,),
        in_specs=[pl.BlockSpec(memory_space=pltpu.SMEM)],
        out_specs=(pl.BlockSpec((3, 1, Q_BLOCK, LANE), lambda h: (0, h, 0, 0)),
                   pl.BlockSpec((NSA_NQB, 1, Q_BLOCK, LANE), lambda h: (0, h, 0, 0))),
        compiler_params=pltpu.CompilerParams(dimension_semantics=("parallel",),
                                             vmem_limit_bytes=VMEM_LIMIT),
        name="nsa_bias_tiles",
    )(rel_bias)


def _nsa_cmp_kernel(tk_ref, tv_ref, pos_ref, w1_ref, w2_ref, kc_ref, vc_ref):
    rows = lax.broadcasted_iota(jnp.int32, (NSA_CMP_ROWS, LANE), 0)
    for idx, (t_ref, o_ref) in enumerate(((tk_ref, kc_ref), (tv_ref, vc_ref))):
        u = v = None
        for i in range(CMP_STRIDE):
            t_i = t_ref[pl.ds(i, NSA_CMP_ROWS, stride=CMP_STRIDE), :]
            ui = jnp.dot((t_i + pos_ref[idx, 0, i]).astype(_bf16), w1_ref[idx, 0, i], preferred_element_type=_f32)
            vi = jnp.dot((t_i + pos_ref[idx, 1, i]).astype(_bf16), w1_ref[idx, 1, i], preferred_element_type=_f32)
            u = ui if u is None else u + ui
            v = vi if v is None else v + vi
        pre = u + pltpu.roll(v, NSA_CMP_ROWS - 1, 0)
        act = pre * jax.nn.sigmoid(pre)
        out = jnp.dot(act.astype(_bf16), w2_ref[idx], preferred_element_type=_f32)
        o_ref[0] = jnp.where(rows < NSA_NC, out, 0.0).astype(_bf16)


def _nsa_compress(proj_b, pos_x, w1_x, w2_x):
    B = proj_b.shape[0] // SEQ
    ospec = pl.BlockSpec((1, NSA_CMP_ROWS, LANE), lambda b: (b, 0, 0))
    return pl.pallas_call(
        _nsa_cmp_kernel,
        out_shape=(jax.ShapeDtypeStruct((B, NSA_CMP_ROWS, LANE), _bf16),) * 2,
        grid=(B,),
        in_specs=[pl.BlockSpec((SEQ, LANE), lambda b: (b, PB['nkc'] // LANE)),
                  pl.BlockSpec((SEQ, LANE), lambda b: (b, PB['nvc'] // LANE)),
                  pl.BlockSpec(pos_x.shape, lambda b: (0,) * pos_x.ndim),
                  pl.BlockSpec(w1_x.shape, lambda b: (0,) * w1_x.ndim),
                  pl.BlockSpec((2, LANE, LANE), lambda b: (0, 0, 0))],
        out_specs=(ospec, ospec),
        compiler_params=pltpu.CompilerParams(dimension_semantics=("parallel",),
                                             vmem_limit_bytes=VMEM_LIMIT),
        name="nsa_compress",
    )(proj_b, proj_b, pos_x, w1_x, w2_x)


def _dot_nt(a, b):
    return lax.dot_general(a, b, (((1,), (1,)), ((), ())), preferred_element_type=_f32)


def _nsa_kernel(q_ref, ks_ref, vs_ref, kw_ref, vw_ref, kc_ref, vc_ref, gate_ref, tb_ref, cb_ref, ov_ref, e4_ref,
                o_ref, m_ref, l_ref, acc_ref, sel_ref):
    R, QB, hd = NSA_R, Q_BLOCK, NSA_HEAD_DIM
    qb = pl.program_id(1)
    kl = lax.broadcasted_iota(jnp.int32, (QB, LANE), 0)
    t_q = qb * QB + lax.broadcasted_iota(jnp.int32, (QB, LANE), 1)
    gates_t = jax.nn.sigmoid(gate_ref[...]).T
    heads = [None] * NSA_HEADS

    def split_r(x):
        return [x[:, r * LANE:(r + 1) * LANE] for r in range(R)]

    G = NSA_KV_GROUPS
    KB = 2 * QB
    qps = [jnp.concatenate([q_ref[:, (R * g + r) * LANE:(R * g + r + 1) * LANE] for r in range(R)], axis=0)
           for g in range(G)]

    def flash(k_ref, v_ref, p_lo, p_hi, selected):
        m_ref[...] = jnp.full(m_ref.shape, NEG_INF, _f32)
        l_ref[...] = jnp.zeros(l_ref.shape, _f32)
        acc_ref[...] = jnp.zeros(acc_ref.shape, _f32)
        krow = lax.broadcasted_iota(jnp.int32, (KB, LANE), 0)
        t_q2 = qb * QB + lax.broadcasted_iota(jnp.int32, (KB, LANE), 1)

        def body(i, carry):
            p = p_hi - 1 - i
            off = pl.multiple_of(p * KB, KB)
            k_blk = k_ref[pl.ds(off, KB), :]
            v_blk = v_ref[pl.ds(off, KB), :]
            rel = t_q2 - (p * KB + krow)
            base = (rel >= 0) if selected else ((rel >= 0) & (rel < WINDOW))
            tidx = [jnp.clip(qb - (KB // QB) * p - i, 0, 2) for i in range(KB // QB)]
            for g in range(G):
                mask = (base & (sel_ref[g, pl.ds(off, KB), :] > 0.5)) if selected else base
                s_t = split_r(_dot_nt(k_blk, qps[g]))
                bias = [jnp.concatenate([tb_ref[ti, R * g + r] for ti in tidx], axis=0) for r in range(R)]
                s_t = jnp.concatenate([jnp.where(mask, s_t[r] + bias[r], NEG_INF) for r in range(R)], axis=1)
                m_old = m_ref[g]
                m_new = jnp.maximum(m_old, jnp.max(s_t, axis=0, keepdims=True))
                e = jnp.exp(s_t - m_new)
                alpha = jnp.exp(m_old - m_new)
                l_ref[g] = alpha * l_ref[g] + jnp.sum(e, axis=0, keepdims=True)
                pv = _dot_tn(v_blk, e.astype(_bf16))
                acc_ref[g] = alpha * acc_ref[g] + pv[g * hd:(g + 1) * hd]
                m_ref[g] = m_new
            return carry

        lax.fori_loop(0, p_hi - p_lo, body, 0)
        return [acc_ref[g] / l_ref[g] for g in range(G)]

    o_cmp = []
    for g in range(G):
        qp = qps[g]
        s_t = split_r(_dot_nt(kc_ref[0], qp))
        mask_c = (t_q - (kl * CMP_STRIDE + CMP_BLOCK - 1) >= 0) & (kl < NSA_NC)
        s_t = jnp.concatenate([jnp.where(mask_c, s_t[r] + cb_ref[0, R * g + r], NEG_INF) for r in range(R)], axis=1)
        e = split_r(jnp.exp(s_t - jnp.max(s_t, axis=0, keepdims=True)))
        e = jnp.concatenate([jnp.where(mask_c, e[r], 0.0) for r in range(R)], axis=1)
        den = jnp.sum(e, axis=0, keepdims=True)
        p = (e / jnp.where(den > 0.0, den, 1.0)).astype(_bf16)
        o_cmp.append(_dot_tn(vc_ref[0], p)[g * hd:(g + 1) * hd])

        imp = sum(split_r(jnp.dot(ov_ref[...], p, preferred_element_type=_f32)))
        j = lax.broadcasted_iota(jnp.int32, (NSA_NSB, LANE), 0)
        cur = (qb * QB + lax.broadcasted_iota(jnp.int32, (NSA_NSB, LANE), 1)) // SLC_BLOCK
        imp = jnp.where((j == 0) | (j == cur) | (j == cur - 1), FORCE_SCORE, imp)
        imp = jnp.where(j <= cur, imp, NEG_INF)
        cnt = jnp.zeros((NSA_NSB, LANE), _f32)
        for jo in range(NSA_NSB):
            other = imp[jo:jo + 1, :]
            beats = (other > imp) | ((other == imp) & (j > jo))
            cnt = cnt + jnp.where(beats, 1.0, 0.0)
        sel = jnp.where(cnt < SLC_TOP_N, 1.0, 0.0)
        sel = jnp.concatenate([sel, jnp.zeros((LANE - NSA_NSB, LANE), _f32)], axis=0).astype(_bf16)
        sel_ref[g] = jnp.dot(e4_ref[...], sel, preferred_element_type=_f32)

    p_hi = qb // (KB // QB) + 1
    o_sel = flash(ks_ref, vs_ref, 0, p_hi, True)
    o_win = flash(kw_ref, vw_ref, jnp.maximum(qb - WINDOW // QB, 0) // (KB // QB), p_hi, False)

    for g in range(G):
        o_c, o_s, o_w = split_r(o_cmp[g]), split_r(o_sel[g]), split_r(o_win[g])
        for r in range(R):
            h = R * g + r
            heads[h] = (gates_t[3 * h:3 * h + 1] * o_c[r] + gates_t[3 * h + 1:3 * h + 2] * o_s[r]
                        + gates_t[3 * h + 2:3 * h + 3] * o_w[r])

    for pk in range(NSA_HEADS * hd // LANE):
        per = LANE // hd
        pair = jnp.concatenate(heads[per * pk:per * (pk + 1)], axis=0)
        o_ref[:, pk * LANE:(pk + 1) * LANE] = pair.T.astype(o_ref.dtype)


def _nsa_attention(proj_a, proj_b, kc, vc, tb, cb, ov4, e4):
    B = kc.shape[0]
    nqb = NSA_NQB
    qcols = NSA_HEADS * LANE
    kv = lambda blk: pl.BlockSpec((SEQ, LANE), lambda b, q: (b, qcols // LANE + blk))
    cspec = pl.BlockSpec((1, NSA_CMP_ROWS, LANE), lambda b, q: (b, 0, 0))
    return pl.pallas_call(
        _nsa_kernel,
        out_shape=jax.ShapeDtypeStruct((B * SEQ, NSA_HEADS * NSA_HEAD_DIM), _bf16),
        grid=(B, nqb),
        in_specs=[pl.BlockSpec((Q_BLOCK, qcols), lambda b, q: (b * nqb + q, 0)),
                  kv(0), kv(1), kv(2), kv(3), cspec, cspec,
                  pl.BlockSpec((Q_BLOCK, LANE), lambda b, q: (b * nqb + q, PB['misc'] // LANE)),
                  pl.BlockSpec((3, NSA_HEADS, Q_BLOCK, LANE), lambda b, q: (0, 0, 0, 0)),
                  pl.BlockSpec((1, NSA_HEADS, Q_BLOCK, LANE), lambda b, q: (q, 0, 0, 0)),
                  pl.BlockSpec((NSA_NSB, LANE), lambda b, q: (0, 0)),
                  pl.BlockSpec((SEQ, LANE), lambda b, q: (0, 0))],
        out_specs=pl.BlockSpec((Q_BLOCK, NSA_HEADS * NSA_HEAD_DIM), lambda b, q: (b * nqb + q, 0)),
        scratch_shapes=[pltpu.VMEM((NSA_KV_GROUPS, 1, NSA_R * Q_BLOCK), _f32),
                        pltpu.VMEM((NSA_KV_GROUPS, 1, NSA_R * Q_BLOCK), _f32),
                        pltpu.VMEM((NSA_KV_GROUPS, NSA_HEAD_DIM, NSA_R * Q_BLOCK), _f32),
                        pltpu.VMEM((NSA_KV_GROUPS, SEQ, Q_BLOCK), _f32)],
        compiler_params=pltpu.CompilerParams(dimension_semantics=("parallel", "arbitrary"),
                                             vmem_limit_bytes=VMEM_LIMIT),
        name="nsa_attention",
    )(proj_a, proj_a, proj_a, proj_a, proj_a, kc, vc, proj_b, tb, cb, ov4, e4)


def _nsa_constants():
    n = jnp.arange(NSA_CMP_ROWS)[None, :]
    jj = jnp.arange(NSA_NSB)[:, None]
    overlap_t = ((n * CMP_STRIDE <= jj * SLC_BLOCK + SLC_BLOCK - 1)
                 & (n * CMP_STRIDE + CMP_BLOCK - 1 >= jj * SLC_BLOCK) & (n < NSA_NC)).astype(_bf16)
    expand_t = (jnp.arange(SEQ)[:, None] // SLC_BLOCK == jnp.arange(LANE)[None, :]).astype(_bf16)
    return overlap_t, expand_t


def _nsa_weight_prep(w_in, cmp_pos, cmp_w1, cmp_w2):
    G, hd = NSA_KV_GROUPS, NSA_HEAD_DIM
    eye = jnp.eye(G, dtype=_f32)
    nq = NSA_HEADS * hd
    wq = w_in[:, :, :nq].reshape(DEPTH, D_MODEL, NSA_HEADS, 1, hd) * (hd ** -0.5)
    head_group = (jnp.arange(NSA_HEADS)[:, None] // NSA_R == jnp.arange(G)[None, :]).astype(_f32)
    wq = (wq * head_group[None, None, :, :, None]).reshape(DEPTH, D_MODEL, NSA_HEADS * LANE)
    slc_win = w_in[:, :, nq + 2 * NSA_KV:nq + 6 * NSA_KV]
    w_a = jnp.concatenate([wq, slc_win], axis=-1).astype(_bf16)
    w_b = jnp.concatenate([w_in[:, :, _IN_OFF[n]:_IN_OFF[n] + w] for n, w in _PB_SEGMENTS], axis=-1)
    w_b = jnp.pad(w_b, ((0, 0), (0, 0), (0, PROJ_B_COLS - PROJ_B_USED))).astype(_bf16)
    half = CMP_BLOCK // 2
    w1 = cmp_w1.reshape(DEPTH, 2, CMP_BLOCK, hd, hd)
    w1_x = jnp.einsum('lxide,gh->lxigdhe', w1, eye).reshape(DEPTH, 2, 2, half, G * hd, G * hd).astype(_bf16)
    w2_x = jnp.einsum('lxde,gh->lxgdhe', cmp_w2, eye).reshape(DEPTH, 2, G * hd, G * hd).astype(_bf16)
    pos_x = jnp.broadcast_to(cmp_pos[:, :, :, None, :], (DEPTH, 2, CMP_BLOCK, G, hd))
    pos_x = pos_x.reshape(DEPTH, 2, 2, half, 1, G * hd)
    return w_a, w_b, w1_x, w2_x, pos_x


REC_BLOCK = 2 * CHUNK
assert REC_BLOCK == LANE and SEQ % REC_BLOCK == 0


def _split3(x):
    hi = x.astype(_bf16)
    r = x - hi.astype(_f32)
    mid = r.astype(_bf16)
    lo = (r - mid.astype(_f32)).astype(_bf16)
    return hi, mid, lo


def _dot_exact_rhs(m, x):
    return sum(jnp.dot(m, p, preferred_element_type=_f32) for p in _split3(x))


def _dot_exact_lhs(x, m):
    return sum(jnp.dot(p, m, preferred_element_type=_f32) for p in _split3(x))


def _dot_tn(a, b):
    return lax.dot_general(a, b, (((0,), (0,)), ((), ())), preferred_element_type=_f32)


def _mm3(x, y):
    xh = x.astype(_bf16)
    xl = (x - xh.astype(_f32)).astype(_bf16)
    yh = y.astype(_bf16)
    yl = (y - yh.astype(_f32)).astype(_bf16)
    return (jnp.dot(xh, yh, preferred_element_type=_f32) + jnp.dot(xh, yl, preferred_element_type=_f32)
            + jnp.dot(xl, yh, preferred_element_type=_f32))


def _chunk_masks():
    ri = lax.broadcasted_iota(jnp.int32, (REC_BLOCK, REC_BLOCK), 0)
    ci = lax.broadcasted_iota(jnp.int32, (REC_BLOCK, REC_BLOCK), 1)
    same = (ri // CHUNK) == (ci // CHUNK)
    return ri, ci, (ci <= ri) & same, (ci < ri) & same, (ri <= ci) & same


def _as_mxu(mask):
    return jnp.where(mask, 1.0, 0.0).astype(_bf16)


def _softplus(x):
    return jnp.maximum(x, 0.0) + jnp.log1p(jnp.exp(-jnp.abs(x)))


def _silu(x):
    return x * jax.nn.sigmoid(x)


def _chunk_last(x):
    ri = lax.broadcasted_iota(jnp.int32, x.shape, 0)
    return jnp.where(ri < CHUNK, x[CHUNK - 1:CHUNK], x[2 * CHUNK - 1:2 * CHUNK])


def _expand_heads(x, lane0, nheads, width):
    per = LANE // width
    lane = lax.broadcasted_iota(jnp.int32, (x.shape[0], LANE), 1)
    pieces = []
    for p0 in range(lane0, lane0 + nheads, per):
        piece = jnp.broadcast_to(x[:, p0:p0 + 1], (x.shape[0], LANE))
        for k in range(1, per):
            piece = jnp.where(lane < k * width, piece, jnp.broadcast_to(x[:, p0 + k:p0 + k + 1], (x.shape[0], LANE)))
        pieces.append(piece)
    return jnp.concatenate(pieces, axis=1) if len(pieces) > 1 else pieces[0]


def _conv_silu(x, prev, w_ref, c0, bias=None):
    n, C = x.shape
    ntap = w_ref.shape[0]
    rows = lax.broadcasted_iota(jnp.int32, (n, C), 0)
    acc = x * w_ref[ntap - 1:ntap, c0:c0 + C]
    for s in range(1, ntap):
        xs = jnp.where(rows < s, pltpu.roll(prev, s, 0), pltpu.roll(x, s, 0))
        acc = acc + xs * w_ref[ntap - 1 - s:ntap - s, c0:c0 + C]
    if bias is not None:
        acc = acc + bias
    return _silu(acc)


def _rms(x, w):
    return x * lax.rsqrt(jnp.mean(x * x, axis=-1, keepdims=True) + EPS) * w


def _ssd_kernel(z_ref, xbc_ref, dt_ref, cw_ref, cb_ref, dtb_ref, alog_ref, dskip_ref, nw_ref, o_ref,
                prev_ref, state_ref):
    G, R, P, N = SSD_GROUPS, SSD_HEADS // SSD_GROUPS, SSD_HEAD_DIM, SSD_STATE

    @pl.when(pl.program_id(1) == 0)
    def _():
        prev_ref[...] = jnp.zeros(prev_ref.shape, _f32)
        state_ref[...] = jnp.zeros(state_ref.shape, _f32)

    x_in = xbc_ref[...]
    xc = _conv_silu(x_in, prev_ref[...], cw_ref, 0, cb_ref[...])
    prev_ref[...] = x_in
    _, _, tril, _, triu = _chunk_masks()
    lane = lax.broadcasted_iota(jnp.int32, (REC_BLOCK, LANE), 1)
    L0 = PBL['sdt']
    dt = _softplus(dt_ref[...] + dtb_ref[...])
    da = dt * (-jnp.exp(alog_ref[...]))
    a_cum = _dot_exact_rhs(_as_mxu(tril), da)
    a_cum_t = _dot_exact_lhs(da.T, _as_mxu(triu))
    a_last = _chunk_last(a_cum)
    xs = xc[:, :SSD_INNER]
    xdt = xs * _expand_heads(dt, L0, SSD_HEADS, P)
    xdtd = (xdt * _expand_heads(jnp.exp(a_last - a_cum), L0, SSD_HEADS, P)).astype(_bf16)
    xdt_b = xdt.astype(_bf16)
    ea = _expand_heads(jnp.exp(a_cum), L0, SSD_HEADS, P)
    y_groups = []
    for g in range(G):
        bm = xc[:, SSD_INNER + g * N:SSD_INNER + (g + 1) * N].astype(_bf16)
        cm = xc[:, SSD_INNER + (G + g) * N:SSD_INNER + (G + g + 1) * N].astype(_bf16)
        cbm = _dot_nt(cm, bm)
        intra = []
        for pr in range(R // 2):
            both = []
            for k in range(2):
                h = g * R + 2 * pr + k
                seg = jnp.exp(jnp.where(tril, a_cum[:, L0 + h:L0 + h + 1] - a_cum_t[L0 + h:L0 + h + 1, :], -jnp.inf))
                both.append(jnp.dot((cbm * seg).astype(_bf16), xdt_b[:, (h - k) * P:(h - k + 2) * P],
                                    preferred_element_type=_f32))
            intra.append(jnp.where(lane < P, both[0], both[1]))
        y_intra = jnp.concatenate(intra, axis=1)
        prev_rows = []
        for c in range(REC_BLOCK // CHUNK):
            rows = slice(c * CHUNK, (c + 1) * CHUNK)
            st = state_ref[g]
            prev_rows.append(jnp.dot(cm[rows], st.astype(_bf16), preferred_element_type=_f32))
            dec = _expand_heads(jnp.exp(a_cum[(c + 1) * CHUNK - 1:(c + 1) * CHUNK]), L0, SSD_HEADS, P)
            state_ref[g] = (st * dec[:, g * R * P:(g + 1) * R * P]
                            + _dot_tn(bm[rows], xdtd[rows, g * R * P:(g + 1) * R * P]))
        y_groups.append(y_intra + jnp.concatenate(prev_rows, axis=0) * ea[:, g * R * P:(g + 1) * R * P])
    y = jnp.concatenate(y_groups, axis=1) + xs * dskip_ref[...]
    y = y * _silu(z_ref[...])
    gw = SSD_INNER // G
    for g in range(G):
        o_ref[:, g * gw:(g + 1) * gw] = _rms(y[:, g * gw:(g + 1) * gw], nw_ref[:, g * gw:(g + 1) * gw]).astype(o_ref.dtype)


def _gdn_kernel(q_ref, k_ref, v_ref, z_ref, ba_ref, cw_ref, dtb_ref, alog_ref, nw_ref, o_ref,
                pq_ref, pk_ref, pv_ref, state_ref):
    H, Dh = GDN_HEADS, GDN_HEAD_DIM

    @pl.when(pl.program_id(1) == 0)
    def _():
        for r in (pq_ref, pk_ref, pv_ref, state_ref):
            r[...] = jnp.zeros(r.shape, _f32)

    conv = []
    for i, (x_ref, p_ref) in enumerate(((q_ref, pq_ref), (k_ref, pk_ref), (v_ref, pv_ref))):
        x_in = x_ref[...]
        conv.append(_conv_silu(x_in, p_ref[...], cw_ref, i * GDN_WIDTH))
        p_ref[...] = x_in
    q, k, v = conv
    ri, ci, tril, strict, triu = _chunk_masks()
    ba = ba_ref[...]
    beta = jax.nn.sigmoid(ba)
    gl = -jnp.exp(alog_ref[...]) * _softplus(ba + dtb_ref[...])
    gcum = _dot_exact_rhs(_as_mxu(tril), gl)
    gcum_t = _dot_exact_lhs(gl.T, _as_mxu(triu))
    glast = _chunk_last(gcum)
    zero_rows = jnp.zeros((CHUNK, Dh), _f32)
    eye = jnp.where(tril & jnp.logical_not(strict), 1.0, 0.0)
    hs = range(H)
    sls = [slice(h * Dh, (h + 1) * Dh) for h in hs]
    qq = [q[:, sl] * lax.rsqrt(jnp.sum(q[:, sl] * q[:, sl], axis=-1, keepdims=True) + EPS) * (Dh ** -0.5) for sl in sls]
    kk = [k[:, sl] * lax.rsqrt(jnp.sum(k[:, sl] * k[:, sl], axis=-1, keepdims=True) + EPS) for sl in sls]
    lb = [PBL['gbeta'] + h for h in hs]
    lg = [PBL['gbeta'] + H + h for h in hs]
    bcol = [beta[:, lb[h]:lb[h] + 1] for h in hs]
    gcol = [gcum[:, lg[h]:lg[h] + 1] for h in hs]
    decay = [jnp.exp(jnp.where(tril, gcol[h] - gcum_t[lg[h]:lg[h] + 1, :], -jnp.inf)) for h in hs]
    kb = [kk[h] * bcol[h] for h in hs]
    s = [_dot_nt(jnp.concatenate([kb[h], qq[h]], axis=0).astype(_bf16), kk[h].astype(_bf16)) for h in hs]
    a_mat = [jnp.where(strict, s[h][:REC_BLOCK] * decay[h], 0.0) for h in hs]
    aqk = [(s[h][REC_BLOCK:] * decay[h]).astype(_bf16) for h in hs]
    SUB = 8
    same = lambda n: (ri // n) == (ci // n)
    a_sub = [jnp.where(same(SUB), a_mat[h], 0.0) for h in hs]
    tinv = [eye - a_sub[h] for h in hs]
    pw = a_sub
    for _ in range(SUB.bit_length() - 2):
        pw = [_mm3(pw[h], pw[h]) for h in hs]
        tinv = [tinv[h] + _mm3(tinv[h], pw[h]) for h in hs]
    n = SUB
    while n < CHUNK:
        enclosed = same(2 * n) & jnp.logical_not(same(n))
        tc = [_mm3(tinv[h], jnp.where(enclosed, a_mat[h], 0.0)) for h in hs]
        tinv = [tinv[h] - _mm3(tc[h], tinv[h]) for h in hs]
        n *= 2
    sol = [_mm3(tinv[h], jnp.concatenate([v[:, sls[h]] * bcol[h], kb[h] * jnp.exp(gcol[h])], axis=1)) for h in hs]
    u = [sol[h][:, :Dh] for h in hs]
    w = [sol[h][:, Dh:].astype(_bf16) for h in hs]
    q_dec = [(qq[h] * jnp.exp(gcol[h])).astype(_bf16) for h in hs]
    k_end = [(kk[h] * jnp.exp(glast[:, lg[h]:lg[h] + 1] - gcol[h])).astype(_bf16) for h in hs]
    o_rows = [[] for _ in hs]
    for c in range(REC_BLOCK // CHUNK):
        rows = slice(c * CHUNK, (c + 1) * CHUNK)
        st = [state_ref[h] for h in hs]
        st_b = [st[h].astype(_bf16) for h in hs]
        v_new = [u[h][rows] - jnp.dot(w[h][rows], st_b[h], preferred_element_type=_f32) for h in hs]
        v_full = [jnp.concatenate([v_new[h], zero_rows] if c == 0 else [zero_rows, v_new[h]], axis=0).astype(_bf16)
                  for h in hs]
        for h in hs:
            o_rows[h].append(jnp.dot(q_dec[h][rows], st_b[h], preferred_element_type=_f32)
                             + jnp.dot(aqk[h][rows], v_full[h], preferred_element_type=_f32))
            d_last = jnp.exp(gcum[(c + 1) * CHUNK - 1:(c + 1) * CHUNK, lg[h]:lg[h] + 1])
            state_ref[h] = st[h] * d_last + _dot_tn(k_end[h][rows], v_new[h].astype(_bf16))
    for h in hs:
        o = _rms(jnp.concatenate(o_rows[h], axis=0), nw_ref[...]) * _silu(z_ref[:, sls[h]])
        o_ref[:, sls[h]] = o.astype(o_ref.dtype)


def _gla_kernel(q_ref, k_ref, v_ref, go_ref, lr_ref, w2_ref, gb_ref, nw_ref, o_ref, state_ref):
    H, Dk, Dv = GLA_HEADS, GLA_DK, GLA_DV

    @pl.when(pl.program_id(1) == 0)
    def _():
        state_ref[...] = jnp.zeros(state_ref.shape, _f32)

    _, _, tril, _, _ = _chunk_masks()
    lane = lax.broadcasted_iota(jnp.int32, (REC_BLOCK, LANE), 1)
    pre = jnp.dot(lr_ref[...].astype(_bf16), w2_ref[...], preferred_element_type=_f32) + gb_ref[...]
    gk = (jnp.minimum(pre, 0.0) - jnp.log1p(jnp.exp(-jnp.abs(pre)))) / GLA_GATE_NORM
    bcum = _dot_exact_rhs(_as_mxu(tril), gk)
    blast = _chunk_last(bcum)
    q_dec = q_ref[...] * (Dk ** -0.5) * jnp.exp(bcum)
    k_inv = (k_ref[...] * jnp.exp(-bcum)).astype(_bf16)
    k_end = (k_ref[...] * jnp.exp(blast - bcum)).astype(_bf16)
    per = LANE // Dk
    for pr in range(H // per):
        psl = slice(pr * LANE, (pr + 1) * LANE)
        qd, ki, ke = q_dec[:, psl], k_inv[:, psl], k_end[:, psl]
        qm, vh, o_intra = [], [], []
        for k in range(per):
            h = pr * per + k
            qm.append(jnp.where(lane // Dk == k, qd, 0.0).astype(_bf16))
            vh.append(v_ref[:, h * Dv:(h + 1) * Dv].astype(_bf16))
            attn = jnp.where(tril, _dot_nt(qm[k], ki), 0.0).astype(_bf16)
            o_intra.append(jnp.dot(attn, vh[k], preferred_element_type=_f32))
        o_prev = [[] for _ in range(per)]
        for c in range(REC_BLOCK // CHUNK):
            rows = slice(c * CHUNK, (c + 1) * CHUNK)
            st = state_ref[pr]
            st_b = st.astype(_bf16)
            loc = None
            for k in range(per):
                o_prev[k].append(_dot_nt(qm[k][rows], st_b))
                lk = _dot_tn(vh[k][rows], ke[rows])
                loc = lk if loc is None else jnp.where(lane < k * Dk, loc, lk)
            state_ref[pr] = st * jnp.exp(bcum[(c + 1) * CHUNK - 1:(c + 1) * CHUNK, psl]) + loc
        for k in range(per):
            h = pr * per + k
            o = o_intra[k] + jnp.concatenate(o_prev[k], axis=0)
            o = _rms(o, nw_ref[...]) * _silu(go_ref[:, h * Dv:(h + 1) * Dv])
            o_ref[:, h * Dv:(h + 1) * Dv] = o.astype(o_ref.dtype)


def _rec_call(body, proj_b, col_blocks, params, out_cols, scratch, name):
    B = proj_b.shape[0] // SEQ
    nblk = SEQ // REC_BLOCK
    in_specs = [pl.BlockSpec((REC_BLOCK, w), (lambda b, t, c=c0 // w: (b * nblk + t, c))) for c0, w in col_blocks]
    for p in params:
        in_specs.append(pl.BlockSpec(p.shape, lambda b, t, nd=p.ndim: (0,) * nd))
    return pl.pallas_call(
        body,
        out_shape=jax.ShapeDtypeStruct((B * SEQ, out_cols), _bf16),
        grid=(B, nblk),
        in_specs=in_specs,
        out_specs=pl.BlockSpec((REC_BLOCK, out_cols), lambda b, t: (b * nblk + t, 0)),
        scratch_shapes=scratch,
        compiler_params=pltpu.CompilerParams(dimension_semantics=("parallel", "arbitrary"),
                                             vmem_limit_bytes=VMEM_LIMIT),
        name=name,
    )(*([proj_b] * len(col_blocks)), *params)


def _lane_pad(v, lane0=0):
    return jnp.pad(v.astype(_f32), (lane0, LANE - lane0 - v.shape[0]))[None]


def _ssd_call(proj_b, conv_w, conv_b, dt_bias, a_log, d_skip, norm_w):
    cols = [(PB['sz'], SSD_INNER), (PB['sxbc'], SSD_XBC), (PB['misc'], LANE)]
    params = [conv_w, conv_b[None], _lane_pad(dt_bias, PBL['sdt']), _lane_pad(a_log, PBL['sdt']),
              jnp.repeat(d_skip, SSD_HEAD_DIM)[None], norm_w[None]]
    scratch = [pltpu.VMEM((REC_BLOCK, SSD_XBC), _f32),
               pltpu.VMEM((SSD_GROUPS, SSD_STATE, SSD_INNER // SSD_GROUPS), _f32)]
    return _rec_call(_ssd_kernel, proj_b, cols, params, SSD_INNER, scratch, "ssd")


def _gdn_call(proj_b, conv_w, dt_bias, a_log, norm_w):
    W = GDN_WIDTH
    cols = [(PB['gq'], W), (PB['gk'], W), (PB['gv'], W), (PB['gz'], W), (PB['misc'], LANE)]
    decay_lane = PBL['gbeta'] + GDN_HEADS
    params = [conv_w, _lane_pad(dt_bias, decay_lane), _lane_pad(a_log, decay_lane), norm_w[None]]
    scratch = [pltpu.VMEM((REC_BLOCK, W), _f32)] * 3 + [pltpu.VMEM((GDN_HEADS, GDN_HEAD_DIM, GDN_HEAD_DIM), _f32)]
    return _rec_call(_gdn_kernel, proj_b, cols, params, W, scratch, "gdn")


def _gla_call(proj_b, gate_w2, gate_b, norm_w):
    cols = [(PB['lq'], GLA_KEY), (PB['lk'], GLA_KEY), (PB['lv'], GLA_VAL), (PB['lg'], GLA_VAL), (PB['misc'], LANE)]
    w2 = jnp.pad(gate_w2, ((PBL['llr'], LANE - PBL['llr'] - GLA_GATE_RANK), (0, 0))).astype(_bf16)
    params = [w2, gate_b[None], norm_w[None]]
    scratch = [pltpu.VMEM((GLA_HEADS * GLA_DK // LANE, GLA_DV, LANE), _f32)]
    return _rec_call(_gla_kernel, proj_b, cols, params, GLA_VAL, scratch, "gla")


def kernel(x, c, rel_bias, norm1_w, norm2_w, ada_w, ada_b, w_in, w_out, nsa_cmp_pos, nsa_cmp_w1, nsa_cmp_w2, ssd_conv_w, ssd_conv_b, ssd_dt_bias, ssd_a_log, ssd_d, ssd_norm_w, gdn_conv_w, gdn_dt_bias, gdn_a_log, gdn_norm_w, gla_gate_w2, gla_gate_b, gla_norm_w, mlp_w1, mlp_w2, final_norm_w):
    B, S, D = x.shape
    mod = _ada_all(c, ada_w, ada_b).reshape(DEPTH, B, 6, 1, D)
    w_a, w_b, cmp_w1_x, cmp_w2_x, cmp_pos_x = _nsa_weight_prep(w_in, nsa_cmp_pos, nsa_cmp_w1, nsa_cmp_w2)
    w_out_b = w_out.astype(_bf16)
    w1_b = mlp_w1.astype(_bf16)
    w2_b = mlp_w2.astype(_bf16)
    tb, cb = _nsa_bias_tiles(rel_bias)
    ov4, e4 = _nsa_constants()
    xf = x.reshape(TOKENS, D)
    no_mod = jnp.zeros((B, 1, D), _f32)
    h = _norm_mod(xf, norm1_w[0][None], mod[0, :, 1], mod[0, :, 0])
    for l in range(DEPTH):
        sh1, sc1, g1, sh2, sc2, g2 = (mod[l, :, i] for i in range(6))
        proj_a = _matmul(h, w_a[l], out_dtype=_bf16)
        proj_b = _matmul(h, w_b[l], tn=PROJ_B_TN)
        kc, vc = _nsa_compress(proj_b, cmp_pos_x[l], cmp_w1_x[l], cmp_w2_x[l])
        y_nsa = _nsa_attention(proj_a, proj_b, kc, vc, tb, cb, ov4, e4)
        y_ssd = _ssd_call(proj_b, ssd_conv_w[l], ssd_conv_b[l], ssd_dt_bias[l], ssd_a_log[l], ssd_d[l], ssd_norm_w[l])
        y_gdn = _gdn_call(proj_b, gdn_conv_w[l], gdn_dt_bias[l], gdn_a_log[l], gdn_norm_w[l])
        y_gla = _gla_call(proj_b, gla_gate_w2[l], gla_gate_b[l], gla_norm_w[l])
        xf, h2 = _out_proj((y_nsa, y_ssd, y_gdn, y_gla), w_out_b[l], xf, g1,
                           norm2_w[l][None], sc2, sh2)
        if l + 1 < DEPTH:
            xf, h = _mlp(h2, w1_b[l], w2_b[l], xf, g2, norm1_w[l + 1][None], mod[l + 1, :, 1], mod[l + 1, :, 0], _bf16)
        else:
            _, out = _mlp(h2, w1_b[l], w2_b[l], xf, g2, final_norm_w[None], no_mod, no_mod, _f32)
    return out.reshape(B, S, D)
```

```python
import math
from functools import partial

import jax
import jax.numpy as jnp
from jax import lax
from jax.experimental import pallas as pl
from jax.experimental.pallas import tpu as pltpu

D_MODEL = 2048
BATCH = 16
SEQ = 2048
DEPTH = 4

MIX_GROUP = D_MODEL // 4
NSA_HEAD_DIM = 64
NSA_HEADS = MIX_GROUP // NSA_HEAD_DIM
NSA_KV_GROUPS = max(1, NSA_HEADS // 4)
NSA_KV = NSA_KV_GROUPS * NSA_HEAD_DIM
CMP_BLOCK = 32
CMP_STRIDE = 16
SLC_BLOCK = 64
SLC_TOP_N = 8
WINDOW = 512
Q_BLOCK = 128
REL_BUCKETS = 32
REL_MAX_DIST = 128
SSD_HEAD_DIM = 64
SSD_HEADS = MIX_GROUP // SSD_HEAD_DIM
SSD_INNER = SSD_HEADS * SSD_HEAD_DIM
SSD_GROUPS = 2
SSD_STATE = 128
SSD_CONV = 4
SSD_XBC = SSD_INNER + 2 * SSD_GROUPS * SSD_STATE
GDN_HEAD_DIM = 128
GDN_HEADS = MIX_GROUP // GDN_HEAD_DIM
GDN_WIDTH = GDN_HEADS * GDN_HEAD_DIM
GDN_CONV = 4
GLA_DV = 128
GLA_HEADS = MIX_GROUP // GLA_DV
GLA_DK = GLA_DV // 2
GLA_KEY = GLA_HEADS * GLA_DK
GLA_VAL = GLA_HEADS * GLA_DV
GLA_GATE_RANK = 16
GLA_GATE_NORM = 16.0
CHUNK = 64
MLP_HIDDEN = 4 * D_MODEL
EPS = 1e-6
NEG_INF = -1e30
FORCE_SCORE = 1e9
IN_SPLITS = (NSA_HEADS * NSA_HEAD_DIM, NSA_KV, NSA_KV, NSA_KV, NSA_KV, NSA_KV, NSA_KV, NSA_HEADS * 3,
             SSD_INNER, SSD_XBC, SSD_HEADS,
             GDN_WIDTH, GDN_WIDTH, GDN_WIDTH, GDN_WIDTH, GDN_HEADS, GDN_HEADS,
             GLA_KEY, GLA_KEY, GLA_VAL, GLA_VAL, GLA_GATE_RANK)
IN_COLS = sum(IN_SPLITS)
MIX_OUT = NSA_HEADS * NSA_HEAD_DIM + SSD_INNER + GDN_WIDTH + GLA_VAL

LANE = 128
VMEM_LIMIT = 56 * 1024 * 1024
TOKENS = BATCH * SEQ
_IN_NAMES = ('nq', 'nkc', 'nvc', 'nks', 'nvs', 'nkw', 'nvw', 'ngate', 'sz', 'sxbc', 'sdt',
             'gq', 'gk', 'gv', 'gz', 'gbeta', 'ga', 'lq', 'lk', 'lv', 'lg', 'llr')
_IN_W = dict(zip(_IN_NAMES, IN_SPLITS))
_IN_OFF = {n: sum(IN_SPLITS[:i]) for i, n in enumerate(_IN_NAMES)}
_PB_SEGMENTS = (('sxbc', SSD_XBC), ('gq', GDN_WIDTH), ('gk', GDN_WIDTH), ('gv', GDN_WIDTH), ('gz', GDN_WIDTH),
                ('sz', SSD_INNER), ('lv', GLA_VAL), ('lg', GLA_VAL), ('lq', GLA_KEY), ('lk', GLA_KEY),
                ('nkc', NSA_KV), ('nvc', NSA_KV),
                ('ngate', NSA_HEADS * 3), ('sdt', SSD_HEADS), ('gbeta', 2 * GDN_HEADS), ('llr', GLA_GATE_RANK))
_PB_MISC = ('ngate', 'sdt', 'gbeta', 'llr')
PB, PBL = {}, {}
PROJ_B_USED = 0
for _n, _w in _PB_SEGMENTS:
    if _n in _PB_MISC:
        PB.setdefault('misc', PROJ_B_USED - PROJ_B_USED % LANE)
        PBL[_n] = PROJ_B_USED - PB['misc']
    else:
        assert PROJ_B_USED % _w == 0 and _w % LANE == 0
        PB[_n] = PROJ_B_USED
    PROJ_B_USED += _w
assert PROJ_B_USED - PB['misc'] <= LANE and PBL['ngate'] == 0 and _IN_OFF['ga'] == _IN_OFF['gbeta'] + GDN_HEADS
PROJ_B_TN = 512
PROJ_B_COLS = -(-PROJ_B_USED // PROJ_B_TN) * PROJ_B_TN

_bf16 = jnp.bfloat16
_f32 = jnp.float32


def _ada_kernel(c_ref, w_ref, b_ref, o_ref):
    c = c_ref[...]
    c_act = c * jax.nn.sigmoid(c)
    o_ref[0] = jnp.dot(c_act, w_ref[0], preferred_element_type=_f32) + b_ref[0]


def _ada_all(c, ada_w, ada_b):
    tn = 1024
    return pl.pallas_call(
        _ada_kernel,
        out_shape=jax.ShapeDtypeStruct((DEPTH, BATCH, 6 * D_MODEL), _f32),
        grid=(DEPTH, 6 * D_MODEL // tn),
        in_specs=[pl.BlockSpec((BATCH, D_MODEL), lambda l, j: (0, 0)),
                  pl.BlockSpec((1, D_MODEL, tn), lambda l, j: (l, 0, j)),
                  pl.BlockSpec((1, 1, tn), lambda l, j: (l, 0, j))],
        out_specs=pl.BlockSpec((1, BATCH, tn), lambda l, j: (l, 0, j)),
        compiler_params=pltpu.CompilerParams(dimension_semantics=("parallel", "parallel"),
                                             vmem_limit_bytes=VMEM_LIMIT),
        name="ada_mod",
    )(c, ada_w, ada_b.reshape(DEPTH, 1, 6 * D_MODEL))


def _rms_mod(x, w, sc, sh):
    y = x * lax.rsqrt(jnp.mean(x * x, axis=-1, keepdims=True) + EPS)
    return (y * w) * (1.0 + sc) + sh


def _norm_mod_kernel(x_ref, w_ref, sc_ref, sh_ref, o_ref):
    o_ref[...] = _rms_mod(x_ref[...], w_ref[...], sc_ref[0], sh_ref[0]).astype(o_ref.dtype)


def _norm_mod(x, w, sc, sh, tm=512):
    per_b = SEQ // tm
    return pl.pallas_call(
        _norm_mod_kernel,
        out_shape=jax.ShapeDtypeStruct((TOKENS, D_MODEL), _bf16),
        grid=(TOKENS // tm,),
        in_specs=[pl.BlockSpec((tm, D_MODEL), lambda i: (i, 0)),
                  pl.BlockSpec((1, D_MODEL), lambda i: (0, 0)),
                  pl.BlockSpec((1, 1, D_MODEL), lambda i: (i // per_b, 0, 0)),
                  pl.BlockSpec((1, 1, D_MODEL), lambda i: (i // per_b, 0, 0))],
        out_specs=pl.BlockSpec((tm, D_MODEL), lambda i: (i, 0)),
        compiler_params=pltpu.CompilerParams(dimension_semantics=("parallel",),
                                             vmem_limit_bytes=VMEM_LIMIT),
        name="norm_mod",
    )(x, w, sc, sh)


def _matmul_kernel(a_ref, w_ref, o_ref):
    o_ref[...] = jnp.dot(a_ref[...], w_ref[...], preferred_element_type=_f32).astype(o_ref.dtype)


def _matmul(a, w, tm=1024, tn=512, out_dtype=_f32):
    M, K = a.shape
    N = w.shape[1]
    return pl.pallas_call(
        _matmul_kernel,
        out_shape=jax.ShapeDtypeStruct((M, N), out_dtype),
        grid=(M // tm, N // tn),
        in_specs=[pl.BlockSpec((tm, K), lambda i, j: (i, 0)),
                  pl.BlockSpec((K, tn), lambda i, j: (0, j))],
        out_specs=pl.BlockSpec((tm, tn), lambda i, j: (i, j)),
        compiler_params=pltpu.CompilerParams(dimension_semantics=("parallel", "parallel"),
                                             vmem_limit_bytes=VMEM_LIMIT),
        name="in_proj",
    )(a, w)


def _out_proj_kernel(a0_ref, a1_ref, a2_ref, a3_ref, w_ref, x_ref, g_ref, nw_ref, sc_ref, sh_ref, xo_ref, ho_ref):
    y = None
    for i, a_ref in enumerate((a0_ref, a1_ref, a2_ref, a3_ref)):
        part = jnp.dot(a_ref[...], w_ref[i * MIX_GROUP:(i + 1) * MIX_GROUP, :], preferred_element_type=_f32)
        y = part if y is None else y + part
    xn = x_ref[...] + g_ref[0] * y
    xo_ref[...] = xn
    ho_ref[...] = _rms_mod(xn, nw_ref[...], sc_ref[0], sh_ref[0]).astype(ho_ref.dtype)


def _out_proj(mixed, w, x, g, nw, sc, sh, tm=512):
    per_b = SEQ // tm
    bspec = pl.BlockSpec((1, 1, D_MODEL), lambda i: (i // per_b, 0, 0))
    aspec = pl.BlockSpec((tm, MIX_GROUP), lambda i: (i, 0))
    return pl.pallas_call(
        _out_proj_kernel,
        out_shape=(jax.ShapeDtypeStruct((TOKENS, D_MODEL), _f32),
                   jax.ShapeDtypeStruct((TOKENS, D_MODEL), _bf16)),
        grid=(TOKENS // tm,),
        in_specs=[aspec, aspec, aspec, aspec,
                  pl.BlockSpec((MIX_OUT, D_MODEL), lambda i: (0, 0)),
                  pl.BlockSpec((tm, D_MODEL), lambda i: (i, 0)),
                  bspec,
                  pl.BlockSpec((1, D_MODEL), lambda i: (0, 0)),
                  bspec, bspec],
        out_specs=(pl.BlockSpec((tm, D_MODEL), lambda i: (i, 0)),
                   pl.BlockSpec((tm, D_MODEL), lambda i: (i, 0))),
        compiler_params=pltpu.CompilerParams(dimension_semantics=("parallel",),
                                             vmem_limit_bytes=VMEM_LIMIT),
        name="out_proj",
    )(*mixed, w, x, g, nw, sc, sh)


def _mlp_kernel(h_ref, w1_ref, w2_ref, x_ref, g_ref, nw_ref, sc_ref, sh_ref, o_ref, hn_ref, acc_ref):
    j = pl.program_id(1)

    @pl.when(j == 0)
    def _():
        acc_ref[...] = jnp.zeros_like(acc_ref)

    u = jnp.dot(h_ref[...], w1_ref[...], preferred_element_type=_f32)
    u = jnp.square(jnp.maximum(u, 0.0)).astype(_bf16)
    acc_ref[...] += jnp.dot(u, w2_ref[...], preferred_element_type=_f32)

    @pl.when(j == pl.num_programs(1) - 1)
    def _():
        xn = x_ref[...] + g_ref[0] * acc_ref[...]
        o_ref[...] = xn
        hn_ref[...] = _rms_mod(xn, nw_ref[...], sc_ref[0], sh_ref[0]).astype(hn_ref.dtype)


def _mlp(h, w1, w2, x, g, nw, sc, sh, next_dtype, tm=512, th=1024):
    per_b = SEQ // tm
    bspec = pl.BlockSpec((1, 1, D_MODEL), lambda i, j: (i // per_b, 0, 0))
    xspec = pl.BlockSpec((tm, D_MODEL), lambda i, j: (i, 0))
    return pl.pallas_call(
        _mlp_kernel,
        out_shape=(jax.ShapeDtypeStruct((TOKENS, D_MODEL), _f32),
                   jax.ShapeDtypeStruct((TOKENS, D_MODEL), next_dtype)),
        grid=(TOKENS // tm, MLP_HIDDEN // th),
        in_specs=[xspec,
                  pl.BlockSpec((D_MODEL, th), lambda i, j: (0, j)),
                  pl.BlockSpec((th, D_MODEL), lambda i, j: (j, 0)),
                  xspec, bspec,
                  pl.BlockSpec((1, D_MODEL), lambda i, j: (0, 0)),
                  bspec, bspec],
        out_specs=(xspec, xspec),
        scratch_shapes=[pltpu.VMEM((tm, D_MODEL), _f32)],
        compiler_params=pltpu.CompilerParams(dimension_semantics=("parallel", "arbitrary"),
                                             vmem_limit_bytes=VMEM_LIMIT),
        name="mlp",
    )(h, w1, w2, x, g, nw, sc, sh)


NSA_R = NSA_HEADS // NSA_KV_GROUPS
NSA_CMP_ROWS = SEQ // CMP_STRIDE
NSA_NC = NSA_CMP_ROWS - CMP_BLOCK // CMP_STRIDE + 1
NSA_NSB = SEQ // SLC_BLOCK
NSA_NQB = SEQ // Q_BLOCK
assert NSA_CMP_ROWS == LANE and Q_BLOCK == LANE and LANE % NSA_NSB == 0 and SLC_TOP_N <= NSA_NSB
assert NSA_KV_GROUPS * NSA_HEAD_DIM == LANE and CMP_BLOCK == 2 * CMP_STRIDE


def _bucket_value(tab_ref, h, rel):
    exact = REL_BUCKETS // 2
    n = jnp.maximum(rel, 0)
    large = exact + (jnp.log(jnp.maximum(n, 1).astype(_f32) / exact)
                     / math.log(REL_MAX_DIST / exact) * (REL_BUCKETS - exact)).astype(jnp.int32)
    bucket = jnp.where(n < exact, n, jnp.minimum(large, REL_BUCKETS - 1))
    val = jnp.full(rel.shape, tab_ref[0, h], _f32)
    for b in range(1, REL_BUCKETS):
        val = jnp.where(bucket == b, tab_ref[b, h], val)
    return val


def _nsa_bias_kernel(tab_ref, tb_ref, cb_ref):
    h = pl.program_id(0)
    kl = lax.broadcasted_iota(jnp.int32, (Q_BLOCK, LANE), 0)
    ql = lax.broadcasted_iota(jnp.int32, (Q_BLOCK, LANE), 1)
    for d in range(3):
        tb_ref[d, 0] = _bucket_value(tab_ref, h, ql - kl + d * Q_BLOCK)
    cmp_end = kl * CMP_STRIDE + (CMP_BLOCK - 1)
    for qb in range(NSA_NQB):
        cb_ref[qb, 0] = _bucket_value(tab_ref, h, qb * Q_BLOCK + ql - cmp_end)


def _nsa_bias_tiles(rel_bias):
    assert 2 * Q_BLOCK >= REL_MAX_DIST
    return pl.pallas_call(
        _nsa_bias_kernel,
        out_shape=(jax.ShapeDtypeStruct((3, NSA_HEADS, Q_BLOCK, LANE), _f32),
                   jax.ShapeDtypeStruct((NSA_NQB, NSA_HEADS, Q_BLOCK, LANE), _f32)),
        grid=(NSA_HEADS,),
        in_specs=[pl.BlockSpec(memory_space=pltpu.SMEM)],
        out_specs=(pl.BlockSpec((3, 1, Q_BLOCK, LANE), lambda h: (0, h, 0, 0)),
                   pl.BlockSpec((NSA_NQB, 1, Q_BLOCK, LANE), lambda h: (0, h, 0, 0))),
        compiler_params=pltpu.CompilerParams(dimension_semantics=("parallel",),
                                             vmem_limit_bytes=VMEM_LIMIT),
        name="nsa_bias_tiles",
    )(rel_bias)


def _nsa_cmp_kernel(tk_ref, tv_ref, pos_ref, w1_ref, w2_ref, kc_ref, vc_ref):
    rows = lax.broadcasted_iota(jnp.int32, (NSA_CMP_ROWS, LANE), 0)
    for idx, (t_ref, o_ref) in enumerate(((tk_ref, kc_ref), (tv_ref, vc_ref))):
        u = v = None
        for i in range(CMP_STRIDE):
            t_i = t_ref[pl.ds(i, NSA_CMP_ROWS, stride=CMP_STRIDE), :]
            ui = jnp.dot((t_i + pos_ref[idx, 0, i]).astype(_bf16), w1_ref[idx, 0, i], preferred_element_type=_f32)
            vi = jnp.dot((t_i + pos_ref[idx, 1, i]).astype(_bf16), w1_ref[idx, 1, i], preferred_element_type=_f32)
            u = ui if u is None else u + ui
            v = vi if v is None else v + vi
        pre = u + pltpu.roll(v, NSA_CMP_ROWS - 1, 0)
        act = pre * jax.nn.sigmoid(pre)
        out = jnp.dot(act.astype(_bf16), w2_ref[idx], preferred_element_type=_f32)
        o_ref[0] = jnp.where(rows < NSA_NC, out, 0.0).astype(_bf16)


def _nsa_compress(proj_b, pos_x, w1_x, w2_x):
    B = proj_b.shape[0] // SEQ
    ospec = pl.BlockSpec((1, NSA_CMP_ROWS, LANE), lambda b: (b, 0, 0))
    return pl.pallas_call(
        _nsa_cmp_kernel,
        out_shape=(jax.ShapeDtypeStruct((B, NSA_CMP_ROWS, LANE), _bf16),) * 2,
        grid=(B,),
        in_specs=[pl.BlockSpec((SEQ, LANE), lambda b: (b, PB['nkc'] // LANE)),
                  pl.BlockSpec((SEQ, LANE), lambda b: (b, PB['nvc'] // LANE)),
                  pl.BlockSpec(pos_x.shape, lambda b: (0,) * pos_x.ndim),
                  pl.BlockSpec(w1_x.shape, lambda b: (0,) * w1_x.ndim),
                  pl.BlockSpec((2, LANE, LANE), lambda b: (0, 0, 0))],
        out_specs=(ospec, ospec),
        compiler_params=pltpu.CompilerParams(dimension_semantics=("parallel",),
                                             vmem_limit_bytes=VMEM_LIMIT),
        name="nsa_compress",
    )(proj_b, proj_b, pos_x, w1_x, w2_x)


def _dot_nt(a, b):
    return lax.dot_general(a, b, (((1,), (1,)), ((), ())), preferred_element_type=_f32)


def _nsa_kernel(q_ref, ks_ref, vs_ref, kw_ref, vw_ref, kc_ref, vc_ref, gate_ref, tb_ref, cb_ref, ov_ref,
                o_ref, m_ref, l_ref, acc_ref, sel_ref):
    R, QB, hd = NSA_R, Q_BLOCK, NSA_HEAD_DIM
    qb = pl.program_id(1)
    kl = lax.broadcasted_iota(jnp.int32, (QB, LANE), 0)
    t_q = qb * QB + lax.broadcasted_iota(jnp.int32, (QB, LANE), 1)
    gates_t = jax.nn.sigmoid(gate_ref[...]).T
    heads = [None] * NSA_HEADS

    def split_r(x):
        return [x[:, r * LANE:(r + 1) * LANE] for r in range(R)]

    G = NSA_KV_GROUPS
    KB = 2 * QB
    qps = [jnp.concatenate([q_ref[:, (R * g + r) * LANE:(R * g + r + 1) * LANE] for r in range(R)], axis=0)
           for g in range(G)]

    krow = lax.broadcasted_iota(jnp.int32, (KB, LANE), 0)
    t_q2 = qb * QB + lax.broadcasted_iota(jnp.int32, (KB, LANE), 1)
    SEL, WIN = 0, 1

    def flash_step(p, branches):
        off = pl.multiple_of(p * KB, KB)
        rel = t_q2 - (p * KB + krow)
        tidx = [jnp.clip(qb - (KB // QB) * p - i, 0, 2) for i in range(KB // QB)]
        bias = [jnp.concatenate([tb_ref[ti, h] for ti in tidx], axis=0) for h in range(NSA_HEADS)]
        chains = [(br, g) for br in branches for g in range(G)]
        k_blk = {br: (ks_ref, kw_ref)[br][pl.ds(off, KB), :] for br in branches}
        v_blk = {br: (vs_ref, vw_ref)[br][pl.ds(off, KB), :] for br in branches}
        scores = [split_r(_dot_nt(k_blk[br], qps[g])) for br, g in chains]
        probs, alphas = [], []
        for (br, g), s_t in zip(chains, scores):
            ch = br * G + g
            if br == SEL:
                blocks = [jnp.broadcast_to(sel_ref[g, pl.ds((KB // SLC_BLOCK) * p + i, 1), :], (SLC_BLOCK, LANE))
                          for i in range(KB // SLC_BLOCK)]
                mask = (rel >= 0) & (jnp.concatenate(blocks, axis=0) > 0.5)
            else:
                mask = (rel >= 0) & (rel < WINDOW)
            s_t = jnp.concatenate([jnp.where(mask, s_t[r] + bias[R * g + r], NEG_INF) for r in range(R)], axis=1)
            m_old = m_ref[ch]
            m_new = jnp.maximum(m_old, jnp.max(s_t, axis=0, keepdims=True))
            e = jnp.exp(s_t - m_new)
            alpha = jnp.exp(m_old - m_new)
            l_ref[ch] = alpha * l_ref[ch] + jnp.sum(e, axis=0, keepdims=True)
            m_ref[ch] = m_new
            probs.append(e.astype(_bf16))
            alphas.append(alpha)
        for (br, g), e, alpha in zip(chains, probs, alphas):
            ch = br * G + g
            pv = _dot_tn(v_blk[br], e)
            acc_ref[ch] = alpha * acc_ref[ch] + pv[g * hd:(g + 1) * hd]

    o_cmp = []
    for g in range(G):
        qp = qps[g]
        s_t = split_r(_dot_nt(kc_ref[0], qp))
        mask_c = (t_q - (kl * CMP_STRIDE + CMP_BLOCK - 1) >= 0) & (kl < NSA_NC)
        s_t = jnp.concatenate([jnp.where(mask_c, s_t[r] + cb_ref[0, R * g + r], NEG_INF) for r in range(R)], axis=1)
        e = split_r(jnp.exp(s_t - jnp.max(s_t, axis=0, keepdims=True)))
        e = jnp.concatenate([jnp.where(mask_c, e[r], 0.0) for r in range(R)], axis=1)
        den = jnp.sum(e, axis=0, keepdims=True)
        p = (e / jnp.where(den > 0.0, den, 1.0)).astype(_bf16)
        o_cmp.append(_dot_tn(vc_ref[0], p)[g * hd:(g + 1) * hd])

        imp = sum(split_r(jnp.dot(ov_ref[...], p, preferred_element_type=_f32)))
        j = lax.broadcasted_iota(jnp.int32, (NSA_NSB, LANE), 0)
        cur = (qb * QB + lax.broadcasted_iota(jnp.int32, (NSA_NSB, LANE), 1)) // SLC_BLOCK
        imp = jnp.where((j == 0) | (j == cur) | (j == cur - 1), FORCE_SCORE, imp)
        imp = jnp.where(j <= cur, imp, NEG_INF)
        cnt = jnp.zeros((NSA_NSB, LANE), _f32)
        for jo in range(NSA_NSB):
            other = imp[jo:jo + 1, :]
            beats = (other > imp) | ((other == imp) & (j > jo))
            cnt = cnt + jnp.where(beats, 1.0, 0.0)
        sel_ref[g] = jnp.where(cnt < SLC_TOP_N, 1.0, 0.0)

    p_hi = qb // (KB // QB) + 1
    n_win = p_hi - jnp.maximum(qb - WINDOW // QB, 0) // (KB // QB)
    m_ref[...] = jnp.full(m_ref.shape, NEG_INF, _f32)
    l_ref[...] = jnp.zeros(l_ref.shape, _f32)
    acc_ref[...] = jnp.zeros(acc_ref.shape, _f32)

    def both(i, carry):
        flash_step(p_hi - 1 - i, (SEL, WIN))
        return carry

    def selected_only(i, carry):
        flash_step(p_hi - 1 - i, (SEL,))
        return carry

    lax.fori_loop(0, n_win, both, 0)
    lax.fori_loop(n_win, p_hi, selected_only, 0)
    o_sel = [acc_ref[SEL * G + g] / l_ref[SEL * G + g] for g in range(G)]
    o_win = [acc_ref[WIN * G + g] / l_ref[WIN * G + g] for g in range(G)]

    for g in range(G):
        o_c, o_s, o_w = split_r(o_cmp[g]), split_r(o_sel[g]), split_r(o_win[g])
        for r in range(R):
            h = R * g + r
            heads[h] = (gates_t[3 * h:3 * h + 1] * o_c[r] + gates_t[3 * h + 1:3 * h + 2] * o_s[r]
                        + gates_t[3 * h + 2:3 * h + 3] * o_w[r])

    for pk in range(NSA_HEADS * hd // LANE):
        per = LANE // hd
        pair = jnp.concatenate(heads[per * pk:per * (pk + 1)], axis=0)
        o_ref[:, pk * LANE:(pk + 1) * LANE] = pair.T.astype(o_ref.dtype)


def _nsa_attention(proj_a, proj_b, kc, vc, tb, cb, overlap_t):
    B = kc.shape[0]
    nqb = NSA_NQB
    qcols = NSA_HEADS * LANE
    kv = lambda blk: pl.BlockSpec((SEQ, LANE), lambda b, q: (b, qcols // LANE + blk))
    cspec = pl.BlockSpec((1, NSA_CMP_ROWS, LANE), lambda b, q: (b, 0, 0))
    return pl.pallas_call(
        _nsa_kernel,
        out_shape=jax.ShapeDtypeStruct((B * SEQ, NSA_HEADS * NSA_HEAD_DIM), _bf16),
        grid=(B, nqb),
        in_specs=[pl.BlockSpec((Q_BLOCK, qcols), lambda b, q: (b * nqb + q, 0)),
                  kv(0), kv(1), kv(2), kv(3), cspec, cspec,
                  pl.BlockSpec((Q_BLOCK, LANE), lambda b, q: (b * nqb + q, PB['misc'] // LANE)),
                  pl.BlockSpec((3, NSA_HEADS, Q_BLOCK, LANE), lambda b, q: (0, 0, 0, 0)),
                  pl.BlockSpec((1, NSA_HEADS, Q_BLOCK, LANE), lambda b, q: (q, 0, 0, 0)),
                  pl.BlockSpec((NSA_NSB, LANE), lambda b, q: (0, 0))],
        out_specs=pl.BlockSpec((Q_BLOCK, NSA_HEADS * NSA_HEAD_DIM), lambda b, q: (b * nqb + q, 0)),
        scratch_shapes=[pltpu.VMEM((2 * NSA_KV_GROUPS, 1, NSA_R * Q_BLOCK), _f32),
                        pltpu.VMEM((2 * NSA_KV_GROUPS, 1, NSA_R * Q_BLOCK), _f32),
                        pltpu.VMEM((2 * NSA_KV_GROUPS, NSA_HEAD_DIM, NSA_R * Q_BLOCK), _f32),
                        pltpu.VMEM((NSA_KV_GROUPS, NSA_NSB, Q_BLOCK), _f32)],
        compiler_params=pltpu.CompilerParams(dimension_semantics=("parallel", "arbitrary"),
                                             vmem_limit_bytes=VMEM_LIMIT),
        name="nsa_attention",
    )(proj_a, proj_a, proj_a, proj_a, proj_a, kc, vc, proj_b, tb, cb, overlap_t)


def _nsa_overlap():
    n = jnp.arange(NSA_CMP_ROWS)[None, :]
    jj = jnp.arange(NSA_NSB)[:, None]
    return ((n * CMP_STRIDE <= jj * SLC_BLOCK + SLC_BLOCK - 1)
            & (n * CMP_STRIDE + CMP_BLOCK - 1 >= jj * SLC_BLOCK) & (n < NSA_NC)).astype(_bf16)


def _nsa_weight_prep(w_in, cmp_pos, cmp_w1, cmp_w2):
    G, hd = NSA_KV_GROUPS, NSA_HEAD_DIM
    eye = jnp.eye(G, dtype=_f32)
    nq = NSA_HEADS * hd
    wq = w_in[:, :, :nq].reshape(DEPTH, D_MODEL, NSA_HEADS, 1, hd) * (hd ** -0.5)
    head_group = (jnp.arange(NSA_HEADS)[:, None] // NSA_R == jnp.arange(G)[None, :]).astype(_f32)
    wq = (wq * head_group[None, None, :, :, None]).reshape(DEPTH, D_MODEL, NSA_HEADS * LANE)
    slc_win = w_in[:, :, nq + 2 * NSA_KV:nq + 6 * NSA_KV]
    w_a = jnp.concatenate([wq, slc_win], axis=-1).astype(_bf16)
    w_b = jnp.concatenate([w_in[:, :, _IN_OFF[n]:_IN_OFF[n] + w] for n, w in _PB_SEGMENTS], axis=-1)
    w_b = jnp.pad(w_b, ((0, 0), (0, 0), (0, PROJ_B_COLS - PROJ_B_USED))).astype(_bf16)
    half = CMP_BLOCK // 2
    w1 = cmp_w1.reshape(DEPTH, 2, CMP_BLOCK, hd, hd)
    w1_x = jnp.einsum('lxide,gh->lxigdhe', w1, eye).reshape(DEPTH, 2, 2, half, G * hd, G * hd).astype(_bf16)
    w2_x = jnp.einsum('lxde,gh->lxgdhe', cmp_w2, eye).reshape(DEPTH, 2, G * hd, G * hd).astype(_bf16)
    pos_x = jnp.broadcast_to(cmp_pos[:, :, :, None, :], (DEPTH, 2, CMP_BLOCK, G, hd))
    pos_x = pos_x.reshape(DEPTH, 2, 2, half, 1, G * hd)
    return w_a, w_b, w1_x, w2_x, pos_x


REC_BLOCK = 2 * CHUNK
assert REC_BLOCK == LANE and SEQ % REC_BLOCK == 0


def _split3(x):
    hi = x.astype(_bf16)
    r = x - hi.astype(_f32)
    mid = r.astype(_bf16)
    lo = (r - mid.astype(_f32)).astype(_bf16)
    return hi, mid, lo


def _dot_exact_rhs(m, x):
    return sum(jnp.dot(m, p, preferred_element_type=_f32) for p in _split3(x))


def _dot_exact_lhs(x, m):
    return sum(jnp.dot(p, m, preferred_element_type=_f32) for p in _split3(x))


def _dot_tn(a, b):
    return lax.dot_general(a, b, (((0,), (0,)), ((), ())), preferred_element_type=_f32)


def _mm3(x, y):
    xh = x.astype(_bf16)
    xl = (x - xh.astype(_f32)).astype(_bf16)
    yh = y.astype(_bf16)
    yl = (y - yh.astype(_f32)).astype(_bf16)
    return (jnp.dot(xh, yh, preferred_element_type=_f32) + jnp.dot(xh, yl, preferred_element_type=_f32)
            + jnp.dot(xl, yh, preferred_element_type=_f32))


def _chunk_masks():
    ri = lax.broadcasted_iota(jnp.int32, (REC_BLOCK, REC_BLOCK), 0)
    ci = lax.broadcasted_iota(jnp.int32, (REC_BLOCK, REC_BLOCK), 1)
    same = (ri // CHUNK) == (ci // CHUNK)
    return ri, ci, (ci <= ri) & same, (ci < ri) & same, (ri <= ci) & same


def _as_mxu(mask):
    return jnp.where(mask, 1.0, 0.0).astype(_bf16)


def _softplus(x):
    return jnp.maximum(x, 0.0) + jnp.log1p(jnp.exp(-jnp.abs(x)))


def _silu(x):
    return x * jax.nn.sigmoid(x)


def _chunk_last(x):
    ri = lax.broadcasted_iota(jnp.int32, x.shape, 0)
    return jnp.where(ri < CHUNK, x[CHUNK - 1:CHUNK], x[2 * CHUNK - 1:2 * CHUNK])


def _expand_heads(x, lane0, nheads, width):
    per = LANE // width
    lane = lax.broadcasted_iota(jnp.int32, (x.shape[0], LANE), 1)
    pieces = []
    for p0 in range(lane0, lane0 + nheads, per):
        piece = jnp.broadcast_to(x[:, p0:p0 + 1], (x.shape[0], LANE))
        for k in range(1, per):
            piece = jnp.where(lane < k * width, piece, jnp.broadcast_to(x[:, p0 + k:p0 + k + 1], (x.shape[0], LANE)))
        pieces.append(piece)
    return jnp.concatenate(pieces, axis=1) if len(pieces) > 1 else pieces[0]


def _conv_silu(x, prev, w_ref, c0, bias=None):
    n, C = x.shape
    ntap = w_ref.shape[0]
    rows = lax.broadcasted_iota(jnp.int32, (n, C), 0)
    acc = x * w_ref[ntap - 1:ntap, c0:c0 + C]
    for s in range(1, ntap):
        xs = jnp.where(rows < s, pltpu.roll(prev, s, 0), pltpu.roll(x, s, 0))
        acc = acc + xs * w_ref[ntap - 1 - s:ntap - s, c0:c0 + C]
    if bias is not None:
        acc = acc + bias
    return _silu(acc)


def _rms(x, w):
    return x * lax.rsqrt(jnp.mean(x * x, axis=-1, keepdims=True) + EPS) * w


def _ssd_kernel(z_ref, xbc_ref, dt_ref, cw_ref, cb_ref, dtb_ref, alog_ref, dskip_ref, nw_ref, o_ref,
                prev_ref, state_ref):
    G, R, P, N = SSD_GROUPS, SSD_HEADS // SSD_GROUPS, SSD_HEAD_DIM, SSD_STATE

    @pl.when(pl.program_id(1) == 0)
    def _():
        prev_ref[...] = jnp.zeros(prev_ref.shape, _f32)
        state_ref[...] = jnp.zeros(state_ref.shape, _f32)

    x_in = xbc_ref[...]
    xc = _conv_silu(x_in, prev_ref[...], cw_ref, 0, cb_ref[...])
    prev_ref[...] = x_in
    _, _, tril, _, triu = _chunk_masks()
    lane = lax.broadcasted_iota(jnp.int32, (REC_BLOCK, LANE), 1)
    L0 = PBL['sdt']
    dt = _softplus(dt_ref[...] + dtb_ref[...])
    da = dt * (-jnp.exp(alog_ref[...]))
    a_cum = _dot_exact_rhs(_as_mxu(tril), da)
    a_cum_t = _dot_exact_lhs(da.T, _as_mxu(triu))
    a_last = _chunk_last(a_cum)
    xs = xc[:, :SSD_INNER]
    xdt = xs * _expand_heads(dt, L0, SSD_HEADS, P)
    xdtd = (xdt * _expand_heads(jnp.exp(a_last - a_cum), L0, SSD_HEADS, P)).astype(_bf16)
    xdt_b = xdt.astype(_bf16)
    ea = _expand_heads(jnp.exp(a_cum), L0, SSD_HEADS, P)
    y_groups = []
    for g in range(G):
        bm = xc[:, SSD_INNER + g * N:SSD_INNER + (g + 1) * N].astype(_bf16)
        cm = xc[:, SSD_INNER + (G + g) * N:SSD_INNER + (G + g + 1) * N].astype(_bf16)
        cbm = _dot_nt(cm, bm)
        intra = []
        for pr in range(R // 2):
            both = []
            for k in range(2):
                h = g * R + 2 * pr + k
                seg = jnp.exp(jnp.where(tril, a_cum[:, L0 + h:L0 + h + 1] - a_cum_t[L0 + h:L0 + h + 1, :], -jnp.inf))
                both.append(jnp.dot((cbm * seg).astype(_bf16), xdt_b[:, (h - k) * P:(h - k + 2) * P],
                                    preferred_element_type=_f32))
            intra.append(jnp.where(lane < P, both[0], both[1]))
        y_intra = jnp.concatenate(intra, axis=1)
        prev_rows = []
        for c in range(REC_BLOCK // CHUNK):
            rows = slice(c * CHUNK, (c + 1) * CHUNK)
            st = state_ref[g]
            prev_rows.append(jnp.dot(cm[rows], st.astype(_bf16), preferred_element_type=_f32))
            dec = _expand_heads(jnp.exp(a_cum[(c + 1) * CHUNK - 1:(c + 1) * CHUNK]), L0, SSD_HEADS, P)
            state_ref[g] = (st * dec[:, g * R * P:(g + 1) * R * P]
                            + _dot_tn(bm[rows], xdtd[rows, g * R * P:(g + 1) * R * P]))
        y_groups.append(y_intra + jnp.concatenate(prev_rows, axis=0) * ea[:, g * R * P:(g + 1) * R * P])
    y = jnp.concatenate(y_groups, axis=1) + xs * dskip_ref[...]
    y = y * _silu(z_ref[...])
    gw = SSD_INNER // G
    for g in range(G):
        o_ref[:, g * gw:(g + 1) * gw] = _rms(y[:, g * gw:(g + 1) * gw], nw_ref[:, g * gw:(g + 1) * gw]).astype(o_ref.dtype)


def _gdn_kernel(q_ref, k_ref, v_ref, z_ref, ba_ref, cw_ref, dtb_ref, alog_ref, nw_ref, o_ref,
                pq_ref, pk_ref, pv_ref, state_ref):
    H, Dh = GDN_HEADS, GDN_HEAD_DIM

    @pl.when(pl.program_id(1) == 0)
    def _():
        for r in (pq_ref, pk_ref, pv_ref, state_ref):
            r[...] = jnp.zeros(r.shape, _f32)

    conv = []
    for i, (x_ref, p_ref) in enumerate(((q_ref, pq_ref), (k_ref, pk_ref), (v_ref, pv_ref))):
        x_in = x_ref[...]
        conv.append(_conv_silu(x_in, p_ref[...], cw_ref, i * GDN_WIDTH))
        p_ref[...] = x_in
    q, k, v = conv
    ri, ci, tril, strict, triu = _chunk_masks()
    ba = ba_ref[...]
    beta = jax.nn.sigmoid(ba)
    gl = -jnp.exp(alog_ref[...]) * _softplus(ba + dtb_ref[...])
    gcum = _dot_exact_rhs(_as_mxu(tril), gl)
    gcum_t = _dot_exact_lhs(gl.T, _as_mxu(triu))
    glast = _chunk_last(gcum)
    zero_rows = jnp.zeros((CHUNK, Dh), _f32)
    eye = jnp.where(tril & jnp.logical_not(strict), 1.0, 0.0)
    hs = range(H)
    sls = [slice(h * Dh, (h + 1) * Dh) for h in hs]
    qq = [q[:, sl] * lax.rsqrt(jnp.sum(q[:, sl] * q[:, sl], axis=-1, keepdims=True) + EPS) * (Dh ** -0.5) for sl in sls]
    kk = [k[:, sl] * lax.rsqrt(jnp.sum(k[:, sl] * k[:, sl], axis=-1, keepdims=True) + EPS) for sl in sls]
    lb = [PBL['gbeta'] + h for h in hs]
    lg = [PBL['gbeta'] + H + h for h in hs]
    bcol = [beta[:, lb[h]:lb[h] + 1] for h in hs]
    gcol = [gcum[:, lg[h]:lg[h] + 1] for h in hs]
    decay = [jnp.exp(jnp.where(tril, gcol[h] - gcum_t[lg[h]:lg[h] + 1, :], -jnp.inf)) for h in hs]
    kb = [kk[h] * bcol[h] for h in hs]
    s = [_dot_nt(jnp.concatenate([kb[h], qq[h]], axis=0).astype(_bf16), kk[h].astype(_bf16)) for h in hs]
    a_mat = [jnp.where(strict, s[h][:REC_BLOCK] * decay[h], 0.0) for h in hs]
    aqk = [(s[h][REC_BLOCK:] * decay[h]).astype(_bf16) for h in hs]
    SUB = 8
    same = lambda n: (ri // n) == (ci // n)
    a_sub = [jnp.where(same(SUB), a_mat[h], 0.0) for h in hs]
    tinv = [eye - a_sub[h] for h in hs]
    pw = a_sub
    for _ in range(SUB.bit_length() - 2):
        pw = [_mm3(pw[h], pw[h]) for h in hs]
        tinv = [tinv[h] + _mm3(tinv[h], pw[h]) for h in hs]
    n = SUB
    while n < CHUNK:
        enclosed = same(2 * n) & jnp.logical_not(same(n))
        tc = [_mm3(tinv[h], jnp.where(enclosed, a_mat[h], 0.0)) for h in hs]
        tinv = [tinv[h] - _mm3(tc[h], tinv[h]) for h in hs]
        n *= 2
    sol = [_mm3(tinv[h], jnp.concatenate([v[:, sls[h]] * bcol[h], kb[h] * jnp.exp(gcol[h])], axis=1)) for h in hs]
    u = [sol[h][:, :Dh] for h in hs]
    w = [sol[h][:, Dh:].astype(_bf16) for h in hs]
    q_dec = [(qq[h] * jnp.exp(gcol[h])).astype(_bf16) for h in hs]
    k_end = [(kk[h] * jnp.exp(glast[:, lg[h]:lg[h] + 1] - gcol[h])).astype(_bf16) for h in hs]
    o_rows = [[] for _ in hs]
    for c in range(REC_BLOCK // CHUNK):
        rows = slice(c * CHUNK, (c + 1) * CHUNK)
        st = [state_ref[h] for h in hs]
        st_b = [st[h].astype(_bf16) for h in hs]
        v_new = [u[h][rows] - jnp.dot(w[h][rows], st_b[h], preferred_element_type=_f32) for h in hs]
        v_full = [jnp.concatenate([v_new[h], zero_rows] if c == 0 else [zero_rows, v_new[h]], axis=0).astype(_bf16)
                  for h in hs]
        for h in hs:
            o_rows[h].append(jnp.dot(q_dec[h][rows], st_b[h], preferred_element_type=_f32)
                             + jnp.dot(aqk[h][rows], v_full[h], preferred_element_type=_f32))
            d_last = jnp.exp(gcum[(c + 1) * CHUNK - 1:(c + 1) * CHUNK, lg[h]:lg[h] + 1])
            state_ref[h] = st[h] * d_last + _dot_tn(k_end[h][rows], v_new[h].astype(_bf16))
    for h in hs:
        o = _rms(jnp.concatenate(o_rows[h], axis=0), nw_ref[...]) * _silu(z_ref[:, sls[h]])
        o_ref[:, sls[h]] = o.astype(o_ref.dtype)


def _gla_kernel(q_ref, k_ref, v_ref, go_ref, lr_ref, w2_ref, gb_ref, nw_ref, o_ref, state_ref):
    H, Dk, Dv = GLA_HEADS, GLA_DK, GLA_DV

    @pl.when(pl.program_id(1) == 0)
    def _():
        state_ref[...] = jnp.zeros(state_ref.shape, _f32)

    _, _, tril, _, _ = _chunk_masks()
    lane = lax.broadcasted_iota(jnp.int32, (REC_BLOCK, LANE), 1)
    pre = jnp.dot(lr_ref[...].astype(_bf16), w2_ref[...], preferred_element_type=_f32) + gb_ref[...]
    gk = (jnp.minimum(pre, 0.0) - jnp.log1p(jnp.exp(-jnp.abs(pre)))) / GLA_GATE_NORM
    bcum = _dot_exact_rhs(_as_mxu(tril), gk)
    blast = _chunk_last(bcum)
    q_dec = q_ref[...] * (Dk ** -0.5) * jnp.exp(bcum)
    k_inv = (k_ref[...] * jnp.exp(-bcum)).astype(_bf16)
    k_end = (k_ref[...] * jnp.exp(blast - bcum)).astype(_bf16)
    per = LANE // Dk
    for pr in range(H // per):
        psl = slice(pr * LANE, (pr + 1) * LANE)
        qd, ki, ke = q_dec[:, psl], k_inv[:, psl], k_end[:, psl]
        qm, vh, o_intra = [], [], []
        for k in range(per):
            h = pr * per + k
            qm.append(jnp.where(lane // Dk == k, qd, 0.0).astype(_bf16))
            vh.append(v_ref[:, h * Dv:(h + 1) * Dv].astype(_bf16))
            attn = jnp.where(tril, _dot_nt(qm[k], ki), 0.0).astype(_bf16)
            o_intra.append(jnp.dot(attn, vh[k], preferred_element_type=_f32))
        o_prev = [[] for _ in range(per)]
        for c in range(REC_BLOCK // CHUNK):
            rows = slice(c * CHUNK, (c + 1) * CHUNK)
            st = state_ref[pr]
            st_b = st.astype(_bf16)
            loc = None
            for k in range(per):
                o_prev[k].append(_dot_nt(qm[k][rows], st_b))
                lk = _dot_tn(vh[k][rows], ke[rows])
                loc = lk if loc is None else jnp.where(lane < k * Dk, loc, lk)
            state_ref[pr] = st * jnp.exp(bcum[(c + 1) * CHUNK - 1:(c + 1) * CHUNK, psl]) + loc
        for k in range(per):
            h = pr * per + k
            o = o_intra[k] + jnp.concatenate(o_prev[k], axis=0)
            o = _rms(o, nw_ref[...]) * _silu(go_ref[:, h * Dv:(h + 1) * Dv])
            o_ref[:, h * Dv:(h + 1) * Dv] = o.astype(o_ref.dtype)


def _rec_call(body, proj_b, col_blocks, params, out_cols, scratch, name):
    B = proj_b.shape[0] // SEQ
    nblk = SEQ // REC_BLOCK
    in_specs = [pl.BlockSpec((REC_BLOCK, w), (lambda b, t, c=c0 // w: (b * nblk + t, c))) for c0, w in col_blocks]
    for p in params:
        in_specs.append(pl.BlockSpec(p.shape, lambda b, t, nd=p.ndim: (0,) * nd))
    return pl.pallas_call(
        body,
        out_shape=jax.ShapeDtypeStruct((B * SEQ, out_cols), _bf16),
        grid=(B, nblk),
        in_specs=in_specs,
        out_specs=pl.BlockSpec((REC_BLOCK, out_cols), lambda b, t: (b * nblk + t, 0)),
        scratch_shapes=scratch,
        compiler_params=pltpu.CompilerParams(dimension_semantics=("parallel", "arbitrary"),
                                             vmem_limit_bytes=VMEM_LIMIT),
        name=name,
    )(*([proj_b] * len(col_blocks)), *params)


def _lane_pad(v, lane0=0):
    return jnp.pad(v.astype(_f32), (lane0, LANE - lane0 - v.shape[0]))[None]


def _ssd_call(proj_b, conv_w, conv_b, dt_bias, a_log, d_skip, norm_w):
    cols = [(PB['sz'], SSD_INNER), (PB['sxbc'], SSD_XBC), (PB['misc'], LANE)]
    params = [conv_w, conv_b[None], _lane_pad(dt_bias, PBL['sdt']), _lane_pad(a_log, PBL['sdt']),
              jnp.repeat(d_skip, SSD_HEAD_DIM)[None], norm_w[None]]
    scratch = [pltpu.VMEM((REC_BLOCK, SSD_XBC), _f32),
               pltpu.VMEM((SSD_GROUPS, SSD_STATE, SSD_INNER // SSD_GROUPS), _f32)]
    return _rec_call(_ssd_kernel, proj_b, cols, params, SSD_INNER, scratch, "ssd")


def _gdn_call(proj_b, conv_w, dt_bias, a_log, norm_w):
    W = GDN_WIDTH
    cols = [(PB['gq'], W), (PB['gk'], W), (PB['gv'], W), (PB['gz'], W), (PB['misc'], LANE)]
    decay_lane = PBL['gbeta'] + GDN_HEADS
    params = [conv_w, _lane_pad(dt_bias, decay_lane), _lane_pad(a_log, decay_lane), norm_w[None]]
    scratch = [pltpu.VMEM((REC_BLOCK, W), _f32)] * 3 + [pltpu.VMEM((GDN_HEADS, GDN_HEAD_DIM, GDN_HEAD_DIM), _f32)]
    return _rec_call(_gdn_kernel, proj_b, cols, params, W, scratch, "gdn")


def _gla_call(proj_b, gate_w2, gate_b, norm_w):
    cols = [(PB['lq'], GLA_KEY), (PB['lk'], GLA_KEY), (PB['lv'], GLA_VAL), (PB['lg'], GLA_VAL), (PB['misc'], LANE)]
    w2 = jnp.pad(gate_w2, ((PBL['llr'], LANE - PBL['llr'] - GLA_GATE_RANK), (0, 0))).astype(_bf16)
    params = [w2, gate_b[None], norm_w[None]]
    scratch = [pltpu.VMEM((GLA_HEADS * GLA_DK // LANE, GLA_DV, LANE), _f32)]
    return _rec_call(_gla_kernel, proj_b, cols, params, GLA_VAL, scratch, "gla")


def kernel(x, c, rel_bias, norm1_w, norm2_w, ada_w, ada_b, w_in, w_out, nsa_cmp_pos, nsa_cmp_w1, nsa_cmp_w2, ssd_conv_w, ssd_conv_b, ssd_dt_bias, ssd_a_log, ssd_d, ssd_norm_w, gdn_conv_w, gdn_dt_bias, gdn_a_log, gdn_norm_w, gla_gate_w2, gla_gate_b, gla_norm_w, mlp_w1, mlp_w2, final_norm_w):
    B, S, D = x.shape
    mod = _ada_all(c, ada_w, ada_b).reshape(DEPTH, B, 6, 1, D)
    w_a, w_b, cmp_w1_x, cmp_w2_x, cmp_pos_x = _nsa_weight_prep(w_in, nsa_cmp_pos, nsa_cmp_w1, nsa_cmp_w2)
    w_out_b = w_out.astype(_bf16)
    w1_b = mlp_w1.astype(_bf16)
    w2_b = mlp_w2.astype(_bf16)
    tb, cb = _nsa_bias_tiles(rel_bias)
    overlap_t = _nsa_overlap()
    xf = x.reshape(TOKENS, D)
    no_mod = jnp.zeros((B, 1, D), _f32)
    h = _norm_mod(xf, norm1_w[0][None], mod[0, :, 1], mod[0, :, 0])
    for l in range(DEPTH):
        sh1, sc1, g1, sh2, sc2, g2 = (mod[l, :, i] for i in range(6))
        proj_a = _matmul(h, w_a[l], out_dtype=_bf16)
        proj_b = _matmul(h, w_b[l], tn=PROJ_B_TN)
        kc, vc = _nsa_compress(proj_b, cmp_pos_x[l], cmp_w1_x[l], cmp_w2_x[l])
        y_nsa = _nsa_attention(proj_a, proj_b, kc, vc, tb, cb, overlap_t)
        y_ssd = _ssd_call(proj_b, ssd_conv_w[l], ssd_conv_b[l], ssd_dt_bias[l], ssd_a_log[l], ssd_d[l], ssd_norm_w[l])
        y_gdn = _gdn_call(proj_b, gdn_conv_w[l], gdn_dt_bias[l], gdn_a_log[l], gdn_norm_w[l])
        y_gla = _gla_call(proj_b, gla_gate_w2[l], gla_gate_b[l], gla_norm_w[l])
        xf, h2 = _out_proj((y_nsa, y_ssd, y_gdn, y_gla), w_out_b[l], xf, g1,
                           norm2_w[l][None], sc2, sh2)
        if l + 1 < DEPTH:
            xf, h = _mlp(h2, w1_b[l], w2_b[l], xf, g2, norm1_w[l + 1][None], mod[l + 1, :, 1], mod[l + 1, :, 0], _bf16)
        else:
            _, out = _mlp(h2, w1_b[l], w2_b[l], xf, g2, final_norm_w[None], no_mod, no_mod, _f32)
    return out.reshape(B, S, D)
```

```python
import math
from functools import partial

import jax
import jax.numpy as jnp
from jax import lax
from jax.experimental import pallas as pl
from jax.experimental.pallas import tpu as pltpu

D_MODEL = 2048
BATCH = 16
SEQ = 2048
DEPTH = 4

MIX_GROUP = D_MODEL // 4
NSA_HEAD_DIM = 64
NSA_HEADS = MIX_GROUP // NSA_HEAD_DIM
NSA_KV_GROUPS = max(1, NSA_HEADS // 4)
NSA_KV = NSA_KV_GROUPS * NSA_HEAD_DIM
CMP_BLOCK = 32
CMP_STRIDE = 16
SLC_BLOCK = 64
SLC_TOP_N = 8
WINDOW = 512
Q_BLOCK = 128
REL_BUCKETS = 32
REL_MAX_DIST = 128
SSD_HEAD_DIM = 64
SSD_HEADS = MIX_GROUP // SSD_HEAD_DIM
SSD_INNER = SSD_HEADS * SSD_HEAD_DIM
SSD_GROUPS = 2
SSD_STATE = 128
SSD_CONV = 4
SSD_XBC = SSD_INNER + 2 * SSD_GROUPS * SSD_STATE
GDN_HEAD_DIM = 128
GDN_HEADS = MIX_GROUP // GDN_HEAD_DIM
GDN_WIDTH = GDN_HEADS * GDN_HEAD_DIM
GDN_CONV = 4
GLA_DV = 128
GLA_HEADS = MIX_GROUP // GLA_DV
GLA_DK = GLA_DV // 2
GLA_KEY = GLA_HEADS * GLA_DK
GLA_VAL = GLA_HEADS * GLA_DV
GLA_GATE_RANK = 16
GLA_GATE_NORM = 16.0
CHUNK = 64
MLP_HIDDEN = 4 * D_MODEL
EPS = 1e-6
NEG_INF = -1e30
FORCE_SCORE = 1e9
IN_SPLITS = (NSA_HEADS * NSA_HEAD_DIM, NSA_KV, NSA_KV, NSA_KV, NSA_KV, NSA_KV, NSA_KV, NSA_HEADS * 3,
             SSD_INNER, SSD_XBC, SSD_HEADS,
             GDN_WIDTH, GDN_WIDTH, GDN_WIDTH, GDN_WIDTH, GDN_HEADS, GDN_HEADS,
             GLA_KEY, GLA_KEY, GLA_VAL, GLA_VAL, GLA_GATE_RANK)
IN_COLS = sum(IN_SPLITS)
MIX_OUT = NSA_HEADS * NSA_HEAD_DIM + SSD_INNER + GDN_WIDTH + GLA_VAL

LANE = 128
VMEM_LIMIT = 56 * 1024 * 1024
TOKENS = BATCH * SEQ
_IN_NAMES = ('nq', 'nkc', 'nvc', 'nks', 'nvs', 'nkw', 'nvw', 'ngate', 'sz', 'sxbc', 'sdt',
             'gq', 'gk', 'gv', 'gz', 'gbeta', 'ga', 'lq', 'lk', 'lv', 'lg', 'llr')
_IN_W = dict(zip(_IN_NAMES, IN_SPLITS))
_IN_OFF = {n: sum(IN_SPLITS[:i]) for i, n in enumerate(_IN_NAMES)}
_PB_SEGMENTS = (('sxbc', SSD_XBC), ('gq', GDN_WIDTH), ('gk', GDN_WIDTH), ('gv', GDN_WIDTH), ('gz', GDN_WIDTH),
                ('sz', SSD_INNER), ('lv', GLA_VAL), ('lg', GLA_VAL), ('lq', GLA_KEY), ('lk', GLA_KEY),
                ('nkc', NSA_KV), ('nvc', NSA_KV),
                ('ngate', NSA_HEADS * 3), ('sdt', SSD_HEADS), ('gbeta', 2 * GDN_HEADS), ('llr', GLA_GATE_RANK))
_PB_MISC = ('ngate', 'sdt', 'gbeta', 'llr')
PB, PBL = {}, {}
PROJ_B_USED = 0
for _n, _w in _PB_SEGMENTS:
    if _n in _PB_MISC:
        PB.setdefault('misc', PROJ_B_USED - PROJ_B_USED % LANE)
        PBL[_n] = PROJ_B_USED - PB['misc']
    else:
        assert PROJ_B_USED % _w == 0 and _w % LANE == 0
        PB[_n] = PROJ_B_USED
    PROJ_B_USED += _w
assert PROJ_B_USED - PB['misc'] <= LANE and PBL['ngate'] == 0 and _IN_OFF['ga'] == _IN_OFF['gbeta'] + GDN_HEADS
PROJ_B_TN = 512
PROJ_B_COLS = -(-PROJ_B_USED // PROJ_B_TN) * PROJ_B_TN

_bf16 = jnp.bfloat16
_f32 = jnp.float32


def _ada_kernel(c_ref, w_ref, b_ref, o_ref):
    c = c_ref[...]
    c_act = c * jax.nn.sigmoid(c)
    o_ref[0] = jnp.dot(c_act, w_ref[0], preferred_element_type=_f32) + b_ref[0]


def _ada_all(c, ada_w, ada_b):
    tn = 1024
    return pl.pallas_call(
        _ada_kernel,
        out_shape=jax.ShapeDtypeStruct((DEPTH, BATCH, 6 * D_MODEL), _f32),
        grid=(DEPTH, 6 * D_MODEL // tn),
        in_specs=[pl.BlockSpec((BATCH, D_MODEL), lambda l, j: (0, 0)),
                  pl.BlockSpec((1, D_MODEL, tn), lambda l, j: (l, 0, j)),
                  pl.BlockSpec((1, 1, tn), lambda l, j: (l, 0, j))],
        out_specs=pl.BlockSpec((1, BATCH, tn), lambda l, j: (l, 0, j)),
        compiler_params=pltpu.CompilerParams(dimension_semantics=("parallel", "parallel"),
                                             vmem_limit_bytes=VMEM_LIMIT),
        name="ada_mod",
    )(c, ada_w, ada_b.reshape(DEPTH, 1, 6 * D_MODEL))


def _rms_mod(x, w, sc, sh):
    y = x * lax.rsqrt(jnp.mean(x * x, axis=-1, keepdims=True) + EPS)
    return (y * w) * (1.0 + sc) + sh


def _norm_mod_kernel(x_ref, w_ref, sc_ref, sh_ref, o_ref):
    o_ref[...] = _rms_mod(x_ref[...], w_ref[...], sc_ref[0], sh_ref[0]).astype(o_ref.dtype)


def _norm_mod(x, w, sc, sh, tm=512):
    per_b = SEQ // tm
    return pl.pallas_call(
        _norm_mod_kernel,
        out_shape=jax.ShapeDtypeStruct((TOKENS, D_MODEL), _bf16),
        grid=(TOKENS // tm,),
        in_specs=[pl.BlockSpec((tm, D_MODEL), lambda i: (i, 0)),
                  pl.BlockSpec((1, D_MODEL), lambda i: (0, 0)),
                  pl.BlockSpec((1, 1, D_MODEL), lambda i: (i // per_b, 0, 0)),
                  pl.BlockSpec((1, 1, D_MODEL), lambda i: (i // per_b, 0, 0))],
        out_specs=pl.BlockSpec((tm, D_MODEL), lambda i: (i, 0)),
        compiler_params=pltpu.CompilerParams(dimension_semantics=("parallel",),
                                             vmem_limit_bytes=VMEM_LIMIT),
        name="norm_mod",
    )(x, w, sc, sh)


def _matmul_kernel(a_ref, w_ref, o_ref):
    o_ref[...] = jnp.dot(a_ref[...], w_ref[...], preferred_element_type=_f32).astype(o_ref.dtype)


def _matmul(a, w, tm=1024, tn=512, out_dtype=_f32):
    M, K = a.shape
    N = w.shape[1]
    return pl.pallas_call(
        _matmul_kernel,
        out_shape=jax.ShapeDtypeStruct((M, N), out_dtype),
        grid=(M // tm, N // tn),
        in_specs=[pl.BlockSpec((tm, K), lambda i, j: (i, 0)),
                  pl.BlockSpec((K, tn), lambda i, j: (0, j))],
        out_specs=pl.BlockSpec((tm, tn), lambda i, j: (i, j)),
        compiler_params=pltpu.CompilerParams(dimension_semantics=("parallel", "parallel"),
                                             vmem_limit_bytes=VMEM_LIMIT),
        name="in_proj",
    )(a, w)


def _out_proj_kernel(a0_ref, a1_ref, a2_ref, a3_ref, w_ref, x_ref, g_ref, nw_ref, sc_ref, sh_ref, xo_ref, ho_ref):
    y = None
    for i, a_ref in enumerate((a0_ref, a1_ref, a2_ref, a3_ref)):
        part = jnp.dot(a_ref[...], w_ref[i * MIX_GROUP:(i + 1) * MIX_GROUP, :], preferred_element_type=_f32)
        y = part if y is None else y + part
    xn = x_ref[...] + g_ref[0] * y
    xo_ref[...] = xn
    ho_ref[...] = _rms_mod(xn, nw_ref[...], sc_ref[0], sh_ref[0]).astype(ho_ref.dtype)


def _out_proj(mixed, w, x, g, nw, sc, sh, tm=512):
    per_b = SEQ // tm
    bspec = pl.BlockSpec((1, 1, D_MODEL), lambda i: (i // per_b, 0, 0))
    aspec = pl.BlockSpec((tm, MIX_GROUP), lambda i: (i, 0))
    return pl.pallas_call(
        _out_proj_kernel,
        out_shape=(jax.ShapeDtypeStruct((TOKENS, D_MODEL), _f32),
                   jax.ShapeDtypeStruct((TOKENS, D_MODEL), _bf16)),
        grid=(TOKENS // tm,),
        in_specs=[aspec, aspec, aspec, aspec,
                  pl.BlockSpec((MIX_OUT, D_MODEL), lambda i: (0, 0)),
                  pl.BlockSpec((tm, D_MODEL), lambda i: (i, 0)),
                  bspec,
                  pl.BlockSpec((1, D_MODEL), lambda i: (0, 0)),
                  bspec, bspec],
        out_specs=(pl.BlockSpec((tm, D_MODEL), lambda i: (i, 0)),
                   pl.BlockSpec((tm, D_MODEL), lambda i: (i, 0))),
        compiler_params=pltpu.CompilerParams(dimension_semantics=("parallel",),
                                             vmem_limit_bytes=VMEM_LIMIT),
        name="out_proj",
    )(*mixed, w, x, g, nw, sc, sh)


def _mlp_kernel(h_ref, w1_ref, w2_ref, x_ref, g_ref, nw_ref, sc_ref, sh_ref, o_ref, hn_ref, acc_ref):
    j = pl.program_id(1)

    @pl.when(j == 0)
    def _():
        acc_ref[...] = jnp.zeros_like(acc_ref)

    u = jnp.dot(h_ref[...], w1_ref[...], preferred_element_type=_f32)
    u = jnp.square(jnp.maximum(u, 0.0)).astype(_bf16)
    acc_ref[...] += jnp.dot(u, w2_ref[...], preferred_element_type=_f32)

    @pl.when(j == pl.num_programs(1) - 1)
    def _():
        xn = x_ref[...] + g_ref[0] * acc_ref[...]
        o_ref[...] = xn
        hn_ref[...] = _rms_mod(xn, nw_ref[...], sc_ref[0], sh_ref[0]).astype(hn_ref.dtype)


def _mlp(h, w1, w2, x, g, nw, sc, sh, next_dtype, tm=512, th=1024):
    per_b = SEQ // tm
    bspec = pl.BlockSpec((1, 1, D_MODEL), lambda i, j: (i // per_b, 0, 0))
    xspec = pl.BlockSpec((tm, D_MODEL), lambda i, j: (i, 0))
    return pl.pallas_call(
        _mlp_kernel,
        out_shape=(jax.ShapeDtypeStruct((TOKENS, D_MODEL), _f32),
                   jax.ShapeDtypeStruct((TOKENS, D_MODEL), next_dtype)),
        grid=(TOKENS // tm, MLP_HIDDEN // th),
        in_specs=[xspec,
                  pl.BlockSpec((D_MODEL, th), lambda i, j: (0, j)),
                  pl.BlockSpec((th, D_MODEL), lambda i, j: (j, 0)),
                  xspec, bspec,
                  pl.BlockSpec((1, D_MODEL), lambda i, j: (0, 0)),
                  bspec, bspec],
        out_specs=(xspec, xspec),
        scratch_shapes=[pltpu.VMEM((tm, D_MODEL), _f32)],
        compiler_params=pltpu.CompilerParams(dimension_semantics=("parallel", "arbitrary"),
                                             vmem_limit_bytes=VMEM_LIMIT),
        name="mlp",
    )(h, w1, w2, x, g, nw, sc, sh)


NSA_R = NSA_HEADS // NSA_KV_GROUPS
NSA_CMP_ROWS = SEQ // CMP_STRIDE
NSA_NC = NSA_CMP_ROWS - CMP_BLOCK // CMP_STRIDE + 1
NSA_NSB = SEQ // SLC_BLOCK
NSA_NQB = SEQ // Q_BLOCK
assert NSA_CMP_ROWS == LANE and Q_BLOCK == LANE and LANE % NSA_NSB == 0 and SLC_TOP_N <= NSA_NSB
assert NSA_KV_GROUPS * NSA_HEAD_DIM == LANE and CMP_BLOCK == 2 * CMP_STRIDE


def _bucket_value(tab_ref, h, rel):
    exact = REL_BUCKETS // 2
    n = jnp.maximum(rel, 0)
    large = exact + (jnp.log(jnp.maximum(n, 1).astype(_f32) / exact)
                     / math.log(REL_MAX_DIST / exact) * (REL_BUCKETS - exact)).astype(jnp.int32)
    bucket = jnp.where(n < exact, n, jnp.minimum(large, REL_BUCKETS - 1))
    val = jnp.full(rel.shape, tab_ref[0, h], _f32)
    for b in range(1, REL_BUCKETS):
        val = jnp.where(bucket == b, tab_ref[b, h], val)
    return val


def _nsa_bias_kernel(tab_ref, tb_ref, cb_ref):
    h = pl.program_id(0)
    kl = lax.broadcasted_iota(jnp.int32, (Q_BLOCK, LANE), 0)
    ql = lax.broadcasted_iota(jnp.int32, (Q_BLOCK, LANE), 1)
    for d in range(3):
        tb_ref[d, 0] = _bucket_value(tab_ref, h, ql - kl + d * Q_BLOCK)
    cmp_end = kl * CMP_STRIDE + (CMP_BLOCK - 1)
    for qb in range(NSA_NQB):
        cb_ref[qb, 0] = _bucket_value(tab_ref, h, qb * Q_BLOCK + ql - cmp_end)


def _nsa_bias_tiles(rel_bias):
    assert 2 * Q_BLOCK >= REL_MAX_DIST
    return pl.pallas_call(
        _nsa_bias_kernel,
        out_shape=(jax.ShapeDtypeStruct((3, NSA_HEADS, Q_BLOCK, LANE), _f32),
                   jax.ShapeDtypeStruct((NSA_NQB, NSA_HEADS, Q_BLOCK, LANE), _f32)),
        grid=(NSA_HEADS,),
        in_specs=[pl.BlockSpec(memory_space=pltpu.SMEM)],
        out_specs=(pl.BlockSpec((3, 1, Q_BLOCK, LANE), lambda h: (0, h, 0, 0)),
                   pl.BlockSpec((NSA_NQB, 1, Q_BLOCK, LANE), lambda h: (0, h, 0, 0))),
        compiler_params=pltpu.CompilerParams(dimension_semantics=("parallel",),
                                             vmem_limit_bytes=VMEM_LIMIT),
        name="nsa_bias_tiles",
    )(rel_bias)


def _nsa_cmp_kernel(tk_ref, tv_ref, pos_ref, w1_ref, w2_ref, kc_ref, vc_ref):
    rows = lax.broadcasted_iota(jnp.int32, (NSA_CMP_ROWS, LANE), 0)
    for idx, (t_ref, o_ref) in enumerate(((tk_ref, kc_ref), (tv_ref, vc_ref))):
        u = v = None
        for i in range(CMP_STRIDE):
            t_i = t_ref[pl.ds(i, NSA_CMP_ROWS, stride=CMP_STRIDE), :]
            ui = jnp.dot((t_i + pos_ref[idx, 0, i]).astype(_bf16), w1_ref[idx, 0, i], preferred_element_type=_f32)
            vi = jnp.dot((t_i + pos_ref[idx, 1, i]).astype(_bf16), w1_ref[idx, 1, i], preferred_element_type=_f32)
            u = ui if u is None else u + ui
            v = vi if v is None else v + vi
        pre = u + pltpu.roll(v, NSA_CMP_ROWS - 1, 0)
        act = pre * jax.nn.sigmoid(pre)
        out = jnp.dot(act.astype(_bf16), w2_ref[idx], preferred_element_type=_f32)
        o_ref[0] = jnp.where(rows < NSA_NC, out, 0.0).astype(_bf16)


def _nsa_compress(proj_b, pos_x, w1_x, w2_x):
    B = proj_b.shape[0] // SEQ
    ospec = pl.BlockSpec((1, NSA_CMP_ROWS, LANE), lambda b: (b, 0, 0))
    return pl.pallas_call(
        _nsa_cmp_kernel,
        out_shape=(jax.ShapeDtypeStruct((B, NSA_CMP_ROWS, LANE), _bf16),) * 2,
        grid=(B,),
        in_specs=[pl.BlockSpec((SEQ, LANE), lambda b: (b, PB['nkc'] // LANE)),
                  pl.BlockSpec((SEQ, LANE), lambda b: (b, PB['nvc'] // LANE)),
                  pl.BlockSpec(pos_x.shape, lambda b: (0,) * pos_x.ndim),
                  pl.BlockSpec(w1_x.shape, lambda b: (0,) * w1_x.ndim),
                  pl.BlockSpec((2, LANE, LANE), lambda b: (0, 0, 0))],
        out_specs=(ospec, ospec),
        compiler_params=pltpu.CompilerParams(dimension_semantics=("parallel",),
                                             vmem_limit_bytes=VMEM_LIMIT),
        name="nsa_compress",
    )(proj_b, proj_b, pos_x, w1_x, w2_x)


def _dot_nt(a, b):
    return lax.dot_general(a, b, (((1,), (1,)), ((), ())), preferred_element_type=_f32)


def _nsa_kernel(q_ref, ks_ref, vs_ref, kw_ref, vw_ref, kc_ref, vc_ref, gate_ref, tb_ref, cb_ref, ov_ref,
                o_ref, m_ref, l_ref, acc_ref, sel_ref):
    R, QB, hd = NSA_R, Q_BLOCK, NSA_HEAD_DIM
    qb = pl.program_id(1)
    kl = lax.broadcasted_iota(jnp.int32, (QB, LANE), 0)
    t_q = qb * QB + lax.broadcasted_iota(jnp.int32, (QB, LANE), 1)
    gates_t = jax.nn.sigmoid(gate_ref[...]).T
    heads = [None] * NSA_HEADS

    def split_r(x):
        return [x[:, r * LANE:(r + 1) * LANE] for r in range(R)]

    G = NSA_KV_GROUPS
    KB = 2 * QB
    qps = [jnp.concatenate([q_ref[:, (R * g + r) * LANE:(R * g + r + 1) * LANE] for r in range(R)], axis=0)
           for g in range(G)]

    krow = lax.broadcasted_iota(jnp.int32, (KB, LANE), 0)
    t_q2 = qb * QB + lax.broadcasted_iota(jnp.int32, (KB, LANE), 1)
    SEL, WIN = 0, 1

    def flash_step(work):
        items, scores = [], []
        for p, br in work:
            off = pl.multiple_of(p * KB, KB)
            rel = t_q2 - (p * KB + krow)
            tidx = [jnp.clip(qb - (KB // QB) * p - i, 0, 2) for i in range(KB // QB)]
            k_blk = (ks_ref, kw_ref)[br][pl.ds(off, KB), :]
            for g in range(G):
                items.append((p, br, g, off, rel, tidx))
                scores.append(split_r(_dot_nt(k_blk, qps[g])))
        probs, alphas = [], []
        for (p, br, g, off, rel, tidx), s_t in zip(items, scores):
            ch = br * G + g
            if br == SEL:
                blocks = [jnp.broadcast_to(sel_ref[g, pl.ds((KB // SLC_BLOCK) * p + i, 1), :], (SLC_BLOCK, LANE))
                          for i in range(KB // SLC_BLOCK)]
                mask = (rel >= 0) & (jnp.concatenate(blocks, axis=0) > 0.5)
            else:
                mask = (rel >= 0) & (rel < WINDOW)
            s_t = jnp.concatenate(
                [jnp.where(mask, s_t[r] + jnp.concatenate([tb_ref[ti, R * g + r] for ti in tidx], axis=0), NEG_INF)
                 for r in range(R)], axis=1)
            m_old = m_ref[ch]
            m_new = jnp.maximum(m_old, jnp.max(s_t, axis=0, keepdims=True))
            e = jnp.exp(s_t - m_new)
            alpha = jnp.exp(m_old - m_new)
            l_ref[ch] = alpha * l_ref[ch] + jnp.sum(e, axis=0, keepdims=True)
            m_ref[ch] = m_new
            probs.append(e.astype(_bf16))
            alphas.append(alpha)
        for (p, br, g, off, rel, tidx), e, alpha in zip(items, probs, alphas):
            ch = br * G + g
            pv = _dot_tn((vs_ref, vw_ref)[br][pl.ds(off, KB), :], e)
            acc_ref[ch] = alpha * acc_ref[ch] + pv[g * hd:(g + 1) * hd]

    o_cmp = []
    for g in range(G):
        qp = qps[g]
        s_t = split_r(_dot_nt(kc_ref[0], qp))
        mask_c = (t_q - (kl * CMP_STRIDE + CMP_BLOCK - 1) >= 0) & (kl < NSA_NC)
        s_t = jnp.concatenate([jnp.where(mask_c, s_t[r] + cb_ref[0, R * g + r], NEG_INF) for r in range(R)], axis=1)
        e = split_r(jnp.exp(s_t - jnp.max(s_t, axis=0, keepdims=True)))
        e = jnp.concatenate([jnp.where(mask_c, e[r], 0.0) for r in range(R)], axis=1)
        den = jnp.sum(e, axis=0, keepdims=True)
        p = (e / jnp.where(den > 0.0, den, 1.0)).astype(_bf16)
        o_cmp.append(_dot_tn(vc_ref[0], p)[g * hd:(g + 1) * hd])

        imp = sum(split_r(jnp.dot(ov_ref[...], p, preferred_element_type=_f32)))
        j = lax.broadcasted_iota(jnp.int32, (NSA_NSB, LANE), 0)
        cur = (qb * QB + lax.broadcasted_iota(jnp.int32, (NSA_NSB, LANE), 1)) // SLC_BLOCK
        imp = jnp.where((j == 0) | (j == cur) | (j == cur - 1), FORCE_SCORE, imp)
        imp = jnp.where(j <= cur, imp, NEG_INF)
        cnt = jnp.zeros((NSA_NSB, LANE), _f32)
        for jo in range(NSA_NSB):
            other = imp[jo:jo + 1, :]
            beats = (other > imp) | ((other == imp) & (j > jo))
            cnt = cnt + jnp.where(beats, 1.0, 0.0)
        sel_ref[g] = jnp.where(cnt < SLC_TOP_N, 1.0, 0.0)

    p_hi = qb // (KB // QB) + 1
    n_win = p_hi - jnp.maximum(qb - WINDOW // QB, 0) // (KB // QB)
    m_ref[...] = jnp.full(m_ref.shape, NEG_INF, _f32)
    l_ref[...] = jnp.zeros(l_ref.shape, _f32)
    acc_ref[...] = jnp.zeros(acc_ref.shape, _f32)

    def both(i, carry):
        p = p_hi - 1 - i
        flash_step([(p, SEL), (p, WIN)])
        return carry

    def selected_pair(i, carry):
        p = p_hi - 1 - n_win - 2 * i
        flash_step([(p, SEL), (p - 1, SEL)])
        return carry

    def selected_one(i, carry):
        flash_step([(0, SEL)])
        return carry

    n_sel = p_hi - n_win
    lax.fori_loop(0, n_win, both, 0)
    lax.fori_loop(0, n_sel // 2, selected_pair, 0)
    lax.fori_loop(0, n_sel % 2, selected_one, 0)
    o_sel = [acc_ref[SEL * G + g] / l_ref[SEL * G + g] for g in range(G)]
    o_win = [acc_ref[WIN * G + g] / l_ref[WIN * G + g] for g in range(G)]

    for g in range(G):
        o_c, o_s, o_w = split_r(o_cmp[g]), split_r(o_sel[g]), split_r(o_win[g])
        for r in range(R):
            h = R * g + r
            heads[h] = (gates_t[3 * h:3 * h + 1] * o_c[r] + gates_t[3 * h + 1:3 * h + 2] * o_s[r]
                        + gates_t[3 * h + 2:3 * h + 3] * o_w[r])

    for pk in range(NSA_HEADS * hd // LANE):
        per = LANE // hd
        pair = jnp.concatenate(heads[per * pk:per * (pk + 1)], axis=0)
        o_ref[:, pk * LANE:(pk + 1) * LANE] = pair.T.astype(o_ref.dtype)


def _nsa_attention(proj_a, proj_b, kc, vc, tb, cb, overlap_t):
    B = kc.shape[0]
    nqb = NSA_NQB
    qcols = NSA_HEADS * LANE
    kv = lambda blk: pl.BlockSpec((SEQ, LANE), lambda b, q: (b, qcols // LANE + blk))
    cspec = pl.BlockSpec((1, NSA_CMP_ROWS, LANE), lambda b, q: (b, 0, 0))
    return pl.pallas_call(
        _nsa_kernel,
        out_shape=jax.ShapeDtypeStruct((B * SEQ, NSA_HEADS * NSA_HEAD_DIM), _bf16),
        grid=(B, nqb),
        in_specs=[pl.BlockSpec((Q_BLOCK, qcols), lambda b, q: (b * nqb + q, 0)),
                  kv(0), kv(1), kv(2), kv(3), cspec, cspec,
                  pl.BlockSpec((Q_BLOCK, LANE), lambda b, q: (b * nqb + q, PB['misc'] // LANE)),
                  pl.BlockSpec((3, NSA_HEADS, Q_BLOCK, LANE), lambda b, q: (0, 0, 0, 0)),
                  pl.BlockSpec((1, NSA_HEADS, Q_BLOCK, LANE), lambda b, q: (q, 0, 0, 0)),
                  pl.BlockSpec((NSA_NSB, LANE), lambda b, q: (0, 0))],
        out_specs=pl.BlockSpec((Q_BLOCK, NSA_HEADS * NSA_HEAD_DIM), lambda b, q: (b * nqb + q, 0)),
        scratch_shapes=[pltpu.VMEM((2 * NSA_KV_GROUPS, 1, NSA_R * Q_BLOCK), _f32),
                        pltpu.VMEM((2 * NSA_KV_GROUPS, 1, NSA_R * Q_BLOCK), _f32),
                        pltpu.VMEM((2 * NSA_KV_GROUPS, NSA_HEAD_DIM, NSA_R * Q_BLOCK), _f32),
                        pltpu.VMEM((NSA_KV_GROUPS, NSA_NSB, Q_BLOCK), _f32)],
        compiler_params=pltpu.CompilerParams(dimension_semantics=("parallel", "arbitrary"),
                                             vmem_limit_bytes=VMEM_LIMIT),
        name="nsa_attention",
    )(proj_a, proj_a, proj_a, proj_a, proj_a, kc, vc, proj_b, tb, cb, overlap_t)


def _nsa_overlap():
    n = jnp.arange(NSA_CMP_ROWS)[None, :]
    jj = jnp.arange(NSA_NSB)[:, None]
    return ((n * CMP_STRIDE <= jj * SLC_BLOCK + SLC_BLOCK - 1)
            & (n * CMP_STRIDE + CMP_BLOCK - 1 >= jj * SLC_BLOCK) & (n < NSA_NC)).astype(_bf16)


def _nsa_weight_prep(w_in, cmp_pos, cmp_w1, cmp_w2):
    G, hd = NSA_KV_GROUPS, NSA_HEAD_DIM
    eye = jnp.eye(G, dtype=_f32)
    nq = NSA_HEADS * hd
    wq = w_in[:, :, :nq].reshape(DEPTH, D_MODEL, NSA_HEADS, 1, hd) * (hd ** -0.5)
    head_group = (jnp.arange(NSA_HEADS)[:, None] // NSA_R == jnp.arange(G)[None, :]).astype(_f32)
    wq = (wq * head_group[None, None, :, :, None]).reshape(DEPTH, D_MODEL, NSA_HEADS * LANE)
    slc_win = w_in[:, :, nq + 2 * NSA_KV:nq + 6 * NSA_KV]
    w_a = jnp.concatenate([wq, slc_win], axis=-1).astype(_bf16)
    w_b = jnp.concatenate([w_in[:, :, _IN_OFF[n]:_IN_OFF[n] + w] for n, w in _PB_SEGMENTS], axis=-1)
    w_b = jnp.pad(w_b, ((0, 0), (0, 0), (0, PROJ_B_COLS - PROJ_B_USED))).astype(_bf16)
    half = CMP_BLOCK // 2
    w1 = cmp_w1.reshape(DEPTH, 2, CMP_BLOCK, hd, hd)
    w1_x = jnp.einsum('lxide,gh->lxigdhe', w1, eye).reshape(DEPTH, 2, 2, half, G * hd, G * hd).astype(_bf16)
    w2_x = jnp.einsum('lxde,gh->lxgdhe', cmp_w2, eye).reshape(DEPTH, 2, G * hd, G * hd).astype(_bf16)
    pos_x = jnp.broadcast_to(cmp_pos[:, :, :, None, :], (DEPTH, 2, CMP_BLOCK, G, hd))
    pos_x = pos_x.reshape(DEPTH, 2, 2, half, 1, G * hd)
    return w_a, w_b, w1_x, w2_x, pos_x


REC_BLOCK = 2 * CHUNK
REC_SEQS_PER_STEP = 2
assert REC_BLOCK == LANE and SEQ % REC_BLOCK == 0 and BATCH % REC_SEQS_PER_STEP == 0


def _split3(x):
    hi = x.astype(_bf16)
    r = x - hi.astype(_f32)
    mid = r.astype(_bf16)
    lo = (r - mid.astype(_f32)).astype(_bf16)
    return hi, mid, lo


def _dot_exact_rhs(m, x):
    return sum(jnp.dot(m, p, preferred_element_type=_f32) for p in _split3(x))


def _dot_exact_lhs(x, m):
    return sum(jnp.dot(p, m, preferred_element_type=_f32) for p in _split3(x))


def _dot_tn(a, b):
    return lax.dot_general(a, b, (((0,), (0,)), ((), ())), preferred_element_type=_f32)


def _mm3(x, y):
    xh = x.astype(_bf16)
    xl = (x - xh.astype(_f32)).astype(_bf16)
    yh = y.astype(_bf16)
    yl = (y - yh.astype(_f32)).astype(_bf16)
    return (jnp.dot(xh, yh, preferred_element_type=_f32) + jnp.dot(xh, yl, preferred_element_type=_f32)
            + jnp.dot(xl, yh, preferred_element_type=_f32))


def _chunk_masks():
    ri = lax.broadcasted_iota(jnp.int32, (REC_BLOCK, REC_BLOCK), 0)
    ci = lax.broadcasted_iota(jnp.int32, (REC_BLOCK, REC_BLOCK), 1)
    same = (ri // CHUNK) == (ci // CHUNK)
    return ri, ci, (ci <= ri) & same, (ci < ri) & same, (ri <= ci) & same


def _as_mxu(mask):
    return jnp.where(mask, 1.0, 0.0).astype(_bf16)


def _softplus(x):
    return jnp.maximum(x, 0.0) + jnp.log1p(jnp.exp(-jnp.abs(x)))


def _silu(x):
    return x * jax.nn.sigmoid(x)


def _chunk_last(x):
    ri = lax.broadcasted_iota(jnp.int32, x.shape, 0)
    return jnp.where(ri < CHUNK, x[CHUNK - 1:CHUNK], x[2 * CHUNK - 1:2 * CHUNK])


def _expand_heads(x, lane0, nheads, width):
    per = LANE // width
    lane = lax.broadcasted_iota(jnp.int32, (x.shape[0], LANE), 1)
    pieces = []
    for p0 in range(lane0, lane0 + nheads, per):
        piece = jnp.broadcast_to(x[:, p0:p0 + 1], (x.shape[0], LANE))
        for k in range(1, per):
            piece = jnp.where(lane < k * width, piece, jnp.broadcast_to(x[:, p0 + k:p0 + k + 1], (x.shape[0], LANE)))
        pieces.append(piece)
    return jnp.concatenate(pieces, axis=1) if len(pieces) > 1 else pieces[0]


def _conv_silu(x, prev, w_ref, c0, bias=None):
    n, C = x.shape
    ntap = w_ref.shape[0]
    rows = lax.broadcasted_iota(jnp.int32, (n, C), 0)
    acc = x * w_ref[ntap - 1:ntap, c0:c0 + C]
    for s in range(1, ntap):
        xs = jnp.where(rows < s, pltpu.roll(prev, s, 0), pltpu.roll(x, s, 0))
        acc = acc + xs * w_ref[ntap - 1 - s:ntap - s, c0:c0 + C]
    if bias is not None:
        acc = acc + bias
    return _silu(acc)


def _rms(x, w):
    return x * lax.rsqrt(jnp.mean(x * x, axis=-1, keepdims=True) + EPS) * w


def _ssd_kernel(z_ref, xbc_ref, dt_ref, cw_ref, cb_ref, dtb_ref, alog_ref, dskip_ref, nw_ref, o_ref,
                prev_ref, state_ref):
    @pl.when(pl.program_id(1) == 0)
    def _():
        prev_ref[...] = jnp.zeros(prev_ref.shape, _f32)
        state_ref[...] = jnp.zeros(state_ref.shape, _f32)

    for ns in range(z_ref.shape[0]):
        _ssd_block(z_ref.at[ns], xbc_ref.at[ns], dt_ref.at[ns], cw_ref, cb_ref, dtb_ref, alog_ref, dskip_ref, nw_ref,
                   o_ref.at[ns], prev_ref.at[ns], state_ref.at[ns])


def _ssd_block(z_ref, xbc_ref, dt_ref, cw_ref, cb_ref, dtb_ref, alog_ref, dskip_ref, nw_ref, o_ref,
               prev_ref, state_ref):
    G, R, P, N = SSD_GROUPS, SSD_HEADS // SSD_GROUPS, SSD_HEAD_DIM, SSD_STATE
    x_in = xbc_ref[...]
    xc = _conv_silu(x_in, prev_ref[...], cw_ref, 0, cb_ref[...])
    prev_ref[...] = x_in
    _, _, tril, _, triu = _chunk_masks()
    lane = lax.broadcasted_iota(jnp.int32, (REC_BLOCK, LANE), 1)
    L0 = PBL['sdt']
    dt = _softplus(dt_ref[...] + dtb_ref[...])
    da = dt * (-jnp.exp(alog_ref[...]))
    a_cum = _dot_exact_rhs(_as_mxu(tril), da)
    a_cum_t = _dot_exact_lhs(da.T, _as_mxu(triu))
    a_last = _chunk_last(a_cum)
    xs = xc[:, :SSD_INNER]
    xdt = xs * _expand_heads(dt, L0, SSD_HEADS, P)
    xdtd = (xdt * _expand_heads(jnp.exp(a_last - a_cum), L0, SSD_HEADS, P)).astype(_bf16)
    xdt_b = xdt.astype(_bf16)
    ea = _expand_heads(jnp.exp(a_cum), L0, SSD_HEADS, P)
    y_groups = []
    for g in range(G):
        bm = xc[:, SSD_INNER + g * N:SSD_INNER + (g + 1) * N].astype(_bf16)
        cm = xc[:, SSD_INNER + (G + g) * N:SSD_INNER + (G + g + 1) * N].astype(_bf16)
        cbm = _dot_nt(cm, bm)
        intra = []
        for pr in range(R // 2):
            both = []
            for k in range(2):
                h = g * R + 2 * pr + k
                seg = jnp.exp(jnp.where(tril, a_cum[:, L0 + h:L0 + h + 1] - a_cum_t[L0 + h:L0 + h + 1, :], -jnp.inf))
                both.append(jnp.dot((cbm * seg).astype(_bf16), xdt_b[:, (h - k) * P:(h - k + 2) * P],
                                    preferred_element_type=_f32))
            intra.append(jnp.where(lane < P, both[0], both[1]))
        y_intra = jnp.concatenate(intra, axis=1)
        prev_rows = []
        for c in range(REC_BLOCK // CHUNK):
            rows = slice(c * CHUNK, (c + 1) * CHUNK)
            st = state_ref[g]
            prev_rows.append(jnp.dot(cm[rows], st.astype(_bf16), preferred_element_type=_f32))
            dec = _expand_heads(jnp.exp(a_cum[(c + 1) * CHUNK - 1:(c + 1) * CHUNK]), L0, SSD_HEADS, P)
            state_ref[g] = (st * dec[:, g * R * P:(g + 1) * R * P]
                            + _dot_tn(bm[rows], xdtd[rows, g * R * P:(g + 1) * R * P]))
        y_groups.append(y_intra + jnp.concatenate(prev_rows, axis=0) * ea[:, g * R * P:(g + 1) * R * P])
    y = jnp.concatenate(y_groups, axis=1) + xs * dskip_ref[...]
    y = y * _silu(z_ref[...])
    gw = SSD_INNER // G
    for g in range(G):
        o_ref[:, g * gw:(g + 1) * gw] = _rms(y[:, g * gw:(g + 1) * gw], nw_ref[:, g * gw:(g + 1) * gw]).astype(o_ref.dtype)


def _gdn_kernel(q_ref, k_ref, v_ref, z_ref, ba_ref, cw_ref, dtb_ref, alog_ref, nw_ref, o_ref,
                pq_ref, pk_ref, pv_ref, state_ref):
    H, Dh = GDN_HEADS, GDN_HEAD_DIM

    @pl.when(pl.program_id(1) == 0)
    def _():
        for r in (pq_ref, pk_ref, pv_ref, state_ref):
            r[...] = jnp.zeros(r.shape, _f32)

    NS = q_ref.shape[0]
    ri, ci, tril, strict, triu = _chunk_masks()
    q, k, v, beta, gcum, gcum_t, glast = [], [], [], [], [], [], []
    for ns in range(NS):
        for dst, x_ref, p_ref, c0 in ((q, q_ref, pq_ref, 0), (k, k_ref, pk_ref, GDN_WIDTH), (v, v_ref, pv_ref, 2 * GDN_WIDTH)):
            x_in = x_ref[ns]
            dst.append(_conv_silu(x_in, p_ref[ns], cw_ref, c0))
            p_ref[ns] = x_in
        ba = ba_ref[ns]
        beta.append(jax.nn.sigmoid(ba))
        gl = -jnp.exp(alog_ref[...]) * _softplus(ba + dtb_ref[...])
        gcum.append(_dot_exact_rhs(_as_mxu(tril), gl))
        gcum_t.append(_dot_exact_lhs(gl.T, _as_mxu(triu)))
        glast.append(_chunk_last(gcum[ns]))
    zero_rows = jnp.zeros((CHUNK, Dh), _f32)
    eye = jnp.where(tril & jnp.logical_not(strict), 1.0, 0.0)
    hs = range(NS * H)
    seq = [i // H for i in hs]
    sls = [slice((i % H) * Dh, (i % H + 1) * Dh) for i in hs]
    qh = [q[seq[i]][:, sls[i]] for i in hs]
    kh = [k[seq[i]][:, sls[i]] for i in hs]
    qq = [qh[i] * lax.rsqrt(jnp.sum(qh[i] * qh[i], axis=-1, keepdims=True) + EPS) * (Dh ** -0.5) for i in hs]
    kk = [kh[i] * lax.rsqrt(jnp.sum(kh[i] * kh[i], axis=-1, keepdims=True) + EPS) for i in hs]
    lb = [PBL['gbeta'] + i % H for i in hs]
    lg = [PBL['gbeta'] + H + i % H for i in hs]
    bcol = [beta[seq[h]][:, lb[h]:lb[h] + 1] for h in hs]
    gcol = [gcum[seq[h]][:, lg[h]:lg[h] + 1] for h in hs]
    decay = [jnp.exp(jnp.where(tril, gcol[h] - gcum_t[seq[h]][lg[h]:lg[h] + 1, :], -jnp.inf)) for h in hs]
    kb = [kk[h] * bcol[h] for h in hs]
    s = [_dot_nt(jnp.concatenate([kb[h], qq[h]], axis=0).astype(_bf16), kk[h].astype(_bf16)) for h in hs]
    a_mat = [jnp.where(strict, s[h][:REC_BLOCK] * decay[h], 0.0) for h in hs]
    aqk = [(s[h][REC_BLOCK:] * decay[h]).astype(_bf16) for h in hs]
    SUB = 8
    same = lambda n: (ri // n) == (ci // n)
    a_sub = [jnp.where(same(SUB), a_mat[h], 0.0) for h in hs]
    tinv = [eye - a_sub[h] for h in hs]
    pw = a_sub
    for _ in range(SUB.bit_length() - 2):
        pw = [_mm3(pw[h], pw[h]) for h in hs]
        tinv = [tinv[h] + _mm3(tinv[h], pw[h]) for h in hs]
    n = SUB
    while n < CHUNK:
        enclosed = same(2 * n) & jnp.logical_not(same(n))
        tc = [_mm3(tinv[h], jnp.where(enclosed, a_mat[h], 0.0)) for h in hs]
        tinv = [tinv[h] - _mm3(tc[h], tinv[h]) for h in hs]
        n *= 2
    sol = [_mm3(tinv[h], jnp.concatenate([v[seq[h]][:, sls[h]] * bcol[h], kb[h] * jnp.exp(gcol[h])], axis=1))
           for h in hs]
    u = [sol[h][:, :Dh] for h in hs]
    w = [sol[h][:, Dh:].astype(_bf16) for h in hs]
    q_dec = [(qq[h] * jnp.exp(gcol[h])).astype(_bf16) for h in hs]
    k_end = [(kk[h] * jnp.exp(glast[seq[h]][:, lg[h]:lg[h] + 1] - gcol[h])).astype(_bf16) for h in hs]
    o_rows = [[] for _ in hs]
    for c in range(REC_BLOCK // CHUNK):
        rows = slice(c * CHUNK, (c + 1) * CHUNK)
        st = [state_ref[h] for h in hs]
        st_b = [st[h].astype(_bf16) for h in hs]
        v_new = [u[h][rows] - jnp.dot(w[h][rows], st_b[h], preferred_element_type=_f32) for h in hs]
        v_full = [jnp.concatenate([v_new[h], zero_rows] if c == 0 else [zero_rows, v_new[h]], axis=0).astype(_bf16)
                  for h in hs]
        for h in hs:
            o_rows[h].append(jnp.dot(q_dec[h][rows], st_b[h], preferred_element_type=_f32)
                             + jnp.dot(aqk[h][rows], v_full[h], preferred_element_type=_f32))
            d_last = jnp.exp(gcum[seq[h]][(c + 1) * CHUNK - 1:(c + 1) * CHUNK, lg[h]:lg[h] + 1])
            state_ref[h] = st[h] * d_last + _dot_tn(k_end[h][rows], v_new[h].astype(_bf16))
    for h in hs:
        o = _rms(jnp.concatenate(o_rows[h], axis=0), nw_ref[...]) * _silu(z_ref[seq[h], :, sls[h]])
        o_ref[seq[h], :, sls[h]] = o.astype(o_ref.dtype)


def _gla_kernel(q_ref, k_ref, v_ref, go_ref, lr_ref, w2_ref, gb_ref, nw_ref, o_ref, state_ref):
    @pl.when(pl.program_id(1) == 0)
    def _():
        state_ref[...] = jnp.zeros(state_ref.shape, _f32)

    for ns in range(q_ref.shape[0]):
        _gla_block(q_ref.at[ns], k_ref.at[ns], v_ref.at[ns], go_ref.at[ns], lr_ref.at[ns], w2_ref, gb_ref, nw_ref,
                   o_ref.at[ns], state_ref.at[ns])


def _gla_block(q_ref, k_ref, v_ref, go_ref, lr_ref, w2_ref, gb_ref, nw_ref, o_ref, state_ref):
    H, Dk, Dv = GLA_HEADS, GLA_DK, GLA_DV
    _, _, tril, _, _ = _chunk_masks()
    lane = lax.broadcasted_iota(jnp.int32, (REC_BLOCK, LANE), 1)
    pre = jnp.dot(lr_ref[...].astype(_bf16), w2_ref[...], preferred_element_type=_f32) + gb_ref[...]
    gk = (jnp.minimum(pre, 0.0) - jnp.log1p(jnp.exp(-jnp.abs(pre)))) / GLA_GATE_NORM
    bcum = _dot_exact_rhs(_as_mxu(tril), gk)
    blast = _chunk_last(bcum)
    q_dec = q_ref[...] * (Dk ** -0.5) * jnp.exp(bcum)
    k_inv = (k_ref[...] * jnp.exp(-bcum)).astype(_bf16)
    k_end = (k_ref[...] * jnp.exp(blast - bcum)).astype(_bf16)
    per = LANE // Dk
    nchunk = REC_BLOCK // CHUNK
    chunk_rows = [slice(c * CHUNK, (c + 1) * CHUNK) for c in range(nchunk)]
    psl = [slice(pr * LANE, (pr + 1) * LANE) for pr in range(H // per)]
    qm = [jnp.where(lane // Dk == h % per, q_dec[:, psl[h // per]], 0.0).astype(_bf16) for h in range(H)]
    vh = [v_ref[:, h * Dv:(h + 1) * Dv].astype(_bf16) for h in range(H)]
    attn = [jnp.where(tril, _dot_nt(qm[h], k_inv[:, psl[h // per]]), 0.0).astype(_bf16) for h in range(H)]
    o_intra = [jnp.dot(attn[h], vh[h], preferred_element_type=_f32) for h in range(H)]
    local = []
    for rows in chunk_rows:
        per_pair = []
        for pr in range(H // per):
            loc = None
            for k in range(per):
                lk = _dot_tn(vh[pr * per + k][rows], k_end[rows, psl[pr]])
                loc = lk if loc is None else jnp.where(lane < k * Dk, loc, lk)
            per_pair.append(loc)
        local.append(per_pair)
    o_prev = [[] for _ in range(H)]
    for c, rows in enumerate(chunk_rows):
        for pr in range(H // per):
            st = state_ref[pr]
            st_b = st.astype(_bf16)
            for k in range(per):
                o_prev[pr * per + k].append(_dot_nt(qm[pr * per + k][rows], st_b))
            state_ref[pr] = st * jnp.exp(bcum[(c + 1) * CHUNK - 1:(c + 1) * CHUNK, psl[pr]]) + local[c][pr]
    for h in range(H):
        o = o_intra[h] + jnp.concatenate(o_prev[h], axis=0)
        o = _rms(o, nw_ref[...]) * _silu(go_ref[:, h * Dv:(h + 1) * Dv])
        o_ref[:, h * Dv:(h + 1) * Dv] = o.astype(o_ref.dtype)


def _rec_call(body, proj_b, col_blocks, params, out_cols, scratch, name, nseq=None):
    B = proj_b.shape[0] // SEQ
    nblk = SEQ // REC_BLOCK
    if nseq is None:
        src, grid = proj_b, (B, nblk)
        in_specs = [pl.BlockSpec((REC_BLOCK, w), (lambda b, t, c=c0 // w: (b * nblk + t, c))) for c0, w in col_blocks]
        out_shape = jax.ShapeDtypeStruct((B * SEQ, out_cols), _bf16)
        out_spec = pl.BlockSpec((REC_BLOCK, out_cols), lambda b, t: (b * nblk + t, 0))
    else:
        src, grid = proj_b.reshape(B, SEQ, proj_b.shape[1]), (B // nseq, nblk)
        in_specs = [pl.BlockSpec((nseq, REC_BLOCK, w), (lambda b, t, c=c0 // w: (b, t, c))) for c0, w in col_blocks]
        out_shape = jax.ShapeDtypeStruct((B, SEQ, out_cols), _bf16)
        out_spec = pl.BlockSpec((nseq, REC_BLOCK, out_cols), lambda b, t: (b, t, 0))
    for p in params:
        in_specs.append(pl.BlockSpec(p.shape, lambda b, t, nd=p.ndim: (0,) * nd))
    out = pl.pallas_call(
        body,
        out_shape=out_shape,
        grid=grid,
        in_specs=in_specs,
        out_specs=out_spec,
        scratch_shapes=scratch,
        compiler_params=pltpu.CompilerParams(dimension_semantics=("parallel", "arbitrary"),
                                             vmem_limit_bytes=VMEM_LIMIT),
        name=name,
    )(*([src] * len(col_blocks)), *params)
    return out.reshape(B * SEQ, out_cols)


def _lane_pad(v, lane0=0):
    return jnp.pad(v.astype(_f32), (lane0, LANE - lane0 - v.shape[0]))[None]


def _ssd_call(proj_b, conv_w, conv_b, dt_bias, a_log, d_skip, norm_w):
    cols = [(PB['sz'], SSD_INNER), (PB['sxbc'], SSD_XBC), (PB['misc'], LANE)]
    params = [conv_w, conv_b[None], _lane_pad(dt_bias, PBL['sdt']), _lane_pad(a_log, PBL['sdt']),
              jnp.repeat(d_skip, SSD_HEAD_DIM)[None], norm_w[None]]
    ns = REC_SEQS_PER_STEP
    scratch = [pltpu.VMEM((ns, REC_BLOCK, SSD_XBC), _f32),
               pltpu.VMEM((ns, SSD_GROUPS, SSD_STATE, SSD_INNER // SSD_GROUPS), _f32)]
    return _rec_call(_ssd_kernel, proj_b, cols, params, SSD_INNER, scratch, "ssd", nseq=ns)


def _gdn_call(proj_b, conv_w, dt_bias, a_log, norm_w):
    W = GDN_WIDTH
    cols = [(PB['gq'], W), (PB['gk'], W), (PB['gv'], W), (PB['gz'], W), (PB['misc'], LANE)]
    decay_lane = PBL['gbeta'] + GDN_HEADS
    params = [conv_w, _lane_pad(dt_bias, decay_lane), _lane_pad(a_log, decay_lane), norm_w[None]]
    ns = REC_SEQS_PER_STEP
    scratch = ([pltpu.VMEM((ns, REC_BLOCK, W), _f32)] * 3
               + [pltpu.VMEM((ns * GDN_HEADS, GDN_HEAD_DIM, GDN_HEAD_DIM), _f32)])
    return _rec_call(_gdn_kernel, proj_b, cols, params, W, scratch, "gdn", nseq=ns)


def _gla_call(proj_b, gate_w2, gate_b, norm_w):
    cols = [(PB['lq'], GLA_KEY), (PB['lk'], GLA_KEY), (PB['lv'], GLA_VAL), (PB['lg'], GLA_VAL), (PB['misc'], LANE)]
    w2 = jnp.pad(gate_w2, ((PBL['llr'], LANE - PBL['llr'] - GLA_GATE_RANK), (0, 0))).astype(_bf16)
    params = [w2, gate_b[None], norm_w[None]]
    ns = REC_SEQS_PER_STEP
    scratch = [pltpu.VMEM((ns, GLA_HEADS * GLA_DK // LANE, GLA_DV, LANE), _f32)]
    return _rec_call(_gla_kernel, proj_b, cols, params, GLA_VAL, scratch, "gla", nseq=ns)


def kernel(x, c, rel_bias, norm1_w, norm2_w, ada_w, ada_b, w_in, w_out, nsa_cmp_pos, nsa_cmp_w1, nsa_cmp_w2, ssd_conv_w, ssd_conv_b, ssd_dt_bias, ssd_a_log, ssd_d, ssd_norm_w, gdn_conv_w, gdn_dt_bias, gdn_a_log, gdn_norm_w, gla_gate_w2, gla_gate_b, gla_norm_w, mlp_w1, mlp_w2, final_norm_w):
    B, S, D = x.shape
    mod = _ada_all(c, ada_w, ada_b).reshape(DEPTH, B, 6, 1, D)
    w_a, w_b, cmp_w1_x, cmp_w2_x, cmp_pos_x = _nsa_weight_prep(w_in, nsa_cmp_pos, nsa_cmp_w1, nsa_cmp_w2)
    w_out_b = w_out.astype(_bf16)
    w1_b = mlp_w1.astype(_bf16)
    w2_b = mlp_w2.astype(_bf16)
    tb, cb = _nsa_bias_tiles(rel_bias)
    overlap_t = _nsa_overlap()
    xf = x.reshape(TOKENS, D)
    no_mod = jnp.zeros((B, 1, D), _f32)
    h = _norm_mod(xf, norm1_w[0][None], mod[0, :, 1], mod[0, :, 0])
    for l in range(DEPTH):
        sh1, sc1, g1, sh2, sc2, g2 = (mod[l, :, i] for i in range(6))
        proj_a = _matmul(h, w_a[l], out_dtype=_bf16)
        proj_b = _matmul(h, w_b[l], tn=PROJ_B_TN)
        kc, vc = _nsa_compress(proj_b, cmp_pos_x[l], cmp_w1_x[l], cmp_w2_x[l])
        y_nsa = _nsa_attention(proj_a, proj_b, kc, vc, tb, cb, overlap_t)
        y_ssd = _ssd_call(proj_b, ssd_conv_w[l], ssd_conv_b[l], ssd_dt_bias[l], ssd_a_log[l], ssd_d[l], ssd_norm_w[l])
        y_gdn = _gdn_call(proj_b, gdn_conv_w[l], gdn_dt_bias[l], gdn_a_log[l], gdn_norm_w[l])
        y_gla = _gla_call(proj_b, gla_gate_w2[l], gla_gate_b[l], gla_norm_w[l])
        xf, h2 = _out_proj((y_nsa, y_ssd, y_gdn, y_gla), w_out_b[l], xf, g1,
                           norm2_w[l][None], sc2, sh2)
        if l + 1 < DEPTH:
            xf, h = _mlp(h2, w1_b[l], w2_b[l], xf, g2, norm1_w[l + 1][None], mod[l + 1, :, 1], mod[l + 1, :, 0], _bf16)
        else:
            _, out = _mlp(h2, w1_b[l], w2_b[l], xf, g2, final_norm_w[None], no_mod, no_mod, _f32)
    return out.reshape(B, S, D)
```

```python
import math
from functools import partial

import jax
import jax.numpy as jnp
from jax import lax
from jax.experimental import pallas as pl
from jax.experimental.pallas import tpu as pltpu

D_MODEL = 2048
BATCH = 16
SEQ = 2048
DEPTH = 4

MIX_GROUP = D_MODEL // 4
NSA_HEAD_DIM = 64
NSA_HEADS = MIX_GROUP // NSA_HEAD_DIM
NSA_KV_GROUPS = max(1, NSA_HEADS // 4)
NSA_KV = NSA_KV_GROUPS * NSA_HEAD_DIM
CMP_BLOCK = 32
CMP_STRIDE = 16
SLC_BLOCK = 64
SLC_TOP_N = 8
WINDOW = 512
Q_BLOCK = 128
REL_BUCKETS = 32
REL_MAX_DIST = 128
SSD_HEAD_DIM = 64
SSD_HEADS = MIX_GROUP // SSD_HEAD_DIM
SSD_INNER = SSD_HEADS * SSD_HEAD_DIM
SSD_GROUPS = 2
SSD_STATE = 128
SSD_CONV = 4
SSD_XBC = SSD_INNER + 2 * SSD_GROUPS * SSD_STATE
GDN_HEAD_DIM = 128
GDN_HEADS = MIX_GROUP // GDN_HEAD_DIM
GDN_WIDTH = GDN_HEADS * GDN_HEAD_DIM
GDN_CONV = 4
GLA_DV = 128
GLA_HEADS = MIX_GROUP // GLA_DV
GLA_DK = GLA_DV // 2
GLA_KEY = GLA_HEADS * GLA_DK
GLA_VAL = GLA_HEADS * GLA_DV
GLA_GATE_RANK = 16
GLA_GATE_NORM = 16.0
CHUNK = 64
MLP_HIDDEN = 4 * D_MODEL
EPS = 1e-6
NEG_INF = -1e30
FORCE_SCORE = 1e9
IN_SPLITS = (NSA_HEADS * NSA_HEAD_DIM, NSA_KV, NSA_KV, NSA_KV, NSA_KV, NSA_KV, NSA_KV, NSA_HEADS * 3,
             SSD_INNER, SSD_XBC, SSD_HEADS,
             GDN_WIDTH, GDN_WIDTH, GDN_WIDTH, GDN_WIDTH, GDN_HEADS, GDN_HEADS,
             GLA_KEY, GLA_KEY, GLA_VAL, GLA_VAL, GLA_GATE_RANK)
IN_COLS = sum(IN_SPLITS)
MIX_OUT = NSA_HEADS * NSA_HEAD_DIM + SSD_INNER + GDN_WIDTH + GLA_VAL

LANE = 128
VMEM_LIMIT = 56 * 1024 * 1024
TOKENS = BATCH * SEQ
_IN_NAMES = ('nq', 'nkc', 'nvc', 'nks', 'nvs', 'nkw', 'nvw', 'ngate', 'sz', 'sxbc', 'sdt',
             'gq', 'gk', 'gv', 'gz', 'gbeta', 'ga', 'lq', 'lk', 'lv', 'lg', 'llr')
_IN_W = dict(zip(_IN_NAMES, IN_SPLITS))
_IN_OFF = {n: sum(IN_SPLITS[:i]) for i, n in enumerate(_IN_NAMES)}
_PB_SEGMENTS = (('sxbc', SSD_XBC), ('gq', GDN_WIDTH), ('gk', GDN_WIDTH), ('gv', GDN_WIDTH), ('gz', GDN_WIDTH),
                ('sz', SSD_INNER), ('lv', GLA_VAL), ('lg', GLA_VAL), ('lq', GLA_KEY), ('lk', GLA_KEY),
                ('nkc', NSA_KV), ('nvc', NSA_KV),
                ('ngate', NSA_HEADS * 3), ('sdt', SSD_HEADS), ('gbeta', 2 * GDN_HEADS), ('llr', GLA_GATE_RANK))
_PB_MISC = ('ngate', 'sdt', 'gbeta', 'llr')
PB, PBL = {}, {}
PROJ_B_USED = 0
for _n, _w in _PB_SEGMENTS:
    if _n in _PB_MISC:
        PB.setdefault('misc', PROJ_B_USED - PROJ_B_USED % LANE)
        PBL[_n] = PROJ_B_USED - PB['misc']
    else:
        assert PROJ_B_USED % _w == 0 and _w % LANE == 0
        PB[_n] = PROJ_B_USED
    PROJ_B_USED += _w
assert PROJ_B_USED - PB['misc'] <= LANE and PBL['ngate'] == 0 and _IN_OFF['ga'] == _IN_OFF['gbeta'] + GDN_HEADS
PROJ_B_TN = 512
PROJ_B_COLS = -(-PROJ_B_USED // PROJ_B_TN) * PROJ_B_TN

_bf16 = jnp.bfloat16
_f32 = jnp.float32


def _ada_kernel(c_ref, w_ref, b_ref, o_ref):
    c = c_ref[...]
    c_act = c * jax.nn.sigmoid(c)
    o_ref[0] = jnp.dot(c_act, w_ref[0], preferred_element_type=_f32) + b_ref[0]


def _ada_all(c, ada_w, ada_b):
    tn = 1024
    return pl.pallas_call(
        _ada_kernel,
        out_shape=jax.ShapeDtypeStruct((DEPTH, BATCH, 6 * D_MODEL), _f32),
        grid=(DEPTH, 6 * D_MODEL // tn),
        in_specs=[pl.BlockSpec((BATCH, D_MODEL), lambda l, j: (0, 0)),
                  pl.BlockSpec((1, D_MODEL, tn), lambda l, j: (l, 0, j)),
                  pl.BlockSpec((1, 1, tn), lambda l, j: (l, 0, j))],
        out_specs=pl.BlockSpec((1, BATCH, tn), lambda l, j: (l, 0, j)),
        compiler_params=pltpu.CompilerParams(dimension_semantics=("parallel", "parallel"),
                                             vmem_limit_bytes=VMEM_LIMIT),
        name="ada_mod",
    )(c, ada_w, ada_b.reshape(DEPTH, 1, 6 * D_MODEL))


def _rms_mod(x, w, sc, sh):
    y = x * lax.rsqrt(jnp.mean(x * x, axis=-1, keepdims=True) + EPS)
    return (y * w) * (1.0 + sc) + sh


def _norm_mod_kernel(x_ref, w_ref, sc_ref, sh_ref, o_ref):
    o_ref[...] = _rms_mod(x_ref[...], w_ref[...], sc_ref[0], sh_ref[0]).astype(o_ref.dtype)


def _norm_mod(x, w, sc, sh, tm=512):
    per_b = SEQ // tm
    return pl.pallas_call(
        _norm_mod_kernel,
        out_shape=jax.ShapeDtypeStruct((TOKENS, D_MODEL), _bf16),
        grid=(TOKENS // tm,),
        in_specs=[pl.BlockSpec((tm, D_MODEL), lambda i: (i, 0)),
                  pl.BlockSpec((1, D_MODEL), lambda i: (0, 0)),
                  pl.BlockSpec((1, 1, D_MODEL), lambda i: (i // per_b, 0, 0)),
                  pl.BlockSpec((1, 1, D_MODEL), lambda i: (i // per_b, 0, 0))],
        out_specs=pl.BlockSpec((tm, D_MODEL), lambda i: (i, 0)),
        compiler_params=pltpu.CompilerParams(dimension_semantics=("parallel",),
                                             vmem_limit_bytes=VMEM_LIMIT),
        name="norm_mod",
    )(x, w, sc, sh)


def _matmul_kernel(a_ref, w_ref, o_ref):
    o_ref[...] = jnp.dot(a_ref[...], w_ref[...], preferred_element_type=_f32).astype(o_ref.dtype)


def _matmul(a, w, tm=1024, tn=512, out_dtype=_f32):
    M, K = a.shape
    N = w.shape[1]
    return pl.pallas_call(
        _matmul_kernel,
        out_shape=jax.ShapeDtypeStruct((M, N), out_dtype),
        grid=(M // tm, N // tn),
        in_specs=[pl.BlockSpec((tm, K), lambda i, j: (i, 0)),
                  pl.BlockSpec((K, tn), lambda i, j: (0, j))],
        out_specs=pl.BlockSpec((tm, tn), lambda i, j: (i, j)),
        compiler_params=pltpu.CompilerParams(dimension_semantics=("parallel", "parallel"),
                                             vmem_limit_bytes=VMEM_LIMIT),
        name="in_proj",
    )(a, w)


def _out_proj_kernel(a0_ref, a1_ref, a2_ref, a3_ref, w_ref, x_ref, g_ref, nw_ref, sc_ref, sh_ref, xo_ref, ho_ref):
    y = None
    for i, a_ref in enumerate((a0_ref, a1_ref, a2_ref, a3_ref)):
        part = jnp.dot(a_ref[...], w_ref[i * MIX_GROUP:(i + 1) * MIX_GROUP, :], preferred_element_type=_f32)
        y = part if y is None else y + part
    xn = x_ref[...] + g_ref[0] * y
    xo_ref[...] = xn
    ho_ref[...] = _rms_mod(xn, nw_ref[...], sc_ref[0], sh_ref[0]).astype(ho_ref.dtype)


def _out_proj(mixed, w, x, g, nw, sc, sh, tm=512):
    per_b = SEQ // tm
    bspec = pl.BlockSpec((1, 1, D_MODEL), lambda i: (i // per_b, 0, 0))
    aspec = pl.BlockSpec((tm, MIX_GROUP), lambda i: (i, 0))
    return pl.pallas_call(
        _out_proj_kernel,
        out_shape=(jax.ShapeDtypeStruct((TOKENS, D_MODEL), _f32),
                   jax.ShapeDtypeStruct((TOKENS, D_MODEL), _bf16)),
        grid=(TOKENS // tm,),
        in_specs=[aspec, aspec, aspec, aspec,
                  pl.BlockSpec((MIX_OUT, D_MODEL), lambda i: (0, 0)),
                  pl.BlockSpec((tm, D_MODEL), lambda i: (i, 0)),
                  bspec,
                  pl.BlockSpec((1, D_MODEL), lambda i: (0, 0)),
                  bspec, bspec],
        out_specs=(pl.BlockSpec((tm, D_MODEL), lambda i: (i, 0)),
                   pl.BlockSpec((tm, D_MODEL), lambda i: (i, 0))),
        compiler_params=pltpu.CompilerParams(dimension_semantics=("parallel",),
                                             vmem_limit_bytes=VMEM_LIMIT),
        name="out_proj",
    )(*mixed, w, x, g, nw, sc, sh)


def _mlp_kernel(h_ref, w1_ref, w2_ref, x_ref, g_ref, nw_ref, sc_ref, sh_ref, o_ref, hn_ref, acc_ref):
    j = pl.program_id(1)

    @pl.when(j == 0)
    def _():
        acc_ref[...] = jnp.zeros_like(acc_ref)

    u = jnp.dot(h_ref[...], w1_ref[...], preferred_element_type=_f32)
    u = jnp.square(jnp.maximum(u, 0.0)).astype(_bf16)
    acc_ref[...] += jnp.dot(u, w2_ref[...], preferred_element_type=_f32)

    @pl.when(j == pl.num_programs(1) - 1)
    def _():
        xn = x_ref[...] + g_ref[0] * acc_ref[...]
        o_ref[...] = xn
        hn_ref[...] = _rms_mod(xn, nw_ref[...], sc_ref[0], sh_ref[0]).astype(hn_ref.dtype)


def _mlp(h, w1, w2, x, g, nw, sc, sh, next_dtype, tm=512, th=1024):
    per_b = SEQ // tm
    bspec = pl.BlockSpec((1, 1, D_MODEL), lambda i, j: (i // per_b, 0, 0))
    xspec = pl.BlockSpec((tm, D_MODEL), lambda i, j: (i, 0))
    return pl.pallas_call(
        _mlp_kernel,
        out_shape=(jax.ShapeDtypeStruct((TOKENS, D_MODEL), _f32),
                   jax.ShapeDtypeStruct((TOKENS, D_MODEL), next_dtype)),
        grid=(TOKENS // tm, MLP_HIDDEN // th),
        in_specs=[xspec,
                  pl.BlockSpec((D_MODEL, th), lambda i, j: (0, j)),
                  pl.BlockSpec((th, D_MODEL), lambda i, j: (j, 0)),
                  xspec, bspec,
                  pl.BlockSpec((1, D_MODEL), lambda i, j: (0, 0)),
                  bspec, bspec],
        out_specs=(xspec, xspec),
        scratch_shapes=[pltpu.VMEM((tm, D_MODEL), _f32)],
        compiler_params=pltpu.CompilerParams(dimension_semantics=("parallel", "arbitrary"),
                                             vmem_limit_bytes=VMEM_LIMIT),
        name="mlp",
    )(h, w1, w2, x, g, nw, sc, sh)


NSA_R = NSA_HEADS // NSA_KV_GROUPS
NSA_CMP_ROWS = SEQ // CMP_STRIDE
NSA_NC = NSA_CMP_ROWS - CMP_BLOCK // CMP_STRIDE + 1
NSA_NSB = SEQ // SLC_BLOCK
NSA_NQB = SEQ // Q_BLOCK
NSA_SEQS_PER_STEP = 2
assert BATCH % NSA_SEQS_PER_STEP == 0
assert NSA_CMP_ROWS == LANE and Q_BLOCK == LANE and LANE % NSA_NSB == 0 and SLC_TOP_N <= NSA_NSB
assert NSA_KV_GROUPS * NSA_HEAD_DIM == LANE and CMP_BLOCK == 2 * CMP_STRIDE


def _bucket_value(tab_ref, h, rel):
    exact = REL_BUCKETS // 2
    n = jnp.maximum(rel, 0)
    large = exact + (jnp.log(jnp.maximum(n, 1).astype(_f32) / exact)
                     / math.log(REL_MAX_DIST / exact) * (REL_BUCKETS - exact)).astype(jnp.int32)
    bucket = jnp.where(n < exact, n, jnp.minimum(large, REL_BUCKETS - 1))
    val = jnp.full(rel.shape, tab_ref[0, h], _f32)
    for b in range(1, REL_BUCKETS):
        val = jnp.where(bucket == b, tab_ref[b, h], val)
    return val


def _nsa_bias_kernel(tab_ref, tb_ref, cb_ref):
    h = pl.program_id(0)
    kl = lax.broadcasted_iota(jnp.int32, (Q_BLOCK, LANE), 0)
    ql = lax.broadcasted_iota(jnp.int32, (Q_BLOCK, LANE), 1)
    for d in range(3):
        tb_ref[d, 0] = _bucket_value(tab_ref, h, ql - kl + d * Q_BLOCK)
    cmp_end = kl * CMP_STRIDE + (CMP_BLOCK - 1)
    for qb in range(NSA_NQB):
        cb_ref[qb, 0] = _bucket_value(tab_ref, h, qb * Q_BLOCK + ql - cmp_end)


def _nsa_bias_tiles(rel_bias):
    assert 2 * Q_BLOCK >= REL_MAX_DIST
    return pl.pallas_call(
        _nsa_bias_kernel,
        out_shape=(jax.ShapeDtypeStruct((3, NSA_HEADS, Q_BLOCK, LANE), _f32),
                   jax.ShapeDtypeStruct((NSA_NQB, NSA_HEADS, Q_BLOCK, LANE), _f32)),
        grid=(NSA_HEADS,),
        in_specs=[pl.BlockSpec(memory_space=pltpu.SMEM)],
        out_specs=(pl.BlockSpec((3, 1, Q_BLOCK, LANE), lambda h: (0, h, 0, 0)),
                   pl.BlockSpec((NSA_NQB, 1, Q_BLOCK, LANE), lambda h: (0, h, 0, 0))),
        compiler_params=pltpu.CompilerParams(dimension_semantics=("parallel",),
                                             vmem_limit_bytes=VMEM_LIMIT),
        name="nsa_bias_tiles",
    )(rel_bias)


def _nsa_cmp_kernel(tk_ref, tv_ref, pos_ref, w1_ref, w2_ref, kc_ref, vc_ref):
    rows = lax.broadcasted_iota(jnp.int32, (NSA_CMP_ROWS, LANE), 0)
    for idx, (t_ref, o_ref) in enumerate(((tk_ref, kc_ref), (tv_ref, vc_ref))):
        u = v = None
        for i in range(CMP_STRIDE):
            t_i = t_ref[pl.ds(i, NSA_CMP_ROWS, stride=CMP_STRIDE), :]
            ui = jnp.dot((t_i + pos_ref[idx, 0, i]).astype(_bf16), w1_ref[idx, 0, i], preferred_element_type=_f32)
            vi = jnp.dot((t_i + pos_ref[idx, 1, i]).astype(_bf16), w1_ref[idx, 1, i], preferred_element_type=_f32)
            u = ui if u is None else u + ui
            v = vi if v is None else v + vi
        pre = u + pltpu.roll(v, NSA_CMP_ROWS - 1, 0)
        act = pre * jax.nn.sigmoid(pre)
        out = jnp.dot(act.astype(_bf16), w2_ref[idx], preferred_element_type=_f32)
        o_ref[0] = jnp.where(rows < NSA_NC, out, 0.0).astype(_bf16)


def _nsa_compress(proj_b, pos_x, w1_x, w2_x):
    B = proj_b.shape[0] // SEQ
    ospec = pl.BlockSpec((1, NSA_CMP_ROWS, LANE), lambda b: (b, 0, 0))
    return pl.pallas_call(
        _nsa_cmp_kernel,
        out_shape=(jax.ShapeDtypeStruct((B, NSA_CMP_ROWS, LANE), _bf16),) * 2,
        grid=(B,),
        in_specs=[pl.BlockSpec((SEQ, LANE), lambda b: (b, PB['nkc'] // LANE)),
                  pl.BlockSpec((SEQ, LANE), lambda b: (b, PB['nvc'] // LANE)),
                  pl.BlockSpec(pos_x.shape, lambda b: (0,) * pos_x.ndim),
                  pl.BlockSpec(w1_x.shape, lambda b: (0,) * w1_x.ndim),
                  pl.BlockSpec((2, LANE, LANE), lambda b: (0, 0, 0))],
        out_specs=(ospec, ospec),
        compiler_params=pltpu.CompilerParams(dimension_semantics=("parallel",),
                                             vmem_limit_bytes=VMEM_LIMIT),
        name="nsa_compress",
    )(proj_b, proj_b, pos_x, w1_x, w2_x)


def _dot_nt(a, b):
    return lax.dot_general(a, b, (((1,), (1,)), ((), ())), preferred_element_type=_f32)


def _nsa_kernel(q_ref, ks_ref, vs_ref, kw_ref, vw_ref, kc_ref, vc_ref, gate_ref, tb_ref, cb_ref, ov_ref,
                o_ref, m_ref, l_ref, acc_ref, sel_ref):
    R, QB, hd = NSA_R, Q_BLOCK, NSA_HEAD_DIM
    qb = pl.program_id(1)
    kl = lax.broadcasted_iota(jnp.int32, (QB, LANE), 0)
    t_q = qb * QB + lax.broadcasted_iota(jnp.int32, (QB, LANE), 1)
    NSQ = q_ref.shape[0]
    G = NSA_KV_GROUPS
    KB = 2 * QB

    def split_r(x):
        return [x[:, r * LANE:(r + 1) * LANE] for r in range(R)]

    gates_t = [jax.nn.sigmoid(gate_ref[sq]).T for sq in range(NSQ)]
    qps = [[jnp.concatenate([q_ref[sq, :, (R * g + r) * LANE:(R * g + r + 1) * LANE] for r in range(R)], axis=0)
            for g in range(G)] for sq in range(NSQ)]

    krow = lax.broadcasted_iota(jnp.int32, (KB, LANE), 0)
    t_q2 = qb * QB + lax.broadcasted_iota(jnp.int32, (KB, LANE), 1)
    SEL, WIN = 0, 1
    chain = lambda sq, br, g: (sq * 2 + br) * G + g

    def flash_step(work):
        items, scores = [], []
        for p, br in work:
            off = pl.multiple_of(p * KB, KB)
            rel = t_q2 - (p * KB + krow)
            tidx = [jnp.clip(qb - (KB // QB) * p - i, 0, 2) for i in range(KB // QB)]
            for sq in range(NSQ):
                k_blk = (ks_ref, kw_ref)[br][sq, pl.ds(off, KB), :]
                for g in range(G):
                    items.append((p, br, sq, g, off, rel, tidx))
                    scores.append(split_r(_dot_nt(k_blk, qps[sq][g])))
        probs, alphas = [], []
        for (p, br, sq, g, off, rel, tidx), s_t in zip(items, scores):
            ch = chain(sq, br, g)
            if br == SEL:
                blocks = [jnp.broadcast_to(sel_ref[sq * G + g, pl.ds((KB // SLC_BLOCK) * p + i, 1), :],
                                           (SLC_BLOCK, LANE)) for i in range(KB // SLC_BLOCK)]
                mask = (rel >= 0) & (jnp.concatenate(blocks, axis=0) > 0.5)
            else:
                mask = (rel >= 0) & (rel < WINDOW)
            s_t = jnp.concatenate(
                [jnp.where(mask, s_t[r] + jnp.concatenate([tb_ref[ti, R * g + r] for ti in tidx], axis=0), NEG_INF)
                 for r in range(R)], axis=1)
            m_old = m_ref[ch]
            m_new = jnp.maximum(m_old, jnp.max(s_t, axis=0, keepdims=True))
            e = jnp.exp(s_t - m_new)
            alpha = jnp.exp(m_old - m_new)
            l_ref[ch] = alpha * l_ref[ch] + jnp.sum(e, axis=0, keepdims=True)
            m_ref[ch] = m_new
            probs.append(e.astype(_bf16))
            alphas.append(alpha)
        for (p, br, sq, g, off, rel, tidx), e, alpha in zip(items, probs, alphas):
            ch = chain(sq, br, g)
            pv = _dot_tn((vs_ref, vw_ref)[br][sq, pl.ds(off, KB), :], e)
            acc_ref[ch] = alpha * acc_ref[ch] + pv[g * hd:(g + 1) * hd]

    o_cmp = []
    mask_c = (t_q - (kl * CMP_STRIDE + CMP_BLOCK - 1) >= 0) & (kl < NSA_NC)
    for sq, g in [(sq, g) for sq in range(NSQ) for g in range(G)]:
        qp = qps[sq][g]
        s_t = split_r(_dot_nt(kc_ref[sq], qp))
        s_t = jnp.concatenate([jnp.where(mask_c, s_t[r] + cb_ref[0, R * g + r], NEG_INF) for r in range(R)], axis=1)
        e = split_r(jnp.exp(s_t - jnp.max(s_t, axis=0, keepdims=True)))
        e = jnp.concatenate([jnp.where(mask_c, e[r], 0.0) for r in range(R)], axis=1)
        den = jnp.sum(e, axis=0, keepdims=True)
        p = (e / jnp.where(den > 0.0, den, 1.0)).astype(_bf16)
        o_cmp.append(_dot_tn(vc_ref[sq], p)[g * hd:(g + 1) * hd])

        imp = sum(split_r(jnp.dot(ov_ref[...], p, preferred_element_type=_f32)))
        j = lax.broadcasted_iota(jnp.int32, (NSA_NSB, LANE), 0)
        cur = (qb * QB + lax.broadcasted_iota(jnp.int32, (NSA_NSB, LANE), 1)) // SLC_BLOCK
        imp = jnp.where((j == 0) | (j == cur) | (j == cur - 1), FORCE_SCORE, imp)
        imp = jnp.where(j <= cur, imp, NEG_INF)
        cnt = jnp.zeros((NSA_NSB, LANE), _f32)
        for jo in range(NSA_NSB):
            other = imp[jo:jo + 1, :]
            beats = (other > imp) | ((other == imp) & (j > jo))
            cnt = cnt + jnp.where(beats, 1.0, 0.0)
        sel_ref[sq * G + g] = jnp.where(cnt < SLC_TOP_N, 1.0, 0.0)

    p_hi = qb // (KB // QB) + 1
    n_win = p_hi - jnp.maximum(qb - WINDOW // QB, 0) // (KB // QB)
    m_ref[...] = jnp.full(m_ref.shape, NEG_INF, _f32)
    l_ref[...] = jnp.zeros(l_ref.shape, _f32)
    acc_ref[...] = jnp.zeros(acc_ref.shape, _f32)

    def both(i, carry):
        p = p_hi - 1 - i
        flash_step([(p, SEL), (p, WIN)])
        return carry

    def selected_pair(i, carry):
        p = p_hi - 1 - n_win - 2 * i
        flash_step([(p, SEL), (p - 1, SEL)])
        return carry

    def selected_one(i, carry):
        flash_step([(0, SEL)])
        return carry

    n_sel = p_hi - n_win
    lax.fori_loop(0, n_win, both, 0)
    lax.fori_loop(0, n_sel // 2, selected_pair, 0)
    lax.fori_loop(0, n_sel % 2, selected_one, 0)
    for sq in range(NSQ):
        heads = [None] * NSA_HEADS
        for g in range(G):
            cs, cw = chain(sq, SEL, g), chain(sq, WIN, g)
            o_c, o_s, o_w = split_r(o_cmp[sq * G + g]), split_r(acc_ref[cs] / l_ref[cs]), split_r(acc_ref[cw] / l_ref[cw])
            for r in range(R):
                h = R * g + r
                gt = gates_t[sq]
                heads[h] = (gt[3 * h:3 * h + 1] * o_c[r] + gt[3 * h + 1:3 * h + 2] * o_s[r]
                            + gt[3 * h + 2:3 * h + 3] * o_w[r])
        for pk in range(NSA_HEADS * hd // LANE):
            per = LANE // hd
            pair = jnp.concatenate(heads[per * pk:per * (pk + 1)], axis=0)
            o_ref[sq, :, pk * LANE:(pk + 1) * LANE] = pair.T.astype(o_ref.dtype)


def _nsa_attention(proj_a, proj_b, kc, vc, tb, cb, overlap_t):
    B = kc.shape[0]
    nsq = NSA_SEQS_PER_STEP
    nqb = NSA_NQB
    qcols = NSA_HEADS * LANE
    out_cols = NSA_HEADS * NSA_HEAD_DIM
    pa = proj_a.reshape(B, SEQ, proj_a.shape[1])
    pb = proj_b.reshape(B, SEQ, proj_b.shape[1])
    kv = lambda blk: pl.BlockSpec((nsq, SEQ, LANE), lambda b, q: (b, 0, qcols // LANE + blk))
    cspec = pl.BlockSpec((nsq, NSA_CMP_ROWS, LANE), lambda b, q: (b, 0, 0))
    nchain = nsq * 2 * NSA_KV_GROUPS
    out = pl.pallas_call(
        _nsa_kernel,
        out_shape=jax.ShapeDtypeStruct((B, SEQ, out_cols), _bf16),
        grid=(B // nsq, nqb),
        in_specs=[pl.BlockSpec((nsq, Q_BLOCK, qcols), lambda b, q: (b, q, 0)),
                  kv(0), kv(1), kv(2), kv(3), cspec, cspec,
                  pl.BlockSpec((nsq, Q_BLOCK, LANE), lambda b, q: (b, q, PB['misc'] // LANE)),
                  pl.BlockSpec((3, NSA_HEADS, Q_BLOCK, LANE), lambda b, q: (0, 0, 0, 0)),
                  pl.BlockSpec((1, NSA_HEADS, Q_BLOCK, LANE), lambda b, q: (q, 0, 0, 0)),
                  pl.BlockSpec((NSA_NSB, LANE), lambda b, q: (0, 0))],
        out_specs=pl.BlockSpec((nsq, Q_BLOCK, out_cols), lambda b, q: (b, q, 0)),
        scratch_shapes=[pltpu.VMEM((nchain, 1, NSA_R * Q_BLOCK), _f32),
                        pltpu.VMEM((nchain, 1, NSA_R * Q_BLOCK), _f32),
                        pltpu.VMEM((nchain, NSA_HEAD_DIM, NSA_R * Q_BLOCK), _f32),
                        pltpu.VMEM((nsq * NSA_KV_GROUPS, NSA_NSB, Q_BLOCK), _f32)],
        compiler_params=pltpu.CompilerParams(dimension_semantics=("parallel", "arbitrary"),
                                             vmem_limit_bytes=VMEM_LIMIT),
        name="nsa_attention",
    )(pa, pa, pa, pa, pa, kc, vc, pb, tb, cb, overlap_t)
    return out.reshape(B * SEQ, out_cols)


def _nsa_overlap():
    n = jnp.arange(NSA_CMP_ROWS)[None, :]
    jj = jnp.arange(NSA_NSB)[:, None]
    return ((n * CMP_STRIDE <= jj * SLC_BLOCK + SLC_BLOCK - 1)
            & (n * CMP_STRIDE + CMP_BLOCK - 1 >= jj * SLC_BLOCK) & (n < NSA_NC)).astype(_bf16)


def _nsa_weight_prep(w_in, cmp_pos, cmp_w1, cmp_w2):
    G, hd = NSA_KV_GROUPS, NSA_HEAD_DIM
    eye = jnp.eye(G, dtype=_f32)
    nq = NSA_HEADS * hd
    w_in = w_in.astype(_bf16)
    zeros = jnp.zeros((DEPTH, D_MODEL, hd), _bf16)
    q_cols = []
    for h in range(NSA_HEADS):
        wq_h = w_in[:, :, h * hd:(h + 1) * hd] * (hd ** -0.5)
        q_cols += [wq_h, zeros] if h // NSA_R == 0 else [zeros, wq_h]
    slc_win = w_in[:, :, nq + 2 * NSA_KV:nq + 6 * NSA_KV]
    w_a = jnp.concatenate(q_cols + [slc_win], axis=-1)
    w_b = jnp.concatenate([w_in[:, :, _IN_OFF[n]:_IN_OFF[n] + w] for n, w in _PB_SEGMENTS]
                          + [jnp.zeros((DEPTH, D_MODEL, PROJ_B_COLS - PROJ_B_USED), _bf16)], axis=-1)
    half = CMP_BLOCK // 2
    w1 = cmp_w1.reshape(DEPTH, 2, CMP_BLOCK, hd, hd)
    w1_x = jnp.einsum('lxide,gh->lxigdhe', w1, eye).reshape(DEPTH, 2, 2, half, G * hd, G * hd).astype(_bf16)
    w2_x = jnp.einsum('lxde,gh->lxgdhe', cmp_w2, eye).reshape(DEPTH, 2, G * hd, G * hd).astype(_bf16)
    pos_x = jnp.broadcast_to(cmp_pos[:, :, :, None, :], (DEPTH, 2, CMP_BLOCK, G, hd))
    pos_x = pos_x.reshape(DEPTH, 2, 2, half, 1, G * hd)
    return w_a, w_b, w1_x, w2_x, pos_x


REC_BLOCK = 2 * CHUNK
REC_SEQS_PER_STEP = 2
assert REC_BLOCK == LANE and SEQ % REC_BLOCK == 0 and BATCH % REC_SEQS_PER_STEP == 0


def _split3(x):
    hi = x.astype(_bf16)
    r = x - hi.astype(_f32)
    mid = r.astype(_bf16)
    lo = (r - mid.astype(_f32)).astype(_bf16)
    return hi, mid, lo


def _dot_exact_rhs(m, x):
    return sum(jnp.dot(m, p, preferred_element_type=_f32) for p in _split3(x))


def _dot_exact_lhs(x, m):
    return sum(jnp.dot(p, m, preferred_element_type=_f32) for p in _split3(x))


def _dot_tn(a, b):
    return lax.dot_general(a, b, (((0,), (0,)), ((), ())), preferred_element_type=_f32)


def _mm3(x, y):
    xh = x.astype(_bf16)
    xl = (x - xh.astype(_f32)).astype(_bf16)
    yh = y.astype(_bf16)
    yl = (y - yh.astype(_f32)).astype(_bf16)
    return (jnp.dot(xh, yh, preferred_element_type=_f32) + jnp.dot(xh, yl, preferred_element_type=_f32)
            + jnp.dot(xl, yh, preferred_element_type=_f32))


def _chunk_masks():
    ri = lax.broadcasted_iota(jnp.int32, (REC_BLOCK, REC_BLOCK), 0)
    ci = lax.broadcasted_iota(jnp.int32, (REC_BLOCK, REC_BLOCK), 1)
    same = (ri // CHUNK) == (ci // CHUNK)
    return ri, ci, (ci <= ri) & same, (ci < ri) & same, (ri <= ci) & same


def _as_mxu(mask):
    return jnp.where(mask, 1.0, 0.0).astype(_bf16)


def _softplus(x):
    return jnp.maximum(x, 0.0) + jnp.log1p(jnp.exp(-jnp.abs(x)))


def _silu(x):
    return x * jax.nn.sigmoid(x)


def _chunk_last(x):
    ri = lax.broadcasted_iota(jnp.int32, x.shape, 0)
    return jnp.where(ri < CHUNK, x[CHUNK - 1:CHUNK], x[2 * CHUNK - 1:2 * CHUNK])


def _expand_heads(x, lane0, nheads, width):
    per = LANE // width
    lane = lax.broadcasted_iota(jnp.int32, (x.shape[0], LANE), 1)
    pieces = []
    for p0 in range(lane0, lane0 + nheads, per):
        piece = jnp.broadcast_to(x[:, p0:p0 + 1], (x.shape[0], LANE))
        for k in range(1, per):
            piece = jnp.where(lane < k * width, piece, jnp.broadcast_to(x[:, p0 + k:p0 + k + 1], (x.shape[0], LANE)))
        pieces.append(piece)
    return jnp.concatenate(pieces, axis=1) if len(pieces) > 1 else pieces[0]


def _conv_silu(x, prev, w_ref, c0, bias=None):
    n, C = x.shape
    ntap = w_ref.shape[0]
    rows = lax.broadcasted_iota(jnp.int32, (n, C), 0)
    acc = x * w_ref[ntap - 1:ntap, c0:c0 + C]
    for s in range(1, ntap):
        xs = jnp.where(rows < s, pltpu.roll(prev, s, 0), pltpu.roll(x, s, 0))
        acc = acc + xs * w_ref[ntap - 1 - s:ntap - s, c0:c0 + C]
    if bias is not None:
        acc = acc + bias
    return _silu(acc)


def _rms(x, w):
    return x * lax.rsqrt(jnp.mean(x * x, axis=-1, keepdims=True) + EPS) * w


def _ssd_kernel(z_ref, xbc_ref, dt_ref, cw_ref, cb_ref, dtb_ref, alog_ref, dskip_ref, nw_ref, o_ref,
                prev_ref, state_ref):
    @pl.when(pl.program_id(1) == 0)
    def _():
        prev_ref[...] = jnp.zeros(prev_ref.shape, _f32)
        state_ref[...] = jnp.zeros(state_ref.shape, _f32)

    for ns in range(z_ref.shape[0]):
        _ssd_block(z_ref.at[ns], xbc_ref.at[ns], dt_ref.at[ns], cw_ref, cb_ref, dtb_ref, alog_ref, dskip_ref, nw_ref,
                   o_ref.at[ns], prev_ref.at[ns], state_ref.at[ns])


def _ssd_block(z_ref, xbc_ref, dt_ref, cw_ref, cb_ref, dtb_ref, alog_ref, dskip_ref, nw_ref, o_ref,
               prev_ref, state_ref):
    G, R, P, N = SSD_GROUPS, SSD_HEADS // SSD_GROUPS, SSD_HEAD_DIM, SSD_STATE
    x_in = xbc_ref[...]
    xc = _conv_silu(x_in, prev_ref[...], cw_ref, 0, cb_ref[...])
    prev_ref[...] = x_in
    _, _, tril, _, triu = _chunk_masks()
    lane = lax.broadcasted_iota(jnp.int32, (REC_BLOCK, LANE), 1)
    L0 = PBL['sdt']
    dt = _softplus(dt_ref[...] + dtb_ref[...])
    da = dt * (-jnp.exp(alog_ref[...]))
    a_cum = _dot_exact_rhs(_as_mxu(tril), da)
    a_cum_t = _dot_exact_lhs(da.T, _as_mxu(triu))
    a_last = _chunk_last(a_cum)
    xs = xc[:, :SSD_INNER]
    xdt = xs * _expand_heads(dt, L0, SSD_HEADS, P)
    xdtd = (xdt * _expand_heads(jnp.exp(a_last - a_cum), L0, SSD_HEADS, P)).astype(_bf16)
    xdt_b = xdt.astype(_bf16)
    ea = _expand_heads(jnp.exp(a_cum), L0, SSD_HEADS, P)
    y_groups = []
    for g in range(G):
        bm = xc[:, SSD_INNER + g * N:SSD_INNER + (g + 1) * N].astype(_bf16)
        cm = xc[:, SSD_INNER + (G + g) * N:SSD_INNER + (G + g + 1) * N].astype(_bf16)
        cbm = _dot_nt(cm, bm)
        intra = []
        for pr in range(R // 2):
            both = []
            for k in range(2):
                h = g * R + 2 * pr + k
                seg = jnp.exp(jnp.where(tril, a_cum[:, L0 + h:L0 + h + 1] - a_cum_t[L0 + h:L0 + h + 1, :], -jnp.inf))
                both.append(jnp.dot((cbm * seg).astype(_bf16), xdt_b[:, (h - k) * P:(h - k + 2) * P],
                                    preferred_element_type=_f32))
            intra.append(jnp.where(lane < P, both[0], both[1]))
        y_intra = jnp.concatenate(intra, axis=1)
        prev_rows = []
        for c in range(REC_BLOCK // CHUNK):
            rows = slice(c * CHUNK, (c + 1) * CHUNK)
            st = state_ref[g]
            prev_rows.append(jnp.dot(cm[rows], st.astype(_bf16), preferred_element_type=_f32))
            dec = _expand_heads(jnp.exp(a_cum[(c + 1) * CHUNK - 1:(c + 1) * CHUNK]), L0, SSD_HEADS, P)
            state_ref[g] = (st * dec[:, g * R * P:(g + 1) * R * P]
                            + _dot_tn(bm[rows], xdtd[rows, g * R * P:(g + 1) * R * P]))
        y_groups.append(y_intra + jnp.concatenate(prev_rows, axis=0) * ea[:, g * R * P:(g + 1) * R * P])
    y = jnp.concatenate(y_groups, axis=1) + xs * dskip_ref[...]
    y = y * _silu(z_ref[...])
    gw = SSD_INNER // G
    for g in range(G):
        o_ref[:, g * gw:(g + 1) * gw] = _rms(y[:, g * gw:(g + 1) * gw], nw_ref[:, g * gw:(g + 1) * gw]).astype(o_ref.dtype)


def _gdn_kernel(q_ref, k_ref, v_ref, z_ref, ba_ref, cw_ref, dtb_ref, alog_ref, nw_ref, o_ref,
                pq_ref, pk_ref, pv_ref, state_ref):
    H, Dh = GDN_HEADS, GDN_HEAD_DIM

    @pl.when(pl.program_id(1) == 0)
    def _():
        for r in (pq_ref, pk_ref, pv_ref, state_ref):
            r[...] = jnp.zeros(r.shape, _f32)

    NS = q_ref.shape[0]
    ri, ci, tril, strict, triu = _chunk_masks()
    q, k, v, beta, gcum, gcum_t, glast = [], [], [], [], [], [], []
    for ns in range(NS):
        for dst, x_ref, p_ref, c0 in ((q, q_ref, pq_ref, 0), (k, k_ref, pk_ref, GDN_WIDTH), (v, v_ref, pv_ref, 2 * GDN_WIDTH)):
            x_in = x_ref[ns]
            dst.append(_conv_silu(x_in, p_ref[ns], cw_ref, c0))
            p_ref[ns] = x_in
        ba = ba_ref[ns]
        beta.append(jax.nn.sigmoid(ba))
        gl = -jnp.exp(alog_ref[...]) * _softplus(ba + dtb_ref[...])
        gcum.append(_dot_exact_rhs(_as_mxu(tril), gl))
        gcum_t.append(_dot_exact_lhs(gl.T, _as_mxu(triu)))
        glast.append(_chunk_last(gcum[ns]))
    zero_rows = jnp.zeros((CHUNK, Dh), _f32)
    eye = jnp.where(tril & jnp.logical_not(strict), 1.0, 0.0)
    hs = range(NS * H)
    seq = [i // H for i in hs]
    sls = [slice((i % H) * Dh, (i % H + 1) * Dh) for i in hs]
    qh = [q[seq[i]][:, sls[i]] for i in hs]
    kh = [k[seq[i]][:, sls[i]] for i in hs]
    qq = [qh[i] * lax.rsqrt(jnp.sum(qh[i] * qh[i], axis=-1, keepdims=True) + EPS) * (Dh ** -0.5) for i in hs]
    kk = [kh[i] * lax.rsqrt(jnp.sum(kh[i] * kh[i], axis=-1, keepdims=True) + EPS) for i in hs]
    lb = [PBL['gbeta'] + i % H for i in hs]
    lg = [PBL['gbeta'] + H + i % H for i in hs]
    bcol = [beta[seq[h]][:, lb[h]:lb[h] + 1] for h in hs]
    gcol = [gcum[seq[h]][:, lg[h]:lg[h] + 1] for h in hs]
    decay = [jnp.exp(jnp.where(tril, gcol[h] - gcum_t[seq[h]][lg[h]:lg[h] + 1, :], -jnp.inf)) for h in hs]
    kb = [kk[h] * bcol[h] for h in hs]
    s = [_dot_nt(jnp.concatenate([kb[h], qq[h]], axis=0).astype(_bf16), kk[h].astype(_bf16)) for h in hs]
    a_mat = [jnp.where(strict, s[h][:REC_BLOCK] * decay[h], 0.0) for h in hs]
    aqk = [(s[h][REC_BLOCK:] * decay[h]).astype(_bf16) for h in hs]
    SUB = 8
    same = lambda n: (ri // n) == (ci // n)
    a_sub = [jnp.where(same(SUB), a_mat[h], 0.0) for h in hs]
    tinv = [eye - a_sub[h] for h in hs]
    pw = a_sub
    for _ in range(SUB.bit_length() - 2):
        pw = [_mm3(pw[h], pw[h]) for h in hs]
        tinv = [tinv[h] + _mm3(tinv[h], pw[h]) for h in hs]
    n = SUB
    while n < CHUNK:
        enclosed = same(2 * n) & jnp.logical_not(same(n))
        tc = [_mm3(tinv[h], jnp.where(enclosed, a_mat[h], 0.0)) for h in hs]
        tinv = [tinv[h] - _mm3(tc[h], tinv[h]) for h in hs]
        n *= 2
    sol = [_mm3(tinv[h], jnp.concatenate([v[seq[h]][:, sls[h]] * bcol[h], kb[h] * jnp.exp(gcol[h])], axis=1))
           for h in hs]
    u = [sol[h][:, :Dh] for h in hs]
    w = [sol[h][:, Dh:].astype(_bf16) for h in hs]
    q_dec = [(qq[h] * jnp.exp(gcol[h])).astype(_bf16) for h in hs]
    k_end = [(kk[h] * jnp.exp(glast[seq[h]][:, lg[h]:lg[h] + 1] - gcol[h])).astype(_bf16) for h in hs]
    o_rows = [[] for _ in hs]
    for c in range(REC_BLOCK // CHUNK):
        rows = slice(c * CHUNK, (c + 1) * CHUNK)
        st = [state_ref[h] for h in hs]
        st_b = [st[h].astype(_bf16) for h in hs]
        v_new = [u[h][rows] - jnp.dot(w[h][rows], st_b[h], preferred_element_type=_f32) for h in hs]
        v_full = [jnp.concatenate([v_new[h], zero_rows] if c == 0 else [zero_rows, v_new[h]], axis=0).astype(_bf16)
                  for h in hs]
        for h in hs:
            o_rows[h].append(jnp.dot(q_dec[h][rows], st_b[h], preferred_element_type=_f32)
                             + jnp.dot(aqk[h][rows], v_full[h], preferred_element_type=_f32))
            d_last = jnp.exp(gcum[seq[h]][(c + 1) * CHUNK - 1:(c + 1) * CHUNK, lg[h]:lg[h] + 1])
            state_ref[h] = st[h] * d_last + _dot_tn(k_end[h][rows], v_new[h].astype(_bf16))
    for h in hs:
        o = _rms(jnp.concatenate(o_rows[h], axis=0), nw_ref[...]) * _silu(z_ref[seq[h], :, sls[h]])
        o_ref[seq[h], :, sls[h]] = o.astype(o_ref.dtype)


def _gla_kernel(q_ref, k_ref, v_ref, go_ref, lr_ref, w2_ref, gb_ref, nw_ref, o_ref, state_ref):
    @pl.when(pl.program_id(1) == 0)
    def _():
        state_ref[...] = jnp.zeros(state_ref.shape, _f32)

    for ns in range(q_ref.shape[0]):
        _gla_block(q_ref.at[ns], k_ref.at[ns], v_ref.at[ns], go_ref.at[ns], lr_ref.at[ns], w2_ref, gb_ref, nw_ref,
                   o_ref.at[ns], state_ref.at[ns])


def _gla_block(q_ref, k_ref, v_ref, go_ref, lr_ref, w2_ref, gb_ref, nw_ref, o_ref, state_ref):
    H, Dk, Dv = GLA_HEADS, GLA_DK, GLA_DV
    _, _, tril, _, _ = _chunk_masks()
    lane = lax.broadcasted_iota(jnp.int32, (REC_BLOCK, LANE), 1)
    pre = jnp.dot(lr_ref[...].astype(_bf16), w2_ref[...], preferred_element_type=_f32) + gb_ref[...]
    gk = (jnp.minimum(pre, 0.0) - jnp.log1p(jnp.exp(-jnp.abs(pre)))) / GLA_GATE_NORM
    bcum = _dot_exact_rhs(_as_mxu(tril), gk)
    blast = _chunk_last(bcum)
    q_dec = q_ref[...] * (Dk ** -0.5) * jnp.exp(bcum)
    k_inv = (k_ref[...] * jnp.exp(-bcum)).astype(_bf16)
    k_end = (k_ref[...] * jnp.exp(blast - bcum)).astype(_bf16)
    per = LANE // Dk
    nchunk = REC_BLOCK // CHUNK
    chunk_rows = [slice(c * CHUNK, (c + 1) * CHUNK) for c in range(nchunk)]
    psl = [slice(pr * LANE, (pr + 1) * LANE) for pr in range(H // per)]
    qm = [jnp.where(lane // Dk == h % per, q_dec[:, psl[h // per]], 0.0).astype(_bf16) for h in range(H)]
    vh = [v_ref[:, h * Dv:(h + 1) * Dv].astype(_bf16) for h in range(H)]
    attn = [jnp.where(tril, _dot_nt(qm[h], k_inv[:, psl[h // per]]), 0.0).astype(_bf16) for h in range(H)]
    o_intra = [jnp.dot(attn[h], vh[h], preferred_element_type=_f32) for h in range(H)]
    local = []
    for rows in chunk_rows:
        per_pair = []
        for pr in range(H // per):
            loc = None
            for k in range(per):
                lk = _dot_tn(vh[pr * per + k][rows], k_end[rows, psl[pr]])
                loc = lk if loc is None else jnp.where(lane < k * Dk, loc, lk)
            per_pair.append(loc)
        local.append(per_pair)
    o_prev = [[] for _ in range(H)]
    for c, rows in enumerate(chunk_rows):
        for pr in range(H // per):
            st = state_ref[pr]
            st_b = st.astype(_bf16)
            for k in range(per):
                o_prev[pr * per + k].append(_dot_nt(qm[pr * per + k][rows], st_b))
            state_ref[pr] = st * jnp.exp(bcum[(c + 1) * CHUNK - 1:(c + 1) * CHUNK, psl[pr]]) + local[c][pr]
    for h in range(H):
        o = o_intra[h] + jnp.concatenate(o_prev[h], axis=0)
        o = _rms(o, nw_ref[...]) * _silu(go_ref[:, h * Dv:(h + 1) * Dv])
        o_ref[:, h * Dv:(h + 1) * Dv] = o.astype(o_ref.dtype)


def _rec_call(body, proj_b, col_blocks, params, out_cols, scratch, name, nseq=None):
    B = proj_b.shape[0] // SEQ
    nblk = SEQ // REC_BLOCK
    if nseq is None:
        src, grid = proj_b, (B, nblk)
        in_specs = [pl.BlockSpec((REC_BLOCK, w), (lambda b, t, c=c0 // w: (b * nblk + t, c))) for c0, w in col_blocks]
        out_shape = jax.ShapeDtypeStruct((B * SEQ, out_cols), _bf16)
        out_spec = pl.BlockSpec((REC_BLOCK, out_cols), lambda b, t: (b * nblk + t, 0))
    else:
        src, grid = proj_b.reshape(B, SEQ, proj_b.shape[1]), (B // nseq, nblk)
        in_specs = [pl.BlockSpec((nseq, REC_BLOCK, w), (lambda b, t, c=c0 // w: (b, t, c))) for c0, w in col_blocks]
        out_shape = jax.ShapeDtypeStruct((B, SEQ, out_cols), _bf16)
        out_spec = pl.BlockSpec((nseq, REC_BLOCK, out_cols), lambda b, t: (b, t, 0))
    for p in params:
        in_specs.append(pl.BlockSpec(p.shape, lambda b, t, nd=p.ndim: (0,) * nd))
    out = pl.pallas_call(
        body,
        out_shape=out_shape,
        grid=grid,
        in_specs=in_specs,
        out_specs=out_spec,
        scratch_shapes=scratch,
        compiler_params=pltpu.CompilerParams(dimension_semantics=("parallel", "arbitrary"),
                                             vmem_limit_bytes=VMEM_LIMIT),
        name=name,
    )(*([src] * len(col_blocks)), *params)
    return out.reshape(B * SEQ, out_cols)


def _lane_pad(v, lane0=0):
    return jnp.pad(v.astype(_f32), (lane0, LANE - lane0 - v.shape[0]))[None]


def _ssd_call(proj_b, conv_w, conv_b, dt_bias, a_log, d_skip, norm_w):
    cols = [(PB['sz'], SSD_INNER), (PB['sxbc'], SSD_XBC), (PB['misc'], LANE)]
    params = [conv_w, conv_b[None], _lane_pad(dt_bias, PBL['sdt']), _lane_pad(a_log, PBL['sdt']),
              jnp.repeat(d_skip, SSD_HEAD_DIM)[None], norm_w[None]]
    ns = REC_SEQS_PER_STEP
    scratch = [pltpu.VMEM((ns, REC_BLOCK, SSD_XBC), _f32),
               pltpu.VMEM((ns, SSD_GROUPS, SSD_STATE, SSD_INNER // SSD_GROUPS), _f32)]
    return _rec_call(_ssd_kernel, proj_b, cols, params, SSD_INNER, scratch, "ssd", nseq=ns)


def _gdn_call(proj_b, conv_w, dt_bias, a_log, norm_w):
    W = GDN_WIDTH
    cols = [(PB['gq'], W), (PB['gk'], W), (PB['gv'], W), (PB['gz'], W), (PB['misc'], LANE)]
    decay_lane = PBL['gbeta'] + GDN_HEADS
    params = [conv_w, _lane_pad(dt_bias, decay_lane), _lane_pad(a_log, decay_lane), norm_w[None]]
    ns = REC_SEQS_PER_STEP
    scratch = ([pltpu.VMEM((ns, REC_BLOCK, W), _f32)] * 3
               + [pltpu.VMEM((ns * GDN_HEADS, GDN_HEAD_DIM, GDN_HEAD_DIM), _f32)])
    return _rec_call(_gdn_kernel, proj_b, cols, params, W, scratch, "gdn", nseq=ns)


def _gla_call(proj_b, gate_w2, gate_b, norm_w):
    cols = [(PB['lq'], GLA_KEY), (PB['lk'], GLA_KEY), (PB['lv'], GLA_VAL), (PB['lg'], GLA_VAL), (PB['misc'], LANE)]
    w2 = jnp.pad(gate_w2, ((PBL['llr'], LANE - PBL['llr'] - GLA_GATE_RANK), (0, 0))).astype(_bf16)
    params = [w2, gate_b[None], norm_w[None]]
    ns = REC_SEQS_PER_STEP
    scratch = [pltpu.VMEM((ns, GLA_HEADS * GLA_DK // LANE, GLA_DV, LANE), _f32)]
    return _rec_call(_gla_kernel, proj_b, cols, params, GLA_VAL, scratch, "gla", nseq=ns)


def kernel(x, c, rel_bias, norm1_w, norm2_w, ada_w, ada_b, w_in, w_out, nsa_cmp_pos, nsa_cmp_w1, nsa_cmp_w2, ssd_conv_w, ssd_conv_b, ssd_dt_bias, ssd_a_log, ssd_d, ssd_norm_w, gdn_conv_w, gdn_dt_bias, gdn_a_log, gdn_norm_w, gla_gate_w2, gla_gate_b, gla_norm_w, mlp_w1, mlp_w2, final_norm_w):
    B, S, D = x.shape
    mod = _ada_all(c, ada_w, ada_b).reshape(DEPTH, B, 6, 1, D)
    w_a, w_b, cmp_w1_x, cmp_w2_x, cmp_pos_x = _nsa_weight_prep(w_in, nsa_cmp_pos, nsa_cmp_w1, nsa_cmp_w2)
    w_out_b = w_out.astype(_bf16)
    w1_b = mlp_w1.astype(_bf16)
    w2_b = mlp_w2.astype(_bf16)
    tb, cb = _nsa_bias_tiles(rel_bias)
    overlap_t = _nsa_overlap()
    xf = x.reshape(TOKENS, D)
    no_mod = jnp.zeros((B, 1, D), _f32)
    h = _norm_mod(xf, norm1_w[0][None], mod[0, :, 1], mod[0, :, 0])
    for l in range(DEPTH):
        sh1, sc1, g1, sh2, sc2, g2 = (mod[l, :, i] for i in range(6))
        proj_a = _matmul(h, w_a[l], out_dtype=_bf16)
        proj_b = _matmul(h, w_b[l], tn=PROJ_B_TN)
        kc, vc = _nsa_compress(proj_b, cmp_pos_x[l], cmp_w1_x[l], cmp_w2_x[l])
        y_nsa = _nsa_attention(proj_a, proj_b, kc, vc, tb, cb, overlap_t)
        y_ssd = _ssd_call(proj_b, ssd_conv_w[l], ssd_conv_b[l], ssd_dt_bias[l], ssd_a_log[l], ssd_d[l], ssd_norm_w[l])
        y_gdn = _gdn_call(proj_b, gdn_conv_w[l], gdn_dt_bias[l], gdn_a_log[l], gdn_norm_w[l])
        y_gla = _gla_call(proj_b, gla_gate_w2[l], gla_gate_b[l], gla_norm_w[l])
        xf, h2 = _out_proj((y_nsa, y_ssd, y_gdn, y_gla), w_out_b[l], xf, g1,
                           norm2_w[l][None], sc2, sh2)
        if l + 1 < DEPTH:
            xf, h = _mlp(h2, w1_b[l], w2_b[l], xf, g2, norm1_w[l + 1][None], mod[l + 1, :, 1], mod[l + 1, :, 0], _bf16)
        else:
            _, out = _mlp(h2, w1_b[l], w2_b[l], xf, g2, final_norm_w[None], no_mod, no_mod, _f32)
    return out.reshape(B, S, D)
```

```python
import math
from functools import partial

import jax
import jax.numpy as jnp
from jax import lax
from jax.experimental import pallas as pl
from jax.experimental.pallas import tpu as pltpu

D_MODEL = 2048
BATCH = 16
SEQ = 2048
DEPTH = 4

MIX_GROUP = D_MODEL // 4
NSA_HEAD_DIM = 64
NSA_HEADS = MIX_GROUP // NSA_HEAD_DIM
NSA_KV_GROUPS = max(1, NSA_HEADS // 4)
NSA_KV = NSA_KV_GROUPS * NSA_HEAD_DIM
CMP_BLOCK = 32
CMP_STRIDE = 16
SLC_BLOCK = 64
SLC_TOP_N = 8
WINDOW = 512
Q_BLOCK = 128
REL_BUCKETS = 32
REL_MAX_DIST = 128
SSD_HEAD_DIM = 64
SSD_HEADS = MIX_GROUP // SSD_HEAD_DIM
SSD_INNER = SSD_HEADS * SSD_HEAD_DIM
SSD_GROUPS = 2
SSD_STATE = 128
SSD_CONV = 4
SSD_XBC = SSD_INNER + 2 * SSD_GROUPS * SSD_STATE
GDN_HEAD_DIM = 128
GDN_HEADS = MIX_GROUP // GDN_HEAD_DIM
GDN_WIDTH = GDN_HEADS * GDN_HEAD_DIM
GDN_CONV = 4
GLA_DV = 128
GLA_HEADS = MIX_GROUP // GLA_DV
GLA_DK = GLA_DV // 2
GLA_KEY = GLA_HEADS * GLA_DK
GLA_VAL = GLA_HEADS * GLA_DV
GLA_GATE_RANK = 16
GLA_GATE_NORM = 16.0
CHUNK = 64
MLP_HIDDEN = 4 * D_MODEL
EPS = 1e-6
NEG_INF = -1e30
FORCE_SCORE = 1e9
IN_SPLITS = (NSA_HEADS * NSA_HEAD_DIM, NSA_KV, NSA_KV, NSA_KV, NSA_KV, NSA_KV, NSA_KV, NSA_HEADS * 3,
             SSD_INNER, SSD_XBC, SSD_HEADS,
             GDN_WIDTH, GDN_WIDTH, GDN_WIDTH, GDN_WIDTH, GDN_HEADS, GDN_HEADS,
             GLA_KEY, GLA_KEY, GLA_VAL, GLA_VAL, GLA_GATE_RANK)
IN_COLS = sum(IN_SPLITS)
MIX_OUT = NSA_HEADS * NSA_HEAD_DIM + SSD_INNER + GDN_WIDTH + GLA_VAL

LANE = 128
VMEM_LIMIT = 56 * 1024 * 1024
TOKENS = BATCH * SEQ
_IN_NAMES = ('nq', 'nkc', 'nvc', 'nks', 'nvs', 'nkw', 'nvw', 'ngate', 'sz', 'sxbc', 'sdt',
             'gq', 'gk', 'gv', 'gz', 'gbeta', 'ga', 'lq', 'lk', 'lv', 'lg', 'llr')
_IN_W = dict(zip(_IN_NAMES, IN_SPLITS))
_IN_OFF = {n: sum(IN_SPLITS[:i]) for i, n in enumerate(_IN_NAMES)}
_PB_SEGMENTS = (('sxbc', SSD_XBC), ('gq', GDN_WIDTH), ('gk', GDN_WIDTH), ('gv', GDN_WIDTH), ('gz', GDN_WIDTH),
                ('sz', SSD_INNER), ('lv', GLA_VAL), ('lg', GLA_VAL), ('lq', GLA_KEY), ('lk', GLA_KEY),
                ('nkc', NSA_KV), ('nvc', NSA_KV),
                ('ngate', NSA_HEADS * 3), ('sdt', SSD_HEADS), ('gbeta', 2 * GDN_HEADS), ('llr', GLA_GATE_RANK))
_PB_MISC = ('ngate', 'sdt', 'gbeta', 'llr')
PB, PBL = {}, {}
PROJ_B_USED = 0
for _n, _w in _PB_SEGMENTS:
    if _n in _PB_MISC:
        PB.setdefault('misc', PROJ_B_USED - PROJ_B_USED % LANE)
        PBL[_n] = PROJ_B_USED - PB['misc']
    else:
        assert PROJ_B_USED % _w == 0 and _w % LANE == 0
        PB[_n] = PROJ_B_USED
    PROJ_B_USED += _w
assert PROJ_B_USED - PB['misc'] <= LANE and PBL['ngate'] == 0 and _IN_OFF['ga'] == _IN_OFF['gbeta'] + GDN_HEADS
PROJ_B_TN = 512
PROJ_B_COLS = -(-PROJ_B_USED // PROJ_B_TN) * PROJ_B_TN

_bf16 = jnp.bfloat16
_f32 = jnp.float32


def _ada_kernel(c_ref, w_ref, b_ref, o_ref):
    c = c_ref[...]
    c_act = c * jax.nn.sigmoid(c)
    o_ref[0] = jnp.dot(c_act, w_ref[0], preferred_element_type=_f32) + b_ref[0]


def _ada_all(c, ada_w, ada_b):
    tn = 1024
    return pl.pallas_call(
        _ada_kernel,
        out_shape=jax.ShapeDtypeStruct((DEPTH, BATCH, 6 * D_MODEL), _f32),
        grid=(DEPTH, 6 * D_MODEL // tn),
        in_specs=[pl.BlockSpec((BATCH, D_MODEL), lambda l, j: (0, 0)),
                  pl.BlockSpec((1, D_MODEL, tn), lambda l, j: (l, 0, j)),
                  pl.BlockSpec((1, 1, tn), lambda l, j: (l, 0, j))],
        out_specs=pl.BlockSpec((1, BATCH, tn), lambda l, j: (l, 0, j)),
        compiler_params=pltpu.CompilerParams(dimension_semantics=("parallel", "parallel"),
                                             vmem_limit_bytes=VMEM_LIMIT),
        name="ada_mod",
    )(c, ada_w, ada_b.reshape(DEPTH, 1, 6 * D_MODEL))


def _rms_mod(x, w, sc, sh):
    y = x * lax.rsqrt(jnp.mean(x * x, axis=-1, keepdims=True) + EPS)
    return (y * w) * (1.0 + sc) + sh


def _norm_mod_kernel(x_ref, w_ref, sc_ref, sh_ref, o_ref):
    o_ref[...] = _rms_mod(x_ref[...], w_ref[...], sc_ref[0], sh_ref[0]).astype(o_ref.dtype)


def _norm_mod(x, w, sc, sh, tm=512):
    per_b = SEQ // tm
    return pl.pallas_call(
        _norm_mod_kernel,
        out_shape=jax.ShapeDtypeStruct((TOKENS, D_MODEL), _bf16),
        grid=(TOKENS // tm,),
        in_specs=[pl.BlockSpec((tm, D_MODEL), lambda i: (i, 0)),
                  pl.BlockSpec((1, D_MODEL), lambda i: (0, 0)),
                  pl.BlockSpec((1, 1, D_MODEL), lambda i: (i // per_b, 0, 0)),
                  pl.BlockSpec((1, 1, D_MODEL), lambda i: (i // per_b, 0, 0))],
        out_specs=pl.BlockSpec((tm, D_MODEL), lambda i: (i, 0)),
        compiler_params=pltpu.CompilerParams(dimension_semantics=("parallel",),
                                             vmem_limit_bytes=VMEM_LIMIT),
        name="norm_mod",
    )(x, w, sc, sh)


def _matmul_kernel(a_ref, w_ref, o_ref):
    o_ref[...] = jnp.dot(a_ref[...], w_ref[...], preferred_element_type=_f32).astype(o_ref.dtype)


def _wspec(layer, block, index):
    if layer is None:
        return pl.BlockSpec(block, index)
    return pl.BlockSpec((None,) + block, lambda *g: (layer,) + index(*g))


def _matmul(a, w, layer=None, tm=1024, tn=512, out_dtype=_f32):
    M, K = a.shape
    N = w.shape[-1]
    return pl.pallas_call(
        _matmul_kernel,
        out_shape=jax.ShapeDtypeStruct((M, N), out_dtype),
        grid=(M // tm, N // tn),
        in_specs=[pl.BlockSpec((tm, K), lambda i, j: (i, 0)),
                  _wspec(layer, (K, tn), lambda i, j: (0, j))],
        out_specs=pl.BlockSpec((tm, tn), lambda i, j: (i, j)),
        compiler_params=pltpu.CompilerParams(dimension_semantics=("parallel", "parallel"),
                                             vmem_limit_bytes=VMEM_LIMIT),
        name="in_proj",
    )(a, w)


def _out_proj_kernel(a0_ref, a1_ref, a2_ref, a3_ref, w_ref, x_ref, g_ref, nw_ref, sc_ref, sh_ref, xo_ref, ho_ref):
    y = None
    for i, a_ref in enumerate((a0_ref, a1_ref, a2_ref, a3_ref)):
        part = jnp.dot(a_ref[...], w_ref[i * MIX_GROUP:(i + 1) * MIX_GROUP, :], preferred_element_type=_f32)
        y = part if y is None else y + part
    xn = x_ref[...] + g_ref[0] * y
    xo_ref[...] = xn
    ho_ref[...] = _rms_mod(xn, nw_ref[...], sc_ref[0], sh_ref[0]).astype(ho_ref.dtype)


def _out_proj(mixed, w, x, g, nw, sc, sh, layer=None, tm=512):
    per_b = SEQ // tm
    bspec = pl.BlockSpec((1, 1, D_MODEL), lambda i: (i // per_b, 0, 0))
    aspec = pl.BlockSpec((tm, MIX_GROUP), lambda i: (i, 0))
    return pl.pallas_call(
        _out_proj_kernel,
        out_shape=(jax.ShapeDtypeStruct((TOKENS, D_MODEL), _f32),
                   jax.ShapeDtypeStruct((TOKENS, D_MODEL), _bf16)),
        grid=(TOKENS // tm,),
        in_specs=[aspec, aspec, aspec, aspec,
                  _wspec(layer, (MIX_OUT, D_MODEL), lambda i: (0, 0)),
                  pl.BlockSpec((tm, D_MODEL), lambda i: (i, 0)),
                  bspec,
                  pl.BlockSpec((1, D_MODEL), lambda i: (0, 0)),
                  bspec, bspec],
        out_specs=(pl.BlockSpec((tm, D_MODEL), lambda i: (i, 0)),
                   pl.BlockSpec((tm, D_MODEL), lambda i: (i, 0))),
        compiler_params=pltpu.CompilerParams(dimension_semantics=("parallel",),
                                             vmem_limit_bytes=VMEM_LIMIT),
        name="out_proj",
    )(*mixed, w, x, g, nw, sc, sh)


def _mlp_kernel(h_ref, w1_ref, w2_ref, x_ref, g_ref, nw_ref, sc_ref, sh_ref, o_ref, hn_ref, acc_ref):
    j = pl.program_id(1)

    @pl.when(j == 0)
    def _():
        acc_ref[...] = jnp.zeros_like(acc_ref)

    u = jnp.dot(h_ref[...], w1_ref[...], preferred_element_type=_f32)
    u = jnp.square(jnp.maximum(u, 0.0)).astype(_bf16)
    acc_ref[...] += jnp.dot(u, w2_ref[...], preferred_element_type=_f32)

    @pl.when(j == pl.num_programs(1) - 1)
    def _():
        xn = x_ref[...] + g_ref[0] * acc_ref[...]
        o_ref[...] = xn
        hn_ref[...] = _rms_mod(xn, nw_ref[...], sc_ref[0], sh_ref[0]).astype(hn_ref.dtype)


def _mlp(h, w1, w2, x, g, nw, sc, sh, next_dtype, layer=None, tm=512, th=1024):
    per_b = SEQ // tm
    bspec = pl.BlockSpec((1, 1, D_MODEL), lambda i, j: (i // per_b, 0, 0))
    xspec = pl.BlockSpec((tm, D_MODEL), lambda i, j: (i, 0))
    return pl.pallas_call(
        _mlp_kernel,
        out_shape=(jax.ShapeDtypeStruct((TOKENS, D_MODEL), _f32),
                   jax.ShapeDtypeStruct((TOKENS, D_MODEL), next_dtype)),
        grid=(TOKENS // tm, MLP_HIDDEN // th),
        in_specs=[xspec,
                  _wspec(layer, (D_MODEL, th), lambda i, j: (0, j)),
                  _wspec(layer, (th, D_MODEL), lambda i, j: (j, 0)),
                  xspec, bspec,
                  pl.BlockSpec((1, D_MODEL), lambda i, j: (0, 0)),
                  bspec, bspec],
        out_specs=(xspec, xspec),
        scratch_shapes=[pltpu.VMEM((tm, D_MODEL), _f32)],
        compiler_params=pltpu.CompilerParams(dimension_semantics=("parallel", "arbitrary"),
                                             vmem_limit_bytes=VMEM_LIMIT),
        name="mlp",
    )(h, w1, w2, x, g, nw, sc, sh)


NSA_R = NSA_HEADS // NSA_KV_GROUPS
NSA_CMP_ROWS = SEQ // CMP_STRIDE
NSA_NC = NSA_CMP_ROWS - CMP_BLOCK // CMP_STRIDE + 1
NSA_NSB = SEQ // SLC_BLOCK
NSA_NQB = SEQ // Q_BLOCK
NSA_SEQS_PER_STEP = 2
assert BATCH % NSA_SEQS_PER_STEP == 0
assert NSA_CMP_ROWS == LANE and Q_BLOCK == LANE and LANE % NSA_NSB == 0 and SLC_TOP_N <= NSA_NSB
assert NSA_KV_GROUPS * NSA_HEAD_DIM == LANE and CMP_BLOCK == 2 * CMP_STRIDE


def _bucket_value(tab_ref, h, rel):
    exact = REL_BUCKETS // 2
    n = jnp.maximum(rel, 0)
    large = exact + (jnp.log(jnp.maximum(n, 1).astype(_f32) / exact)
                     / math.log(REL_MAX_DIST / exact) * (REL_BUCKETS - exact)).astype(jnp.int32)
    bucket = jnp.where(n < exact, n, jnp.minimum(large, REL_BUCKETS - 1))
    val = jnp.full(rel.shape, tab_ref[0, h], _f32)
    for b in range(1, REL_BUCKETS):
        val = jnp.where(bucket == b, tab_ref[b, h], val)
    return val


def _nsa_bias_kernel(tab_ref, tb_ref, cb_ref):
    h = pl.program_id(0)
    kl = lax.broadcasted_iota(jnp.int32, (Q_BLOCK, LANE), 0)
    ql = lax.broadcasted_iota(jnp.int32, (Q_BLOCK, LANE), 1)
    for d in range(3):
        tb_ref[d, 0] = _bucket_value(tab_ref, h, ql - kl + d * Q_BLOCK)
    cmp_end = kl * CMP_STRIDE + (CMP_BLOCK - 1)
    for qb in range(NSA_NQB):
        cb_ref[qb, 0] = _bucket_value(tab_ref, h, qb * Q_BLOCK + ql - cmp_end)


def _nsa_bias_tiles(rel_bias):
    assert 2 * Q_BLOCK >= REL_MAX_DIST
    return pl.pallas_call(
        _nsa_bias_kernel,
        out_shape=(jax.ShapeDtypeStruct((3, NSA_HEADS, Q_BLOCK, LANE), _f32),
                   jax.ShapeDtypeStruct((NSA_NQB, NSA_HEADS, Q_BLOCK, LANE), _f32)),
        grid=(NSA_HEADS,),
        in_specs=[pl.BlockSpec(memory_space=pltpu.SMEM)],
        out_specs=(pl.BlockSpec((3, 1, Q_BLOCK, LANE), lambda h: (0, h, 0, 0)),
                   pl.BlockSpec((NSA_NQB, 1, Q_BLOCK, LANE), lambda h: (0, h, 0, 0))),
        compiler_params=pltpu.CompilerParams(dimension_semantics=("parallel",),
                                             vmem_limit_bytes=VMEM_LIMIT),
        name="nsa_bias_tiles",
    )(rel_bias)


def _nsa_cmp_kernel(tk_ref, tv_ref, pos_ref, w1_ref, w2_ref, kc_ref, vc_ref):
    rows = lax.broadcasted_iota(jnp.int32, (NSA_CMP_ROWS, LANE), 0)
    for idx, (t_ref, o_ref) in enumerate(((tk_ref, kc_ref), (tv_ref, vc_ref))):
        u = v = None
        for i in range(CMP_STRIDE):
            t_i = t_ref[pl.ds(i, NSA_CMP_ROWS, stride=CMP_STRIDE), :]
            ui = jnp.dot((t_i + pos_ref[idx, 0, i]).astype(_bf16), w1_ref[idx, 0, i], preferred_element_type=_f32)
            vi = jnp.dot((t_i + pos_ref[idx, 1, i]).astype(_bf16), w1_ref[idx, 1, i], preferred_element_type=_f32)
            u = ui if u is None else u + ui
            v = vi if v is None else v + vi
        pre = u + pltpu.roll(v, NSA_CMP_ROWS - 1, 0)
        act = pre * jax.nn.sigmoid(pre)
        out = jnp.dot(act.astype(_bf16), w2_ref[idx], preferred_element_type=_f32)
        o_ref[0] = jnp.where(rows < NSA_NC, out, 0.0).astype(_bf16)


def _nsa_compress(proj_b, pos_x, w1_x, w2_x):
    B = proj_b.shape[0] // SEQ
    ospec = pl.BlockSpec((1, NSA_CMP_ROWS, LANE), lambda b: (b, 0, 0))
    return pl.pallas_call(
        _nsa_cmp_kernel,
        out_shape=(jax.ShapeDtypeStruct((B, NSA_CMP_ROWS, LANE), _bf16),) * 2,
        grid=(B,),
        in_specs=[pl.BlockSpec((SEQ, LANE), lambda b: (b, PB['nkc'] // LANE)),
                  pl.BlockSpec((SEQ, LANE), lambda b: (b, PB['nvc'] // LANE)),
                  pl.BlockSpec(pos_x.shape, lambda b: (0,) * pos_x.ndim),
                  pl.BlockSpec(w1_x.shape, lambda b: (0,) * w1_x.ndim),
                  pl.BlockSpec((2, LANE, LANE), lambda b: (0, 0, 0))],
        out_specs=(ospec, ospec),
        compiler_params=pltpu.CompilerParams(dimension_semantics=("parallel",),
                                             vmem_limit_bytes=VMEM_LIMIT),
        name="nsa_compress",
    )(proj_b, proj_b, pos_x, w1_x, w2_x)


def _dot_nt(a, b):
    return lax.dot_general(a, b, (((1,), (1,)), ((), ())), preferred_element_type=_f32)


def _nsa_kernel(q_ref, ks_ref, vs_ref, kw_ref, vw_ref, kc_ref, vc_ref, gate_ref, tb_ref, cb_ref, ov_ref,
                o_ref, m_ref, l_ref, acc_ref, sel_ref):
    R, QB, hd = NSA_R, Q_BLOCK, NSA_HEAD_DIM
    qb = pl.program_id(1)
    kl = lax.broadcasted_iota(jnp.int32, (QB, LANE), 0)
    t_q = qb * QB + lax.broadcasted_iota(jnp.int32, (QB, LANE), 1)
    NSQ = q_ref.shape[0]
    G = NSA_KV_GROUPS
    KB = 2 * QB

    def split_r(x):
        return [x[:, r * LANE:(r + 1) * LANE] for r in range(R)]

    gates_t = [jax.nn.sigmoid(gate_ref[sq]).T for sq in range(NSQ)]
    qps = [[jnp.concatenate([q_ref[sq, :, (R * g + r) * LANE:(R * g + r + 1) * LANE] for r in range(R)], axis=0)
            for g in range(G)] for sq in range(NSQ)]

    krow = lax.broadcasted_iota(jnp.int32, (KB, LANE), 0)
    t_q2 = qb * QB + lax.broadcasted_iota(jnp.int32, (KB, LANE), 1)
    SEL, WIN = 0, 1
    chain = lambda sq, br, g: (sq * 2 + br) * G + g

    def flash_step(work):
        items, scores = [], []
        for p, br in work:
            off = pl.multiple_of(p * KB, KB)
            rel = t_q2 - (p * KB + krow)
            tidx = [jnp.clip(qb - (KB // QB) * p - i, 0, 2) for i in range(KB // QB)]
            for sq in range(NSQ):
                k_blk = (ks_ref, kw_ref)[br][sq, pl.ds(off, KB), :]
                for g in range(G):
                    items.append((p, br, sq, g, off, rel, tidx))
                    scores.append(split_r(_dot_nt(k_blk, qps[sq][g])))
        probs, alphas = [], []
        for (p, br, sq, g, off, rel, tidx), s_t in zip(items, scores):
            ch = chain(sq, br, g)
            if br == SEL:
                blocks = [jnp.broadcast_to(sel_ref[sq * G + g, pl.ds((KB // SLC_BLOCK) * p + i, 1), :],
                                           (SLC_BLOCK, LANE)) for i in range(KB // SLC_BLOCK)]
                mask = (rel >= 0) & (jnp.concatenate(blocks, axis=0) > 0.5)
            else:
                mask = (rel >= 0) & (rel < WINDOW)
            s_t = jnp.concatenate(
                [jnp.where(mask, s_t[r] + jnp.concatenate([tb_ref[ti, R * g + r] for ti in tidx], axis=0), NEG_INF)
                 for r in range(R)], axis=1)
            m_old = m_ref[ch]
            m_new = jnp.maximum(m_old, jnp.max(s_t, axis=0, keepdims=True))
            e = jnp.exp(s_t - m_new)
            alpha = jnp.exp(m_old - m_new)
            l_ref[ch] = alpha * l_ref[ch] + jnp.sum(e, axis=0, keepdims=True)
            m_ref[ch] = m_new
            probs.append(e.astype(_bf16))
            alphas.append(alpha)
        for (p, br, sq, g, off, rel, tidx), e, alpha in zip(items, probs, alphas):
            ch = chain(sq, br, g)
            pv = _dot_tn((vs_ref, vw_ref)[br][sq, pl.ds(off, KB), :], e)
            acc_ref[ch] = alpha * acc_ref[ch] + pv[g * hd:(g + 1) * hd]

    o_cmp = []
    mask_c = (t_q - (kl * CMP_STRIDE + CMP_BLOCK - 1) >= 0) & (kl < NSA_NC)
    for sq, g in [(sq, g) for sq in range(NSQ) for g in range(G)]:
        qp = qps[sq][g]
        s_t = split_r(_dot_nt(kc_ref[sq], qp))
        s_t = jnp.concatenate([jnp.where(mask_c, s_t[r] + cb_ref[0, R * g + r], NEG_INF) for r in range(R)], axis=1)
        e = split_r(jnp.exp(s_t - jnp.max(s_t, axis=0, keepdims=True)))
        e = jnp.concatenate([jnp.where(mask_c, e[r], 0.0) for r in range(R)], axis=1)
        den = jnp.sum(e, axis=0, keepdims=True)
        p = (e / jnp.where(den > 0.0, den, 1.0)).astype(_bf16)
        o_cmp.append(_dot_tn(vc_ref[sq], p)[g * hd:(g + 1) * hd])

        imp = sum(split_r(jnp.dot(ov_ref[...], p, preferred_element_type=_f32)))
        j = lax.broadcasted_iota(jnp.int32, (NSA_NSB, LANE), 0)
        cur = (qb * QB + lax.broadcasted_iota(jnp.int32, (NSA_NSB, LANE), 1)) // SLC_BLOCK
        imp = jnp.where((j == 0) | (j == cur) | (j == cur - 1), FORCE_SCORE, imp)
        imp = jnp.where(j <= cur, imp, NEG_INF)
        cnt = jnp.zeros((NSA_NSB, LANE), _f32)
        for jo in range(NSA_NSB):
            other = imp[jo:jo + 1, :]
            beats = (other > imp) | ((other == imp) & (j > jo))
            cnt = cnt + jnp.where(beats, 1.0, 0.0)
        sel_ref[sq * G + g] = jnp.where(cnt < SLC_TOP_N, 1.0, 0.0)

    p_hi = qb // (KB // QB) + 1
    n_win = p_hi - jnp.maximum(qb - WINDOW // QB, 0) // (KB // QB)
    m_ref[...] = jnp.full(m_ref.shape, NEG_INF, _f32)
    l_ref[...] = jnp.zeros(l_ref.shape, _f32)
    acc_ref[...] = jnp.zeros(acc_ref.shape, _f32)

    def both(i, carry):
        p = p_hi - 1 - i
        flash_step([(p, SEL), (p, WIN)])
        return carry

    def selected_pair(i, carry):
        p = p_hi - 1 - n_win - 2 * i
        flash_step([(p, SEL), (p - 1, SEL)])
        return carry

    def selected_one(i, carry):
        flash_step([(0, SEL)])
        return carry

    n_sel = p_hi - n_win
    lax.fori_loop(0, n_win, both, 0)
    lax.fori_loop(0, n_sel // 2, selected_pair, 0)
    lax.fori_loop(0, n_sel % 2, selected_one, 0)
    for sq in range(NSQ):
        heads = [None] * NSA_HEADS
        for g in range(G):
            cs, cw = chain(sq, SEL, g), chain(sq, WIN, g)
            o_c, o_s, o_w = split_r(o_cmp[sq * G + g]), split_r(acc_ref[cs] / l_ref[cs]), split_r(acc_ref[cw] / l_ref[cw])
            for r in range(R):
                h = R * g + r
                gt = gates_t[sq]
                heads[h] = (gt[3 * h:3 * h + 1] * o_c[r] + gt[3 * h + 1:3 * h + 2] * o_s[r]
                            + gt[3 * h + 2:3 * h + 3] * o_w[r])
        for pk in range(NSA_HEADS * hd // LANE):
            per = LANE // hd
            pair = jnp.concatenate(heads[per * pk:per * (pk + 1)], axis=0)
            o_ref[sq, :, pk * LANE:(pk + 1) * LANE] = pair.T.astype(o_ref.dtype)


def _nsa_attention(proj_a, proj_b, kc, vc, tb, cb, overlap_t):
    B = kc.shape[0]
    nsq = NSA_SEQS_PER_STEP
    nqb = NSA_NQB
    qcols = NSA_HEADS * LANE
    out_cols = NSA_HEADS * NSA_HEAD_DIM
    pa = proj_a.reshape(B, SEQ, proj_a.shape[1])
    pb = proj_b.reshape(B, SEQ, proj_b.shape[1])
    kv = lambda blk: pl.BlockSpec((nsq, SEQ, LANE), lambda b, q: (b, 0, qcols // LANE + blk))
    cspec = pl.BlockSpec((nsq, NSA_CMP_ROWS, LANE), lambda b, q: (b, 0, 0))
    nchain = nsq * 2 * NSA_KV_GROUPS
    out = pl.pallas_call(
        _nsa_kernel,
        out_shape=jax.ShapeDtypeStruct((B, SEQ, out_cols), _bf16),
        grid=(B // nsq, nqb),
        in_specs=[pl.BlockSpec((nsq, Q_BLOCK, qcols), lambda b, q: (b, q, 0)),
                  kv(0), kv(1), kv(2), kv(3), cspec, cspec,
                  pl.BlockSpec((nsq, Q_BLOCK, LANE), lambda b, q: (b, q, PB['misc'] // LANE)),
                  pl.BlockSpec((3, NSA_HEADS, Q_BLOCK, LANE), lambda b, q: (0, 0, 0, 0)),
                  pl.BlockSpec((1, NSA_HEADS, Q_BLOCK, LANE), lambda b, q: (q, 0, 0, 0)),
                  pl.BlockSpec((NSA_NSB, LANE), lambda b, q: (0, 0))],
        out_specs=pl.BlockSpec((nsq, Q_BLOCK, out_cols), lambda b, q: (b, q, 0)),
        scratch_shapes=[pltpu.VMEM((nchain, 1, NSA_R * Q_BLOCK), _f32),
                        pltpu.VMEM((nchain, 1, NSA_R * Q_BLOCK), _f32),
                        pltpu.VMEM((nchain, NSA_HEAD_DIM, NSA_R * Q_BLOCK), _f32),
                        pltpu.VMEM((nsq * NSA_KV_GROUPS, NSA_NSB, Q_BLOCK), _f32)],
        compiler_params=pltpu.CompilerParams(dimension_semantics=("parallel", "arbitrary"),
                                             vmem_limit_bytes=VMEM_LIMIT),
        name="nsa_attention",
    )(pa, pa, pa, pa, pa, kc, vc, pb, tb, cb, overlap_t)
    return out.reshape(B * SEQ, out_cols)


def _nsa_overlap():
    n = jnp.arange(NSA_CMP_ROWS)[None, :]
    jj = jnp.arange(NSA_NSB)[:, None]
    return ((n * CMP_STRIDE <= jj * SLC_BLOCK + SLC_BLOCK - 1)
            & (n * CMP_STRIDE + CMP_BLOCK - 1 >= jj * SLC_BLOCK) & (n < NSA_NC)).astype(_bf16)


def _nsa_weight_prep(w_in, cmp_pos, cmp_w1, cmp_w2):
    G, hd = NSA_KV_GROUPS, NSA_HEAD_DIM
    eye = jnp.eye(G, dtype=_f32)
    nq = NSA_HEADS * hd
    w_in = w_in.astype(_bf16)
    zeros = jnp.zeros((DEPTH, D_MODEL, hd), _bf16)
    q_cols = []
    for h in range(NSA_HEADS):
        wq_h = w_in[:, :, h * hd:(h + 1) * hd] * (hd ** -0.5)
        q_cols += [wq_h, zeros] if h // NSA_R == 0 else [zeros, wq_h]
    slc_win = w_in[:, :, nq + 2 * NSA_KV:nq + 6 * NSA_KV]
    w_a = jnp.concatenate(q_cols + [slc_win], axis=-1)
    w_b = jnp.concatenate([w_in[:, :, _IN_OFF[n]:_IN_OFF[n] + w] for n, w in _PB_SEGMENTS]
                          + [jnp.zeros((DEPTH, D_MODEL, PROJ_B_COLS - PROJ_B_USED), _bf16)], axis=-1)
    half = CMP_BLOCK // 2
    w1 = cmp_w1.reshape(DEPTH, 2, CMP_BLOCK, hd, hd)
    w1_x = jnp.einsum('lxide,gh->lxigdhe', w1, eye).reshape(DEPTH, 2, 2, half, G * hd, G * hd).astype(_bf16)
    w2_x = jnp.einsum('lxde,gh->lxgdhe', cmp_w2, eye).reshape(DEPTH, 2, G * hd, G * hd).astype(_bf16)
    pos_x = jnp.broadcast_to(cmp_pos[:, :, :, None, :], (DEPTH, 2, CMP_BLOCK, G, hd))
    pos_x = pos_x.reshape(DEPTH, 2, 2, half, 1, G * hd)
    return w_a, w_b, w1_x, w2_x, pos_x


REC_BLOCK = 2 * CHUNK
REC_SEQS_PER_STEP = 2
assert REC_BLOCK == LANE and SEQ % REC_BLOCK == 0 and BATCH % REC_SEQS_PER_STEP == 0


def _split3(x):
    hi = x.astype(_bf16)
    r = x - hi.astype(_f32)
    mid = r.astype(_bf16)
    lo = (r - mid.astype(_f32)).astype(_bf16)
    return hi, mid, lo


def _dot_exact_rhs(m, x):
    return sum(jnp.dot(m, p, preferred_element_type=_f32) for p in _split3(x))


def _dot_exact_lhs(x, m):
    return sum(jnp.dot(p, m, preferred_element_type=_f32) for p in _split3(x))


def _dot_tn(a, b):
    return lax.dot_general(a, b, (((0,), (0,)), ((), ())), preferred_element_type=_f32)


def _mm3(x, y):
    xh = x.astype(_bf16)
    xl = (x - xh.astype(_f32)).astype(_bf16)
    yh = y.astype(_bf16)
    yl = (y - yh.astype(_f32)).astype(_bf16)
    return (jnp.dot(xh, yh, preferred_element_type=_f32) + jnp.dot(xh, yl, preferred_element_type=_f32)
            + jnp.dot(xl, yh, preferred_element_type=_f32))


def _chunk_masks():
    ri = lax.broadcasted_iota(jnp.int32, (REC_BLOCK, REC_BLOCK), 0)
    ci = lax.broadcasted_iota(jnp.int32, (REC_BLOCK, REC_BLOCK), 1)
    same = (ri // CHUNK) == (ci // CHUNK)
    return ri, ci, (ci <= ri) & same, (ci < ri) & same, (ri <= ci) & same


def _as_mxu(mask):
    return jnp.where(mask, 1.0, 0.0).astype(_bf16)


def _softplus(x):
    return jnp.maximum(x, 0.0) + jnp.log1p(jnp.exp(-jnp.abs(x)))


def _silu(x):
    return x * jax.nn.sigmoid(x)


def _chunk_last(x):
    ri = lax.broadcasted_iota(jnp.int32, x.shape, 0)
    return jnp.where(ri < CHUNK, x[CHUNK - 1:CHUNK], x[2 * CHUNK - 1:2 * CHUNK])


def _expand_heads(x, lane0, nheads, width):
    per = LANE // width
    lane = lax.broadcasted_iota(jnp.int32, (x.shape[0], LANE), 1)
    pieces = []
    for p0 in range(lane0, lane0 + nheads, per):
        piece = jnp.broadcast_to(x[:, p0:p0 + 1], (x.shape[0], LANE))
        for k in range(1, per):
            piece = jnp.where(lane < k * width, piece, jnp.broadcast_to(x[:, p0 + k:p0 + k + 1], (x.shape[0], LANE)))
        pieces.append(piece)
    return jnp.concatenate(pieces, axis=1) if len(pieces) > 1 else pieces[0]


def _conv_silu(x, prev, w_ref, c0, bias=None):
    n, C = x.shape
    ntap = w_ref.shape[0]
    rows = lax.broadcasted_iota(jnp.int32, (n, C), 0)
    acc = x * w_ref[ntap - 1:ntap, c0:c0 + C]
    for s in range(1, ntap):
        xs = jnp.where(rows < s, pltpu.roll(prev, s, 0), pltpu.roll(x, s, 0))
        acc = acc + xs * w_ref[ntap - 1 - s:ntap - s, c0:c0 + C]
    if bias is not None:
        acc = acc + bias
    return _silu(acc)


def _rms(x, w):
    return x * lax.rsqrt(jnp.mean(x * x, axis=-1, keepdims=True) + EPS) * w


def _ssd_kernel(z_ref, xbc_ref, dt_ref, cw_ref, cb_ref, dtb_ref, alog_ref, dskip_ref, nw_ref, o_ref,
                prev_ref, state_ref):
    @pl.when(pl.program_id(1) == 0)
    def _():
        prev_ref[...] = jnp.zeros(prev_ref.shape, _f32)
        state_ref[...] = jnp.zeros(state_ref.shape, _f32)

    for ns in range(z_ref.shape[0]):
        _ssd_block(z_ref.at[ns], xbc_ref.at[ns], dt_ref.at[ns], cw_ref, cb_ref, dtb_ref, alog_ref, dskip_ref, nw_ref,
                   o_ref.at[ns], prev_ref.at[ns], state_ref.at[ns])


def _ssd_block(z_ref, xbc_ref, dt_ref, cw_ref, cb_ref, dtb_ref, alog_ref, dskip_ref, nw_ref, o_ref,
               prev_ref, state_ref):
    G, R, P, N = SSD_GROUPS, SSD_HEADS // SSD_GROUPS, SSD_HEAD_DIM, SSD_STATE
    x_in = xbc_ref[...]
    xc = _conv_silu(x_in, prev_ref[...], cw_ref, 0, cb_ref[...])
    prev_ref[...] = x_in
    _, _, tril, _, triu = _chunk_masks()
    lane = lax.broadcasted_iota(jnp.int32, (REC_BLOCK, LANE), 1)
    L0 = PBL['sdt']
    dt = _softplus(dt_ref[...] + dtb_ref[...])
    da = dt * (-jnp.exp(alog_ref[...]))
    a_cum = _dot_exact_rhs(_as_mxu(tril), da)
    a_cum_t = _dot_exact_lhs(da.T, _as_mxu(triu))
    a_last = _chunk_last(a_cum)
    xs = xc[:, :SSD_INNER]
    xdt = xs * _expand_heads(dt, L0, SSD_HEADS, P)
    xdtd = (xdt * _expand_heads(jnp.exp(a_last - a_cum), L0, SSD_HEADS, P)).astype(_bf16)
    xdt_b = xdt.astype(_bf16)
    ea = _expand_heads(jnp.exp(a_cum), L0, SSD_HEADS, P)
    y_groups = []
    for g in range(G):
        bm = xc[:, SSD_INNER + g * N:SSD_INNER + (g + 1) * N].astype(_bf16)
        cm = xc[:, SSD_INNER + (G + g) * N:SSD_INNER + (G + g + 1) * N].astype(_bf16)
        cbm = _dot_nt(cm, bm)
        intra = []
        for pr in range(R // 2):
            both = []
            for k in range(2):
                h = g * R + 2 * pr + k
                seg = jnp.exp(jnp.where(tril, a_cum[:, L0 + h:L0 + h + 1] - a_cum_t[L0 + h:L0 + h + 1, :], -jnp.inf))
                both.append(jnp.dot((cbm * seg).astype(_bf16), xdt_b[:, (h - k) * P:(h - k + 2) * P],
                                    preferred_element_type=_f32))
            intra.append(jnp.where(lane < P, both[0], both[1]))
        y_intra = jnp.concatenate(intra, axis=1)
        prev_rows = []
        for c in range(REC_BLOCK // CHUNK):
            rows = slice(c * CHUNK, (c + 1) * CHUNK)
            st = state_ref[g]
            prev_rows.append(jnp.dot(cm[rows], st.astype(_bf16), preferred_element_type=_f32))
            dec = _expand_heads(jnp.exp(a_cum[(c + 1) * CHUNK - 1:(c + 1) * CHUNK]), L0, SSD_HEADS, P)
            state_ref[g] = (st * dec[:, g * R * P:(g + 1) * R * P]
                            + _dot_tn(bm[rows], xdtd[rows, g * R * P:(g + 1) * R * P]))
        y_groups.append(y_intra + jnp.concatenate(prev_rows, axis=0) * ea[:, g * R * P:(g + 1) * R * P])
    y = jnp.concatenate(y_groups, axis=1) + xs * dskip_ref[...]
    y = y * _silu(z_ref[...])
    gw = SSD_INNER // G
    for g in range(G):
        o_ref[:, g * gw:(g + 1) * gw] = _rms(y[:, g * gw:(g + 1) * gw], nw_ref[:, g * gw:(g + 1) * gw]).astype(o_ref.dtype)


def _gdn_kernel(q_ref, k_ref, v_ref, z_ref, ba_ref, cw_ref, dtb_ref, alog_ref, nw_ref, o_ref,
                pq_ref, pk_ref, pv_ref, state_ref):
    H, Dh = GDN_HEADS, GDN_HEAD_DIM

    @pl.when(pl.program_id(1) == 0)
    def _():
        for r in (pq_ref, pk_ref, pv_ref, state_ref):
            r[...] = jnp.zeros(r.shape, _f32)

    NS = q_ref.shape[0]
    ri, ci, tril, strict, triu = _chunk_masks()
    q, k, v, beta, gcum, gcum_t, glast = [], [], [], [], [], [], []
    for ns in range(NS):
        for dst, x_ref, p_ref, c0 in ((q, q_ref, pq_ref, 0), (k, k_ref, pk_ref, GDN_WIDTH), (v, v_ref, pv_ref, 2 * GDN_WIDTH)):
            x_in = x_ref[ns]
            dst.append(_conv_silu(x_in, p_ref[ns], cw_ref, c0))
            p_ref[ns] = x_in
        ba = ba_ref[ns]
        beta.append(jax.nn.sigmoid(ba))
        gl = -jnp.exp(alog_ref[...]) * _softplus(ba + dtb_ref[...])
        gcum.append(_dot_exact_rhs(_as_mxu(tril), gl))
        gcum_t.append(_dot_exact_lhs(gl.T, _as_mxu(triu)))
        glast.append(_chunk_last(gcum[ns]))
    zero_rows = jnp.zeros((CHUNK, Dh), _f32)
    eye = jnp.where(tril & jnp.logical_not(strict), 1.0, 0.0)
    hs = range(NS * H)
    seq = [i // H for i in hs]
    sls = [slice((i % H) * Dh, (i % H + 1) * Dh) for i in hs]
    qh = [q[seq[i]][:, sls[i]] for i in hs]
    kh = [k[seq[i]][:, sls[i]] for i in hs]
    qq = [qh[i] * lax.rsqrt(jnp.sum(qh[i] * qh[i], axis=-1, keepdims=True) + EPS) * (Dh ** -0.5) for i in hs]
    kk = [kh[i] * lax.rsqrt(jnp.sum(kh[i] * kh[i], axis=-1, keepdims=True) + EPS) for i in hs]
    lb = [PBL['gbeta'] + i % H for i in hs]
    lg = [PBL['gbeta'] + H + i % H for i in hs]
    bcol = [beta[seq[h]][:, lb[h]:lb[h] + 1] for h in hs]
    gcol = [gcum[seq[h]][:, lg[h]:lg[h] + 1] for h in hs]
    decay = [jnp.exp(jnp.where(tril, gcol[h] - gcum_t[seq[h]][lg[h]:lg[h] + 1, :], -jnp.inf)) for h in hs]
    kb = [kk[h] * bcol[h] for h in hs]
    s = [_dot_nt(jnp.concatenate([kb[h], qq[h]], axis=0).astype(_bf16), kk[h].astype(_bf16)) for h in hs]
    a_mat = [jnp.where(strict, s[h][:REC_BLOCK] * decay[h], 0.0) for h in hs]
    aqk = [(s[h][REC_BLOCK:] * decay[h]).astype(_bf16) for h in hs]
    SUB = 8
    same = lambda n: (ri // n) == (ci // n)
    a_sub = [jnp.where(same(SUB), a_mat[h], 0.0) for h in hs]
    tinv = [eye - a_sub[h] for h in hs]
    pw = a_sub
    for _ in range(SUB.bit_length() - 2):
        pw = [_mm3(pw[h], pw[h]) for h in hs]
        tinv = [tinv[h] + _mm3(tinv[h], pw[h]) for h in hs]
    n = SUB
    while n < CHUNK:
        enclosed = same(2 * n) & jnp.logical_not(same(n))
        tc = [_mm3(tinv[h], jnp.where(enclosed, a_mat[h], 0.0)) for h in hs]
        tinv = [tinv[h] - _mm3(tc[h], tinv[h]) for h in hs]
        n *= 2
    sol = [_mm3(tinv[h], jnp.concatenate([v[seq[h]][:, sls[h]] * bcol[h], kb[h] * jnp.exp(gcol[h])], axis=1))
           for h in hs]
    u = [sol[h][:, :Dh] for h in hs]
    w = [sol[h][:, Dh:].astype(_bf16) for h in hs]
    q_dec = [(qq[h] * jnp.exp(gcol[h])).astype(_bf16) for h in hs]
    k_end = [(kk[h] * jnp.exp(glast[seq[h]][:, lg[h]:lg[h] + 1] - gcol[h])).astype(_bf16) for h in hs]
    o_rows = [[] for _ in hs]
    for c in range(REC_BLOCK // CHUNK):
        rows = slice(c * CHUNK, (c + 1) * CHUNK)
        st = [state_ref[h] for h in hs]
        st_b = [st[h].astype(_bf16) for h in hs]
        v_new = [u[h][rows] - jnp.dot(w[h][rows], st_b[h], preferred_element_type=_f32) for h in hs]
        v_full = [jnp.concatenate([v_new[h], zero_rows] if c == 0 else [zero_rows, v_new[h]], axis=0).astype(_bf16)
                  for h in hs]
        for h in hs:
            o_rows[h].append(jnp.dot(q_dec[h][rows], st_b[h], preferred_element_type=_f32)
                             + jnp.dot(aqk[h][rows], v_full[h], preferred_element_type=_f32))
            d_last = jnp.exp(gcum[seq[h]][(c + 1) * CHUNK - 1:(c + 1) * CHUNK, lg[h]:lg[h] + 1])
            state_ref[h] = st[h] * d_last + _dot_tn(k_end[h][rows], v_new[h].astype(_bf16))
    for h in hs:
        o = _rms(jnp.concatenate(o_rows[h], axis=0), nw_ref[...]) * _silu(z_ref[seq[h], :, sls[h]])
        o_ref[seq[h], :, sls[h]] = o.astype(o_ref.dtype)


def _gla_kernel(q_ref, k_ref, v_ref, go_ref, lr_ref, w2_ref, gb_ref, nw_ref, o_ref, state_ref):
    @pl.when(pl.program_id(1) == 0)
    def _():
        state_ref[...] = jnp.zeros(state_ref.shape, _f32)

    for ns in range(q_ref.shape[0]):
        _gla_block(q_ref.at[ns], k_ref.at[ns], v_ref.at[ns], go_ref.at[ns], lr_ref.at[ns], w2_ref, gb_ref, nw_ref,
                   o_ref.at[ns], state_ref.at[ns])


def _gla_block(q_ref, k_ref, v_ref, go_ref, lr_ref, w2_ref, gb_ref, nw_ref, o_ref, state_ref):
    H, Dk, Dv = GLA_HEADS, GLA_DK, GLA_DV
    _, _, tril, _, _ = _chunk_masks()
    lane = lax.broadcasted_iota(jnp.int32, (REC_BLOCK, LANE), 1)
    pre = jnp.dot(lr_ref[...].astype(_bf16), w2_ref[...], preferred_element_type=_f32) + gb_ref[...]
    gk = (jnp.minimum(pre, 0.0) - jnp.log1p(jnp.exp(-jnp.abs(pre)))) / GLA_GATE_NORM
    bcum = _dot_exact_rhs(_as_mxu(tril), gk)
    blast = _chunk_last(bcum)
    q_dec = q_ref[...] * (Dk ** -0.5) * jnp.exp(bcum)
    k_inv = (k_ref[...] * jnp.exp(-bcum)).astype(_bf16)
    k_end = (k_ref[...] * jnp.exp(blast - bcum)).astype(_bf16)
    per = LANE // Dk
    nchunk = REC_BLOCK // CHUNK
    chunk_rows = [slice(c * CHUNK, (c + 1) * CHUNK) for c in range(nchunk)]
    psl = [slice(pr * LANE, (pr + 1) * LANE) for pr in range(H // per)]
    qm = [jnp.where(lane // Dk == h % per, q_dec[:, psl[h // per]], 0.0).astype(_bf16) for h in range(H)]
    vh = [v_ref[:, h * Dv:(h + 1) * Dv].astype(_bf16) for h in range(H)]
    attn = [jnp.where(tril, _dot_nt(qm[h], k_inv[:, psl[h // per]]), 0.0).astype(_bf16) for h in range(H)]
    o_intra = [jnp.dot(attn[h], vh[h], preferred_element_type=_f32) for h in range(H)]
    local = []
    for rows in chunk_rows:
        per_pair = []
        for pr in range(H // per):
            loc = None
            for k in range(per):
                lk = _dot_tn(vh[pr * per + k][rows], k_end[rows, psl[pr]])
                loc = lk if loc is None else jnp.where(lane < k * Dk, loc, lk)
            per_pair.append(loc)
        local.append(per_pair)
    o_prev = [[] for _ in range(H)]
    for c, rows in enumerate(chunk_rows):
        for pr in range(H // per):
            st = state_ref[pr]
            st_b = st.astype(_bf16)
            for k in range(per):
                o_prev[pr * per + k].append(_dot_nt(qm[pr * per + k][rows], st_b))
            state_ref[pr] = st * jnp.exp(bcum[(c + 1) * CHUNK - 1:(c + 1) * CHUNK, psl[pr]]) + local[c][pr]
    for h in range(H):
        o = o_intra[h] + jnp.concatenate(o_prev[h], axis=0)
        o = _rms(o, nw_ref[...]) * _silu(go_ref[:, h * Dv:(h + 1) * Dv])
        o_ref[:, h * Dv:(h + 1) * Dv] = o.astype(o_ref.dtype)


def _rec_call(body, proj_b, col_blocks, params, out_cols, scratch, name, nseq=None):
    B = proj_b.shape[0] // SEQ
    nblk = SEQ // REC_BLOCK
    if nseq is None:
        src, grid = proj_b, (B, nblk)
        in_specs = [pl.BlockSpec((REC_BLOCK, w), (lambda b, t, c=c0 // w: (b * nblk + t, c))) for c0, w in col_blocks]
        out_shape = jax.ShapeDtypeStruct((B * SEQ, out_cols), _bf16)
        out_spec = pl.BlockSpec((REC_BLOCK, out_cols), lambda b, t: (b * nblk + t, 0))
    else:
        src, grid = proj_b.reshape(B, SEQ, proj_b.shape[1]), (B // nseq, nblk)
        in_specs = [pl.BlockSpec((nseq, REC_BLOCK, w), (lambda b, t, c=c0 // w: (b, t, c))) for c0, w in col_blocks]
        out_shape = jax.ShapeDtypeStruct((B, SEQ, out_cols), _bf16)
        out_spec = pl.BlockSpec((nseq, REC_BLOCK, out_cols), lambda b, t: (b, t, 0))
    for p in params:
        in_specs.append(pl.BlockSpec(p.shape, lambda b, t, nd=p.ndim: (0,) * nd))
    out = pl.pallas_call(
        body,
        out_shape=out_shape,
        grid=grid,
        in_specs=in_specs,
        out_specs=out_spec,
        scratch_shapes=scratch,
        compiler_params=pltpu.CompilerParams(dimension_semantics=("parallel", "arbitrary"),
                                             vmem_limit_bytes=VMEM_LIMIT),
        name=name,
    )(*([src] * len(col_blocks)), *params)
    return out.reshape(B * SEQ, out_cols)


def _lane_pad(v, lane0=0):
    return jnp.pad(v.astype(_f32), (lane0, LANE - lane0 - v.shape[0]))[None]


def _ssd_call(proj_b, conv_w, conv_b, dt_bias, a_log, d_skip, norm_w):
    cols = [(PB['sz'], SSD_INNER), (PB['sxbc'], SSD_XBC), (PB['misc'], LANE)]
    params = [conv_w, conv_b[None], _lane_pad(dt_bias, PBL['sdt']), _lane_pad(a_log, PBL['sdt']),
              jnp.repeat(d_skip, SSD_HEAD_DIM)[None], norm_w[None]]
    ns = REC_SEQS_PER_STEP
    scratch = [pltpu.VMEM((ns, REC_BLOCK, SSD_XBC), _f32),
               pltpu.VMEM((ns, SSD_GROUPS, SSD_STATE, SSD_INNER // SSD_GROUPS), _f32)]
    return _rec_call(_ssd_kernel, proj_b, cols, params, SSD_INNER, scratch, "ssd", nseq=ns)


def _gdn_call(proj_b, conv_w, dt_bias, a_log, norm_w):
    W = GDN_WIDTH
    cols = [(PB['gq'], W), (PB['gk'], W), (PB['gv'], W), (PB['gz'], W), (PB['misc'], LANE)]
    decay_lane = PBL['gbeta'] + GDN_HEADS
    params = [conv_w, _lane_pad(dt_bias, decay_lane), _lane_pad(a_log, decay_lane), norm_w[None]]
    ns = REC_SEQS_PER_STEP
    scratch = ([pltpu.VMEM((ns, REC_BLOCK, W), _f32)] * 3
               + [pltpu.VMEM((ns * GDN_HEADS, GDN_HEAD_DIM, GDN_HEAD_DIM), _f32)])
    return _rec_call(_gdn_kernel, proj_b, cols, params, W, scratch, "gdn", nseq=ns)


def _gla_call(proj_b, gate_w2, gate_b, norm_w):
    cols = [(PB['lq'], GLA_KEY), (PB['lk'], GLA_KEY), (PB['lv'], GLA_VAL), (PB['lg'], GLA_VAL), (PB['misc'], LANE)]
    w2 = jnp.pad(gate_w2, ((PBL['llr'], LANE - PBL['llr'] - GLA_GATE_RANK), (0, 0))).astype(_bf16)
    params = [w2, gate_b[None], norm_w[None]]
    ns = REC_SEQS_PER_STEP
    scratch = [pltpu.VMEM((ns, GLA_HEADS * GLA_DK // LANE, GLA_DV, LANE), _f32)]
    return _rec_call(_gla_kernel, proj_b, cols, params, GLA_VAL, scratch, "gla", nseq=ns)


def kernel(x, c, rel_bias, norm1_w, norm2_w, ada_w, ada_b, w_in, w_out, nsa_cmp_pos, nsa_cmp_w1, nsa_cmp_w2, ssd_conv_w, ssd_conv_b, ssd_dt_bias, ssd_a_log, ssd_d, ssd_norm_w, gdn_conv_w, gdn_dt_bias, gdn_a_log, gdn_norm_w, gla_gate_w2, gla_gate_b, gla_norm_w, mlp_w1, mlp_w2, final_norm_w):
    B, S, D = x.shape
    mod = _ada_all(c, ada_w, ada_b).reshape(DEPTH, B, 6, 1, D)
    w_a, w_b, cmp_w1_x, cmp_w2_x, cmp_pos_x = _nsa_weight_prep(w_in, nsa_cmp_pos, nsa_cmp_w1, nsa_cmp_w2)
    w_out_b = w_out.astype(_bf16)
    w1_b = mlp_w1.astype(_bf16)
    w2_b = mlp_w2.astype(_bf16)
    tb, cb = _nsa_bias_tiles(rel_bias)
    overlap_t = _nsa_overlap()
    xf = x.reshape(TOKENS, D)
    no_mod = jnp.zeros((B, 1, D), _f32)
    h = _norm_mod(xf, norm1_w[0][None], mod[0, :, 1], mod[0, :, 0])
    for l in range(DEPTH):
        sh1, sc1, g1, sh2, sc2, g2 = (mod[l, :, i] for i in range(6))
        proj_a = _matmul(h, w_a, l, out_dtype=_bf16)
        proj_b = _matmul(h, w_b, l, tn=PROJ_B_TN)
        kc, vc = _nsa_compress(proj_b, cmp_pos_x[l], cmp_w1_x[l], cmp_w2_x[l])
        y_nsa = _nsa_attention(proj_a, proj_b, kc, vc, tb, cb, overlap_t)
        y_ssd = _ssd_call(proj_b, ssd_conv_w[l], ssd_conv_b[l], ssd_dt_bias[l], ssd_a_log[l], ssd_d[l], ssd_norm_w[l])
        y_gdn = _gdn_call(proj_b, gdn_conv_w[l], gdn_dt_bias[l], gdn_a_log[l], gdn_norm_w[l])
        y_gla = _gla_call(proj_b, gla_gate_w2[l], gla_gate_b[l], gla_norm_w[l])
        xf, h2 = _out_proj((y_nsa, y_ssd, y_gdn, y_gla), w_out_b, xf, g1,
                           norm2_w[l][None], sc2, sh2, layer=l)
        if l + 1 < DEPTH:
            xf, h = _mlp(h2, w1_b, w2_b, xf, g2, norm1_w[l + 1][None], mod[l + 1, :, 1], mod[l + 1, :, 0], _bf16,
                         layer=l)
        else:
            _, out = _mlp(h2, w1_b, w2_b, xf, g2, final_norm_w[None], no_mod, no_mod, _f32, layer=l)
    return out.reshape(B, S, D)
```

```python
import math
from functools import partial

import jax
import jax.numpy as jnp
from jax import lax
from jax.experimental import pallas as pl
from jax.experimental.pallas import tpu as pltpu

D_MODEL = 2048
BATCH = 16
SEQ = 2048
DEPTH = 4

MIX_GROUP = D_MODEL // 4
NSA_HEAD_DIM = 64
NSA_HEADS = MIX_GROUP // NSA_HEAD_DIM
NSA_KV_GROUPS = max(1, NSA_HEADS // 4)
NSA_KV = NSA_KV_GROUPS * NSA_HEAD_DIM
CMP_BLOCK = 32
CMP_STRIDE = 16
SLC_BLOCK = 64
SLC_TOP_N = 8
WINDOW = 512
Q_BLOCK = 128
REL_BUCKETS = 32
REL_MAX_DIST = 128
SSD_HEAD_DIM = 64
SSD_HEADS = MIX_GROUP // SSD_HEAD_DIM
SSD_INNER = SSD_HEADS * SSD_HEAD_DIM
SSD_GROUPS = 2
SSD_STATE = 128
SSD_CONV = 4
SSD_XBC = SSD_INNER + 2 * SSD_GROUPS * SSD_STATE
GDN_HEAD_DIM = 128
GDN_HEADS = MIX_GROUP // GDN_HEAD_DIM
GDN_WIDTH = GDN_HEADS * GDN_HEAD_DIM
GDN_CONV = 4
GLA_DV = 128
GLA_HEADS = MIX_GROUP // GLA_DV
GLA_DK = GLA_DV // 2
GLA_KEY = GLA_HEADS * GLA_DK
GLA_VAL = GLA_HEADS * GLA_DV
GLA_GATE_RANK = 16
GLA_GATE_NORM = 16.0
CHUNK = 64
MLP_HIDDEN = 4 * D_MODEL
EPS = 1e-6
NEG_INF = -1e30
FORCE_SCORE = 1e9
IN_SPLITS = (NSA_HEADS * NSA_HEAD_DIM, NSA_KV, NSA_KV, NSA_KV, NSA_KV, NSA_KV, NSA_KV, NSA_HEADS * 3,
             SSD_INNER, SSD_XBC, SSD_HEADS,
             GDN_WIDTH, GDN_WIDTH, GDN_WIDTH, GDN_WIDTH, GDN_HEADS, GDN_HEADS,
             GLA_KEY, GLA_KEY, GLA_VAL, GLA_VAL, GLA_GATE_RANK)
IN_COLS = sum(IN_SPLITS)
MIX_OUT = NSA_HEADS * NSA_HEAD_DIM + SSD_INNER + GDN_WIDTH + GLA_VAL

LANE = 128
VMEM_LIMIT = 56 * 1024 * 1024
TOKENS = BATCH * SEQ
_IN_NAMES = ('nq', 'nkc', 'nvc', 'nks', 'nvs', 'nkw', 'nvw', 'ngate', 'sz', 'sxbc', 'sdt',
             'gq', 'gk', 'gv', 'gz', 'gbeta', 'ga', 'lq', 'lk', 'lv', 'lg', 'llr')
_IN_W = dict(zip(_IN_NAMES, IN_SPLITS))
_IN_OFF = {n: sum(IN_SPLITS[:i]) for i, n in enumerate(_IN_NAMES)}
_PB_SEGMENTS = (('sxbc', SSD_XBC), ('gq', GDN_WIDTH), ('gk', GDN_WIDTH), ('gv', GDN_WIDTH), ('gz', GDN_WIDTH),
                ('sz', SSD_INNER), ('lv', GLA_VAL), ('lg', GLA_VAL), ('lq', GLA_KEY), ('lk', GLA_KEY),
                ('nkc', NSA_KV), ('nvc', NSA_KV),
                ('ngate', NSA_HEADS * 3), ('sdt', SSD_HEADS), ('gbeta', 2 * GDN_HEADS), ('llr', GLA_GATE_RANK))
_PB_MISC = ('ngate', 'sdt', 'gbeta', 'llr')
PB, PBL = {}, {}
PROJ_B_USED = 0
for _n, _w in _PB_SEGMENTS:
    if _n in _PB_MISC:
        PB.setdefault('misc', PROJ_B_USED - PROJ_B_USED % LANE)
        PBL[_n] = PROJ_B_USED - PB['misc']
    else:
        assert PROJ_B_USED % _w == 0 and _w % LANE == 0
        PB[_n] = PROJ_B_USED
    PROJ_B_USED += _w
assert PROJ_B_USED - PB['misc'] <= LANE and PBL['ngate'] == 0 and _IN_OFF['ga'] == _IN_OFF['gbeta'] + GDN_HEADS
PROJ_B_TN = 512
PROJ_B_COLS = -(-PROJ_B_USED // PROJ_B_TN) * PROJ_B_TN

_bf16 = jnp.bfloat16
_f32 = jnp.float32


def _ada_kernel(c_ref, w_ref, b_ref, o_ref):
    c = c_ref[...]
    c_act = c * jax.nn.sigmoid(c)
    o_ref[0] = jnp.dot(c_act, w_ref[0], preferred_element_type=_f32) + b_ref[0]


def _ada_all(c, ada_w, ada_b):
    tn = 1024
    return pl.pallas_call(
        _ada_kernel,
        out_shape=jax.ShapeDtypeStruct((DEPTH, BATCH, 6 * D_MODEL), _f32),
        grid=(DEPTH, 6 * D_MODEL // tn),
        in_specs=[pl.BlockSpec((BATCH, D_MODEL), lambda l, j: (0, 0)),
                  pl.BlockSpec((1, D_MODEL, tn), lambda l, j: (l, 0, j)),
                  pl.BlockSpec((1, 1, tn), lambda l, j: (l, 0, j))],
        out_specs=pl.BlockSpec((1, BATCH, tn), lambda l, j: (l, 0, j)),
        compiler_params=pltpu.CompilerParams(dimension_semantics=("parallel", "parallel"),
                                             vmem_limit_bytes=VMEM_LIMIT),
        name="ada_mod",
    )(c, ada_w, ada_b.reshape(DEPTH, 1, 6 * D_MODEL))


def _rms_mod(x, w, sc, sh):
    y = x * lax.rsqrt(jnp.mean(x * x, axis=-1, keepdims=True) + EPS)
    return (y * w) * (1.0 + sc) + sh


def _norm_mod_kernel(x_ref, w_ref, sc_ref, sh_ref, o_ref):
    o_ref[...] = _rms_mod(x_ref[...], w_ref[...], sc_ref[0], sh_ref[0]).astype(o_ref.dtype)


def _norm_mod(x, w, sc, sh, tm=512):
    per_b = SEQ // tm
    return pl.pallas_call(
        _norm_mod_kernel,
        out_shape=jax.ShapeDtypeStruct((TOKENS, D_MODEL), _bf16),
        grid=(TOKENS // tm,),
        in_specs=[pl.BlockSpec((tm, D_MODEL), lambda i: (i, 0)),
                  pl.BlockSpec((1, D_MODEL), lambda i: (0, 0)),
                  pl.BlockSpec((1, 1, D_MODEL), lambda i: (i // per_b, 0, 0)),
                  pl.BlockSpec((1, 1, D_MODEL), lambda i: (i // per_b, 0, 0))],
        out_specs=pl.BlockSpec((tm, D_MODEL), lambda i: (i, 0)),
        compiler_params=pltpu.CompilerParams(dimension_semantics=("parallel",),
                                             vmem_limit_bytes=VMEM_LIMIT),
        name="norm_mod",
    )(x, w, sc, sh)


def _matmul_kernel(a_ref, w_ref, o_ref):
    o_ref[...] = jnp.dot(a_ref[...], w_ref[...], preferred_element_type=_f32).astype(o_ref.dtype)


def _wspec(layer, block, index):
    if layer is None:
        return pl.BlockSpec(block, index)
    return pl.BlockSpec((None,) + block, lambda *g: (layer,) + index(*g))


def _matmul(a, w, layer=None, tm=1024, tn=512, out_dtype=_f32):
    M, K = a.shape
    N = w.shape[-1]
    return pl.pallas_call(
        _matmul_kernel,
        out_shape=jax.ShapeDtypeStruct((M, N), out_dtype),
        grid=(M // tm, N // tn),
        in_specs=[pl.BlockSpec((tm, K), lambda i, j: (i, 0)),
                  _wspec(layer, (K, tn), lambda i, j: (0, j))],
        out_specs=pl.BlockSpec((tm, tn), lambda i, j: (i, j)),
        compiler_params=pltpu.CompilerParams(dimension_semantics=("parallel", "parallel"),
                                             vmem_limit_bytes=VMEM_LIMIT),
        name="in_proj",
    )(a, w)


def _out_proj_kernel(a0_ref, a1_ref, a2_ref, a3_ref, w_ref, x_ref, g_ref, nw_ref, sc_ref, sh_ref, xo_ref, ho_ref):
    y = None
    for i, a_ref in enumerate((a0_ref, a1_ref, a2_ref, a3_ref)):
        part = jnp.dot(a_ref[...], w_ref[i * MIX_GROUP:(i + 1) * MIX_GROUP, :], preferred_element_type=_f32)
        y = part if y is None else y + part
    xn = x_ref[...] + g_ref[0] * y
    xo_ref[...] = xn
    ho_ref[...] = _rms_mod(xn, nw_ref[...], sc_ref[0], sh_ref[0]).astype(ho_ref.dtype)


def _out_proj(mixed, w, x, g, nw, sc, sh, layer=None, tm=512):
    per_b = SEQ // tm
    bspec = pl.BlockSpec((1, 1, D_MODEL), lambda i: (i // per_b, 0, 0))
    aspec = pl.BlockSpec((tm, MIX_GROUP), lambda i: (i, 0))
    return pl.pallas_call(
        _out_proj_kernel,
        out_shape=(jax.ShapeDtypeStruct((TOKENS, D_MODEL), _f32),
                   jax.ShapeDtypeStruct((TOKENS, D_MODEL), _bf16)),
        grid=(TOKENS // tm,),
        in_specs=[aspec, aspec, aspec, aspec,
                  _wspec(layer, (MIX_OUT, D_MODEL), lambda i: (0, 0)),
                  pl.BlockSpec((tm, D_MODEL), lambda i: (i, 0)),
                  bspec,
                  pl.BlockSpec((1, D_MODEL), lambda i: (0, 0)),
                  bspec, bspec],
        out_specs=(pl.BlockSpec((tm, D_MODEL), lambda i: (i, 0)),
                   pl.BlockSpec((tm, D_MODEL), lambda i: (i, 0))),
        compiler_params=pltpu.CompilerParams(dimension_semantics=("parallel",),
                                             vmem_limit_bytes=VMEM_LIMIT),
        name="out_proj",
    )(*mixed, w, x, g, nw, sc, sh)


def _mlp_kernel(h_ref, w1_ref, w2_ref, x_ref, g_ref, nw_ref, sc_ref, sh_ref, o_ref, hn_ref, acc_ref):
    j = pl.program_id(1)

    @pl.when(j == 0)
    def _():
        acc_ref[...] = jnp.zeros_like(acc_ref)

    u = jnp.dot(h_ref[...], w1_ref[...], preferred_element_type=_f32)
    u = jnp.square(jnp.maximum(u, 0.0)).astype(_bf16)
    acc_ref[...] += jnp.dot(u, w2_ref[...], preferred_element_type=_f32)

    @pl.when(j == pl.num_programs(1) - 1)
    def _():
        xn = x_ref[...] + g_ref[0] * acc_ref[...]
        o_ref[...] = xn
        hn_ref[...] = _rms_mod(xn, nw_ref[...], sc_ref[0], sh_ref[0]).astype(hn_ref.dtype)


def _mlp(h, w1, w2, x, g, nw, sc, sh, next_dtype, layer=None, tm=512, th=1024):
    per_b = SEQ // tm
    bspec = pl.BlockSpec((1, 1, D_MODEL), lambda i, j: (i // per_b, 0, 0))
    xspec = pl.BlockSpec((tm, D_MODEL), lambda i, j: (i, 0))
    return pl.pallas_call(
        _mlp_kernel,
        out_shape=(jax.ShapeDtypeStruct((TOKENS, D_MODEL), _f32),
                   jax.ShapeDtypeStruct((TOKENS, D_MODEL), next_dtype)),
        grid=(TOKENS // tm, MLP_HIDDEN // th),
        in_specs=[xspec,
                  _wspec(layer, (D_MODEL, th), lambda i, j: (0, j)),
                  _wspec(layer, (th, D_MODEL), lambda i, j: (j, 0)),
                  xspec, bspec,
                  pl.BlockSpec((1, D_MODEL), lambda i, j: (0, 0)),
                  bspec, bspec],
        out_specs=(xspec, xspec),
        scratch_shapes=[pltpu.VMEM((tm, D_MODEL), _f32)],
        compiler_params=pltpu.CompilerParams(dimension_semantics=("parallel", "arbitrary"),
                                             vmem_limit_bytes=VMEM_LIMIT),
        name="mlp",
    )(h, w1, w2, x, g, nw, sc, sh)


NSA_R = NSA_HEADS // NSA_KV_GROUPS
NSA_CMP_ROWS = SEQ // CMP_STRIDE
NSA_NC = NSA_CMP_ROWS - CMP_BLOCK // CMP_STRIDE + 1
NSA_NSB = SEQ // SLC_BLOCK
NSA_NQB = SEQ // Q_BLOCK
NSA_SEQS_PER_STEP = 4
assert BATCH % NSA_SEQS_PER_STEP == 0
assert NSA_CMP_ROWS == LANE and Q_BLOCK == LANE and LANE % NSA_NSB == 0 and SLC_TOP_N <= NSA_NSB
assert NSA_KV_GROUPS * NSA_HEAD_DIM == LANE and CMP_BLOCK == 2 * CMP_STRIDE


def _bucket_value(tab_ref, h, rel):
    exact = REL_BUCKETS // 2
    n = jnp.maximum(rel, 0)
    large = exact + (jnp.log(jnp.maximum(n, 1).astype(_f32) / exact)
                     / math.log(REL_MAX_DIST / exact) * (REL_BUCKETS - exact)).astype(jnp.int32)
    bucket = jnp.where(n < exact, n, jnp.minimum(large, REL_BUCKETS - 1))
    val = jnp.full(rel.shape, tab_ref[0, h], _f32)
    for b in range(1, REL_BUCKETS):
        val = jnp.where(bucket == b, tab_ref[b, h], val)
    return val


def _nsa_bias_kernel(tab_ref, tb_ref, cb_ref):
    h = pl.program_id(0)
    kl = lax.broadcasted_iota(jnp.int32, (Q_BLOCK, LANE), 0)
    ql = lax.broadcasted_iota(jnp.int32, (Q_BLOCK, LANE), 1)
    for d in range(3):
        tb_ref[d, 0] = _bucket_value(tab_ref, h, ql - kl + d * Q_BLOCK)
    cmp_end = kl * CMP_STRIDE + (CMP_BLOCK - 1)
    for qb in range(NSA_NQB):
        cb_ref[qb, 0] = _bucket_value(tab_ref, h, qb * Q_BLOCK + ql - cmp_end)


def _nsa_bias_tiles(rel_bias):
    assert 2 * Q_BLOCK >= REL_MAX_DIST
    return pl.pallas_call(
        _nsa_bias_kernel,
        out_shape=(jax.ShapeDtypeStruct((3, NSA_HEADS, Q_BLOCK, LANE), _f32),
                   jax.ShapeDtypeStruct((NSA_NQB, NSA_HEADS, Q_BLOCK, LANE), _f32)),
        grid=(NSA_HEADS,),
        in_specs=[pl.BlockSpec(memory_space=pltpu.SMEM)],
        out_specs=(pl.BlockSpec((3, 1, Q_BLOCK, LANE), lambda h: (0, h, 0, 0)),
                   pl.BlockSpec((NSA_NQB, 1, Q_BLOCK, LANE), lambda h: (0, h, 0, 0))),
        compiler_params=pltpu.CompilerParams(dimension_semantics=("parallel",),
                                             vmem_limit_bytes=VMEM_LIMIT),
        name="nsa_bias_tiles",
    )(rel_bias)


def _nsa_cmp_kernel(tk_ref, tv_ref, pos_ref, w1_ref, w2_ref, kc_ref, vc_ref):
    rows = lax.broadcasted_iota(jnp.int32, (NSA_CMP_ROWS, LANE), 0)
    for idx, (t_ref, o_ref) in enumerate(((tk_ref, kc_ref), (tv_ref, vc_ref))):
        u = v = None
        for i in range(CMP_STRIDE):
            t_i = t_ref[pl.ds(i, NSA_CMP_ROWS, stride=CMP_STRIDE), :]
            ui = jnp.dot((t_i + pos_ref[idx, 0, i]).astype(_bf16), w1_ref[idx, 0, i], preferred_element_type=_f32)
            vi = jnp.dot((t_i + pos_ref[idx, 1, i]).astype(_bf16), w1_ref[idx, 1, i], preferred_element_type=_f32)
            u = ui if u is None else u + ui
            v = vi if v is None else v + vi
        pre = u + pltpu.roll(v, NSA_CMP_ROWS - 1, 0)
        act = pre * jax.nn.sigmoid(pre)
        out = jnp.dot(act.astype(_bf16), w2_ref[idx], preferred_element_type=_f32)
        o_ref[0] = jnp.where(rows < NSA_NC, out, 0.0).astype(_bf16)


def _nsa_compress(proj_b, pos_x, w1_x, w2_x):
    B = proj_b.shape[0] // SEQ
    ospec = pl.BlockSpec((1, NSA_CMP_ROWS, LANE), lambda b: (b, 0, 0))
    return pl.pallas_call(
        _nsa_cmp_kernel,
        out_shape=(jax.ShapeDtypeStruct((B, NSA_CMP_ROWS, LANE), _bf16),) * 2,
        grid=(B,),
        in_specs=[pl.BlockSpec((SEQ, LANE), lambda b: (b, PB['nkc'] // LANE)),
                  pl.BlockSpec((SEQ, LANE), lambda b: (b, PB['nvc'] // LANE)),
                  pl.BlockSpec(pos_x.shape, lambda b: (0,) * pos_x.ndim),
                  pl.BlockSpec(w1_x.shape, lambda b: (0,) * w1_x.ndim),
                  pl.BlockSpec((2, LANE, LANE), lambda b: (0, 0, 0))],
        out_specs=(ospec, ospec),
        compiler_params=pltpu.CompilerParams(dimension_semantics=("parallel",),
                                             vmem_limit_bytes=VMEM_LIMIT),
        name="nsa_compress",
    )(proj_b, proj_b, pos_x, w1_x, w2_x)


def _dot_nt(a, b):
    return lax.dot_general(a, b, (((1,), (1,)), ((), ())), preferred_element_type=_f32)


def _nsa_kernel(q_ref, ks_ref, vs_ref, kw_ref, vw_ref, kc_ref, vc_ref, gate_ref, tb_ref, cb_ref, ov_ref,
                o_ref, m_ref, l_ref, acc_ref, sel_ref):
    R, QB, hd = NSA_R, Q_BLOCK, NSA_HEAD_DIM
    qb = pl.program_id(1)
    kl = lax.broadcasted_iota(jnp.int32, (QB, LANE), 0)
    t_q = qb * QB + lax.broadcasted_iota(jnp.int32, (QB, LANE), 1)
    NSQ = q_ref.shape[0]
    G = NSA_KV_GROUPS
    KB = 2 * QB

    def split_r(x):
        return [x[:, r * LANE:(r + 1) * LANE] for r in range(R)]

    gates_t = [jax.nn.sigmoid(gate_ref[sq]).T for sq in range(NSQ)]
    qps = [[jnp.concatenate([q_ref[sq, :, (R * g + r) * LANE:(R * g + r + 1) * LANE] for r in range(R)], axis=0)
            for g in range(G)] for sq in range(NSQ)]

    krow = lax.broadcasted_iota(jnp.int32, (KB, LANE), 0)
    t_q2 = qb * QB + lax.broadcasted_iota(jnp.int32, (KB, LANE), 1)
    SEL, WIN = 0, 1
    chain = lambda sq, br, g: (sq * 2 + br) * G + g

    def flash_step(work):
        items, scores = [], []
        for p, br in work:
            off = pl.multiple_of(p * KB, KB)
            rel = t_q2 - (p * KB + krow)
            tidx = [jnp.clip(qb - (KB // QB) * p - i, 0, 2) for i in range(KB // QB)]
            for sq in range(NSQ):
                k_blk = (ks_ref, kw_ref)[br][sq, pl.ds(off, KB), :]
                for g in range(G):
                    items.append((p, br, sq, g, off, rel, tidx))
                    scores.append(split_r(_dot_nt(k_blk, qps[sq][g])))
        probs, alphas = [], []
        for (p, br, sq, g, off, rel, tidx), s_t in zip(items, scores):
            ch = chain(sq, br, g)
            if br == SEL:
                blocks = [jnp.broadcast_to(sel_ref[sq * G + g, pl.ds((KB // SLC_BLOCK) * p + i, 1), :],
                                           (SLC_BLOCK, LANE)) for i in range(KB // SLC_BLOCK)]
                mask = (rel >= 0) & (jnp.concatenate(blocks, axis=0) > 0.5)
            else:
                mask = (rel >= 0) & (rel < WINDOW)
            s_t = jnp.concatenate(
                [jnp.where(mask, s_t[r] + jnp.concatenate([tb_ref[ti, R * g + r] for ti in tidx], axis=0), NEG_INF)
                 for r in range(R)], axis=1)
            m_old = m_ref[ch]
            m_new = jnp.maximum(m_old, jnp.max(s_t, axis=0, keepdims=True))
            e = jnp.exp(s_t - m_new)
            alpha = jnp.exp(m_old - m_new)
            l_ref[ch] = alpha * l_ref[ch] + jnp.sum(e, axis=0, keepdims=True)
            m_ref[ch] = m_new
            probs.append(e.astype(_bf16))
            alphas.append(alpha)
        for (p, br, sq, g, off, rel, tidx), e, alpha in zip(items, probs, alphas):
            ch = chain(sq, br, g)
            pv = _dot_tn((vs_ref, vw_ref)[br][sq, pl.ds(off, KB), :], e)
            acc_ref[ch] = alpha * acc_ref[ch] + pv[g * hd:(g + 1) * hd]

    o_cmp = []
    mask_c = (t_q - (kl * CMP_STRIDE + CMP_BLOCK - 1) >= 0) & (kl < NSA_NC)
    for sq, g in [(sq, g) for sq in range(NSQ) for g in range(G)]:
        qp = qps[sq][g]
        s_t = split_r(_dot_nt(kc_ref[sq], qp))
        s_t = jnp.concatenate([jnp.where(mask_c, s_t[r] + cb_ref[0, R * g + r], NEG_INF) for r in range(R)], axis=1)
        e = split_r(jnp.exp(s_t - jnp.max(s_t, axis=0, keepdims=True)))
        e = jnp.concatenate([jnp.where(mask_c, e[r], 0.0) for r in range(R)], axis=1)
        den = jnp.sum(e, axis=0, keepdims=True)
        p = (e / jnp.where(den > 0.0, den, 1.0)).astype(_bf16)
        o_cmp.append(_dot_tn(vc_ref[sq], p)[g * hd:(g + 1) * hd])

        imp = sum(split_r(jnp.dot(ov_ref[...], p, preferred_element_type=_f32)))
        j = lax.broadcasted_iota(jnp.int32, (NSA_NSB, LANE), 0)
        cur = (qb * QB + lax.broadcasted_iota(jnp.int32, (NSA_NSB, LANE), 1)) // SLC_BLOCK
        imp = jnp.where((j == 0) | (j == cur) | (j == cur - 1), FORCE_SCORE, imp)
        imp = jnp.where(j <= cur, imp, NEG_INF)
        cnt = jnp.zeros((NSA_NSB, LANE), _f32)
        for jo in range(NSA_NSB):
            other = imp[jo:jo + 1, :]
            beats = (other > imp) | ((other == imp) & (j > jo))
            cnt = cnt + jnp.where(beats, 1.0, 0.0)
        sel_ref[sq * G + g] = jnp.where(cnt < SLC_TOP_N, 1.0, 0.0)

    p_hi = qb // (KB // QB) + 1
    n_win = p_hi - jnp.maximum(qb - WINDOW // QB, 0) // (KB // QB)
    m_ref[...] = jnp.full(m_ref.shape, NEG_INF, _f32)
    l_ref[...] = jnp.zeros(l_ref.shape, _f32)
    acc_ref[...] = jnp.zeros(acc_ref.shape, _f32)

    def both(i, carry):
        p = p_hi - 1 - i
        flash_step([(p, SEL), (p, WIN)])
        return carry

    def selected_pair(i, carry):
        p = p_hi - 1 - n_win - 2 * i
        flash_step([(p, SEL), (p - 1, SEL)])
        return carry

    def selected_one(i, carry):
        flash_step([(0, SEL)])
        return carry

    n_sel = p_hi - n_win
    lax.fori_loop(0, n_win, both, 0)
    lax.fori_loop(0, n_sel // 2, selected_pair, 0)
    lax.fori_loop(0, n_sel % 2, selected_one, 0)
    for sq in range(NSQ):
        heads = [None] * NSA_HEADS
        for g in range(G):
            cs, cw = chain(sq, SEL, g), chain(sq, WIN, g)
            o_c, o_s, o_w = split_r(o_cmp[sq * G + g]), split_r(acc_ref[cs] / l_ref[cs]), split_r(acc_ref[cw] / l_ref[cw])
            for r in range(R):
                h = R * g + r
                gt = gates_t[sq]
                heads[h] = (gt[3 * h:3 * h + 1] * o_c[r] + gt[3 * h + 1:3 * h + 2] * o_s[r]
                            + gt[3 * h + 2:3 * h + 3] * o_w[r])
        for pk in range(NSA_HEADS * hd // LANE):
            per = LANE // hd
            pair = jnp.concatenate(heads[per * pk:per * (pk + 1)], axis=0)
            o_ref[sq, :, pk * LANE:(pk + 1) * LANE] = pair.T.astype(o_ref.dtype)


def _nsa_attention(proj_a, proj_b, kc, vc, tb, cb, overlap_t):
    B = kc.shape[0]
    nsq = NSA_SEQS_PER_STEP
    nqb = NSA_NQB
    qcols = NSA_HEADS * LANE
    out_cols = NSA_HEADS * NSA_HEAD_DIM
    pa = proj_a.reshape(B, SEQ, proj_a.shape[1])
    pb = proj_b.reshape(B, SEQ, proj_b.shape[1])
    kv = lambda blk: pl.BlockSpec((nsq, SEQ, LANE), lambda b, q: (b, 0, qcols // LANE + blk))
    cspec = pl.BlockSpec((nsq, NSA_CMP_ROWS, LANE), lambda b, q: (b, 0, 0))
    nchain = nsq * 2 * NSA_KV_GROUPS
    out = pl.pallas_call(
        _nsa_kernel,
        out_shape=jax.ShapeDtypeStruct((B, SEQ, out_cols), _bf16),
        grid=(B // nsq, nqb),
        in_specs=[pl.BlockSpec((nsq, Q_BLOCK, qcols), lambda b, q: (b, q, 0)),
                  kv(0), kv(1), kv(2), kv(3), cspec, cspec,
                  pl.BlockSpec((nsq, Q_BLOCK, LANE), lambda b, q: (b, q, PB['misc'] // LANE)),
                  pl.BlockSpec((3, NSA_HEADS, Q_BLOCK, LANE), lambda b, q: (0, 0, 0, 0)),
                  pl.BlockSpec((1, NSA_HEADS, Q_BLOCK, LANE), lambda b, q: (q, 0, 0, 0)),
                  pl.BlockSpec((NSA_NSB, LANE), lambda b, q: (0, 0))],
        out_specs=pl.BlockSpec((nsq, Q_BLOCK, out_cols), lambda b, q: (b, q, 0)),
        scratch_shapes=[pltpu.VMEM((nchain, 1, NSA_R * Q_BLOCK), _f32),
                        pltpu.VMEM((nchain, 1, NSA_R * Q_BLOCK), _f32),
                        pltpu.VMEM((nchain, NSA_HEAD_DIM, NSA_R * Q_BLOCK), _f32),
                        pltpu.VMEM((nsq * NSA_KV_GROUPS, NSA_NSB, Q_BLOCK), _f32)],
        compiler_params=pltpu.CompilerParams(dimension_semantics=("parallel", "arbitrary"),
                                             vmem_limit_bytes=VMEM_LIMIT),
        name="nsa_attention",
    )(pa, pa, pa, pa, pa, kc, vc, pb, tb, cb, overlap_t)
    return out.reshape(B * SEQ, out_cols)


def _nsa_overlap():
    n = jnp.arange(NSA_CMP_ROWS)[None, :]
    jj = jnp.arange(NSA_NSB)[:, None]
    return ((n * CMP_STRIDE <= jj * SLC_BLOCK + SLC_BLOCK - 1)
            & (n * CMP_STRIDE + CMP_BLOCK - 1 >= jj * SLC_BLOCK) & (n < NSA_NC)).astype(_bf16)


def _nsa_weight_prep(w_in, cmp_pos, cmp_w1, cmp_w2):
    G, hd = NSA_KV_GROUPS, NSA_HEAD_DIM
    eye = jnp.eye(G, dtype=_f32)
    nq = NSA_HEADS * hd
    w_in = w_in.astype(_bf16)
    zeros = jnp.zeros((DEPTH, D_MODEL, hd), _bf16)
    q_cols = []
    for h in range(NSA_HEADS):
        wq_h = w_in[:, :, h * hd:(h + 1) * hd] * (hd ** -0.5)
        q_cols += [wq_h, zeros] if h // NSA_R == 0 else [zeros, wq_h]
    slc_win = w_in[:, :, nq + 2 * NSA_KV:nq + 6 * NSA_KV]
    w_a = jnp.concatenate(q_cols + [slc_win], axis=-1)
    w_b = jnp.concatenate([w_in[:, :, _IN_OFF[n]:_IN_OFF[n] + w] for n, w in _PB_SEGMENTS]
                          + [jnp.zeros((DEPTH, D_MODEL, PROJ_B_COLS - PROJ_B_USED), _bf16)], axis=-1)
    half = CMP_BLOCK // 2
    w1 = cmp_w1.reshape(DEPTH, 2, CMP_BLOCK, hd, hd)
    w1_x = jnp.einsum('lxide,gh->lxigdhe', w1, eye).reshape(DEPTH, 2, 2, half, G * hd, G * hd).astype(_bf16)
    w2_x = jnp.einsum('lxde,gh->lxgdhe', cmp_w2, eye).reshape(DEPTH, 2, G * hd, G * hd).astype(_bf16)
    pos_x = jnp.broadcast_to(cmp_pos[:, :, :, None, :], (DEPTH, 2, CMP_BLOCK, G, hd))
    pos_x = pos_x.reshape(DEPTH, 2, 2, half, 1, G * hd)
    return w_a, w_b, w1_x, w2_x, pos_x


REC_BLOCK = 2 * CHUNK
REC_SEQS_PER_STEP = {'ssd': 2, 'gdn': 4, 'gla': 4}
assert REC_BLOCK == LANE and SEQ % REC_BLOCK == 0 and all(BATCH % n == 0 for n in REC_SEQS_PER_STEP.values())


def _split3(x):
    hi = x.astype(_bf16)
    r = x - hi.astype(_f32)
    mid = r.astype(_bf16)
    lo = (r - mid.astype(_f32)).astype(_bf16)
    return hi, mid, lo


def _dot_exact_rhs(m, x):
    return sum(jnp.dot(m, p, preferred_element_type=_f32) for p in _split3(x))


def _dot_exact_lhs(x, m):
    return sum(jnp.dot(p, m, preferred_element_type=_f32) for p in _split3(x))


def _dot_tn(a, b):
    return lax.dot_general(a, b, (((0,), (0,)), ((), ())), preferred_element_type=_f32)


def _mm3(x, y):
    xh = x.astype(_bf16)
    xl = (x - xh.astype(_f32)).astype(_bf16)
    yh = y.astype(_bf16)
    yl = (y - yh.astype(_f32)).astype(_bf16)
    return (jnp.dot(xh, yh, preferred_element_type=_f32) + jnp.dot(xh, yl, preferred_element_type=_f32)
            + jnp.dot(xl, yh, preferred_element_type=_f32))


def _chunk_masks():
    ri = lax.broadcasted_iota(jnp.int32, (REC_BLOCK, REC_BLOCK), 0)
    ci = lax.broadcasted_iota(jnp.int32, (REC_BLOCK, REC_BLOCK), 1)
    same = (ri // CHUNK) == (ci // CHUNK)
    return ri, ci, (ci <= ri) & same, (ci < ri) & same, (ri <= ci) & same


def _as_mxu(mask):
    return jnp.where(mask, 1.0, 0.0).astype(_bf16)


def _softplus(x):
    return jnp.maximum(x, 0.0) + jnp.log1p(jnp.exp(-jnp.abs(x)))


def _silu(x):
    return x * jax.nn.sigmoid(x)


def _chunk_last(x):
    ri = lax.broadcasted_iota(jnp.int32, x.shape, 0)
    return jnp.where(ri < CHUNK, x[CHUNK - 1:CHUNK], x[2 * CHUNK - 1:2 * CHUNK])


def _expand_heads(x, lane0, nheads, width):
    per = LANE // width
    lane = lax.broadcasted_iota(jnp.int32, (x.shape[0], LANE), 1)
    pieces = []
    for p0 in range(lane0, lane0 + nheads, per):
        piece = jnp.broadcast_to(x[:, p0:p0 + 1], (x.shape[0], LANE))
        for k in range(1, per):
            piece = jnp.where(lane < k * width, piece, jnp.broadcast_to(x[:, p0 + k:p0 + k + 1], (x.shape[0], LANE)))
        pieces.append(piece)
    return jnp.concatenate(pieces, axis=1) if len(pieces) > 1 else pieces[0]


def _conv_silu(x, prev, w_ref, c0, bias=None):
    n, C = x.shape
    ntap = w_ref.shape[0]
    rows = lax.broadcasted_iota(jnp.int32, (n, C), 0)
    acc = x * w_ref[ntap - 1:ntap, c0:c0 + C]
    for s in range(1, ntap):
        xs = jnp.where(rows < s, pltpu.roll(prev, s, 0), pltpu.roll(x, s, 0))
        acc = acc + xs * w_ref[ntap - 1 - s:ntap - s, c0:c0 + C]
    if bias is not None:
        acc = acc + bias
    return _silu(acc)


def _rms(x, w):
    return x * lax.rsqrt(jnp.mean(x * x, axis=-1, keepdims=True) + EPS) * w


def _ssd_kernel(z_ref, xbc_ref, dt_ref, cw_ref, cb_ref, dtb_ref, alog_ref, dskip_ref, nw_ref, o_ref,
                prev_ref, state_ref):
    @pl.when(pl.program_id(1) == 0)
    def _():
        prev_ref[...] = jnp.zeros(prev_ref.shape, _f32)
        state_ref[...] = jnp.zeros(state_ref.shape, _f32)

    for ns in range(z_ref.shape[0]):
        _ssd_block(z_ref.at[ns], xbc_ref.at[ns], dt_ref.at[ns], cw_ref, cb_ref, dtb_ref, alog_ref, dskip_ref, nw_ref,
                   o_ref.at[ns], prev_ref.at[ns], state_ref.at[ns])


def _ssd_block(z_ref, xbc_ref, dt_ref, cw_ref, cb_ref, dtb_ref, alog_ref, dskip_ref, nw_ref, o_ref,
               prev_ref, state_ref):
    G, R, P, N = SSD_GROUPS, SSD_HEADS // SSD_GROUPS, SSD_HEAD_DIM, SSD_STATE
    x_in = xbc_ref[...]
    xc = _conv_silu(x_in, prev_ref[...], cw_ref, 0, cb_ref[...])
    prev_ref[...] = x_in
    _, _, tril, _, triu = _chunk_masks()
    lane = lax.broadcasted_iota(jnp.int32, (REC_BLOCK, LANE), 1)
    L0 = PBL['sdt']
    dt = _softplus(dt_ref[...] + dtb_ref[...])
    da = dt * (-jnp.exp(alog_ref[...]))
    a_cum = _dot_exact_rhs(_as_mxu(tril), da)
    a_cum_t = _dot_exact_lhs(da.T, _as_mxu(triu))
    a_last = _chunk_last(a_cum)
    xs = xc[:, :SSD_INNER]
    xdt = xs * _expand_heads(dt, L0, SSD_HEADS, P)
    xdtd = (xdt * _expand_heads(jnp.exp(a_last - a_cum), L0, SSD_HEADS, P)).astype(_bf16)
    xdt_b = xdt.astype(_bf16)
    ea = _expand_heads(jnp.exp(a_cum), L0, SSD_HEADS, P)
    y_groups = []
    for g in range(G):
        bm = xc[:, SSD_INNER + g * N:SSD_INNER + (g + 1) * N].astype(_bf16)
        cm = xc[:, SSD_INNER + (G + g) * N:SSD_INNER + (G + g + 1) * N].astype(_bf16)
        cbm = _dot_nt(cm, bm)
        intra = []
        for pr in range(R // 2):
            both = []
            for k in range(2):
                h = g * R + 2 * pr + k
                seg = jnp.exp(jnp.where(tril, a_cum[:, L0 + h:L0 + h + 1] - a_cum_t[L0 + h:L0 + h + 1, :], -jnp.inf))
                both.append(jnp.dot((cbm * seg).astype(_bf16), xdt_b[:, (h - k) * P:(h - k + 2) * P],
                                    preferred_element_type=_f32))
            intra.append(jnp.where(lane < P, both[0], both[1]))
        y_intra = jnp.concatenate(intra, axis=1)
        prev_rows = []
        for c in range(REC_BLOCK // CHUNK):
            rows = slice(c * CHUNK, (c + 1) * CHUNK)
            st = state_ref[g]
            prev_rows.append(jnp.dot(cm[rows], st.astype(_bf16), preferred_element_type=_f32))
            dec = _expand_heads(jnp.exp(a_cum[(c + 1) * CHUNK - 1:(c + 1) * CHUNK]), L0, SSD_HEADS, P)
            state_ref[g] = (st * dec[:, g * R * P:(g + 1) * R * P]
                            + _dot_tn(bm[rows], xdtd[rows, g * R * P:(g + 1) * R * P]))
        y_groups.append(y_intra + jnp.concatenate(prev_rows, axis=0) * ea[:, g * R * P:(g + 1) * R * P])
    y = jnp.concatenate(y_groups, axis=1) + xs * dskip_ref[...]
    y = y * _silu(z_ref[...])
    gw = SSD_INNER // G
    for g in range(G):
        o_ref[:, g * gw:(g + 1) * gw] = _rms(y[:, g * gw:(g + 1) * gw], nw_ref[:, g * gw:(g + 1) * gw]).astype(o_ref.dtype)


def _gdn_kernel(q_ref, k_ref, v_ref, z_ref, ba_ref, cw_ref, dtb_ref, alog_ref, nw_ref, o_ref,
                pq_ref, pk_ref, pv_ref, state_ref):
    H, Dh = GDN_HEADS, GDN_HEAD_DIM

    @pl.when(pl.program_id(1) == 0)
    def _():
        for r in (pq_ref, pk_ref, pv_ref, state_ref):
            r[...] = jnp.zeros(r.shape, _f32)

    NS = q_ref.shape[0]
    ri, ci, tril, strict, triu = _chunk_masks()
    q, k, v, beta, gcum, gcum_t, glast = [], [], [], [], [], [], []
    for ns in range(NS):
        for dst, x_ref, p_ref, c0 in ((q, q_ref, pq_ref, 0), (k, k_ref, pk_ref, GDN_WIDTH), (v, v_ref, pv_ref, 2 * GDN_WIDTH)):
            x_in = x_ref[ns]
            dst.append(_conv_silu(x_in, p_ref[ns], cw_ref, c0))
            p_ref[ns] = x_in
        ba = ba_ref[ns]
        beta.append(jax.nn.sigmoid(ba))
        gl = -jnp.exp(alog_ref[...]) * _softplus(ba + dtb_ref[...])
        gcum.append(_dot_exact_rhs(_as_mxu(tril), gl))
        gcum_t.append(_dot_exact_lhs(gl.T, _as_mxu(triu)))
        glast.append(_chunk_last(gcum[ns]))
    zero_rows = jnp.zeros((CHUNK, Dh), _f32)
    eye = jnp.where(tril & jnp.logical_not(strict), 1.0, 0.0)
    hs = range(NS * H)
    seq = [i // H for i in hs]
    sls = [slice((i % H) * Dh, (i % H + 1) * Dh) for i in hs]
    qh = [q[seq[i]][:, sls[i]] for i in hs]
    kh = [k[seq[i]][:, sls[i]] for i in hs]
    qq = [qh[i] * lax.rsqrt(jnp.sum(qh[i] * qh[i], axis=-1, keepdims=True) + EPS) * (Dh ** -0.5) for i in hs]
    kk = [kh[i] * lax.rsqrt(jnp.sum(kh[i] * kh[i], axis=-1, keepdims=True) + EPS) for i in hs]
    lb = [PBL['gbeta'] + i % H for i in hs]
    lg = [PBL['gbeta'] + H + i % H for i in hs]
    bcol = [beta[seq[h]][:, lb[h]:lb[h] + 1] for h in hs]
    gcol = [gcum[seq[h]][:, lg[h]:lg[h] + 1] for h in hs]
    decay = [jnp.exp(jnp.where(tril, gcol[h] - gcum_t[seq[h]][lg[h]:lg[h] + 1, :], -jnp.inf)) for h in hs]
    kb = [kk[h] * bcol[h] for h in hs]
    s = [_dot_nt(jnp.concatenate([kb[h], qq[h]], axis=0).astype(_bf16), kk[h].astype(_bf16)) for h in hs]
    a_mat = [jnp.where(strict, s[h][:REC_BLOCK] * decay[h], 0.0) for h in hs]
    aqk = [(s[h][REC_BLOCK:] * decay[h]).astype(_bf16) for h in hs]
    SUB = 8
    same = lambda n: (ri // n) == (ci // n)
    a_sub = [jnp.where(same(SUB), a_mat[h], 0.0) for h in hs]
    tinv = [eye - a_sub[h] for h in hs]
    pw = a_sub
    for _ in range(SUB.bit_length() - 2):
        pw = [_mm3(pw[h], pw[h]) for h in hs]
        tinv = [tinv[h] + _mm3(tinv[h], pw[h]) for h in hs]
    n = SUB
    while n < CHUNK:
        enclosed = same(2 * n) & jnp.logical_not(same(n))
        tc = [_mm3(tinv[h], jnp.where(enclosed, a_mat[h], 0.0)) for h in hs]
        tinv = [tinv[h] - _mm3(tc[h], tinv[h]) for h in hs]
        n *= 2
    sol = [_mm3(tinv[h], jnp.concatenate([v[seq[h]][:, sls[h]] * bcol[h], kb[h] * jnp.exp(gcol[h])], axis=1))
           for h in hs]
    u = [sol[h][:, :Dh] for h in hs]
    w = [sol[h][:, Dh:].astype(_bf16) for h in hs]
    q_dec = [(qq[h] * jnp.exp(gcol[h])).astype(_bf16) for h in hs]
    k_end = [(kk[h] * jnp.exp(glast[seq[h]][:, lg[h]:lg[h] + 1] - gcol[h])).astype(_bf16) for h in hs]
    o_rows = [[] for _ in hs]
    for c in range(REC_BLOCK // CHUNK):
        rows = slice(c * CHUNK, (c + 1) * CHUNK)
        st = [state_ref[h] for h in hs]
        st_b = [st[h].astype(_bf16) for h in hs]
        v_new = [u[h][rows] - jnp.dot(w[h][rows], st_b[h], preferred_element_type=_f32) for h in hs]
        v_full = [jnp.concatenate([v_new[h], zero_rows] if c == 0 else [zero_rows, v_new[h]], axis=0).astype(_bf16)
                  for h in hs]
        for h in hs:
            o_rows[h].append(jnp.dot(q_dec[h][rows], st_b[h], preferred_element_type=_f32)
                             + jnp.dot(aqk[h][rows], v_full[h], preferred_element_type=_f32))
            d_last = jnp.exp(gcum[seq[h]][(c + 1) * CHUNK - 1:(c + 1) * CHUNK, lg[h]:lg[h] + 1])
            state_ref[h] = st[h] * d_last + _dot_tn(k_end[h][rows], v_new[h].astype(_bf16))
    for h in hs:
        o = _rms(jnp.concatenate(o_rows[h], axis=0), nw_ref[...]) * _silu(z_ref[seq[h], :, sls[h]])
        o_ref[seq[h], :, sls[h]] = o.astype(o_ref.dtype)


def _gla_kernel(q_ref, k_ref, v_ref, go_ref, lr_ref, w2_ref, gb_ref, nw_ref, o_ref, state_ref):
    @pl.when(pl.program_id(1) == 0)
    def _():
        state_ref[...] = jnp.zeros(state_ref.shape, _f32)

    for ns in range(q_ref.shape[0]):
        _gla_block(q_ref.at[ns], k_ref.at[ns], v_ref.at[ns], go_ref.at[ns], lr_ref.at[ns], w2_ref, gb_ref, nw_ref,
                   o_ref.at[ns], state_ref.at[ns])


def _gla_block(q_ref, k_ref, v_ref, go_ref, lr_ref, w2_ref, gb_ref, nw_ref, o_ref, state_ref):
    H, Dk, Dv = GLA_HEADS, GLA_DK, GLA_DV
    _, _, tril, _, _ = _chunk_masks()
    lane = lax.broadcasted_iota(jnp.int32, (REC_BLOCK, LANE), 1)
    pre = jnp.dot(lr_ref[...].astype(_bf16), w2_ref[...], preferred_element_type=_f32) + gb_ref[...]
    gk = (jnp.minimum(pre, 0.0) - jnp.log1p(jnp.exp(-jnp.abs(pre)))) / GLA_GATE_NORM
    bcum = _dot_exact_rhs(_as_mxu(tril), gk)
    blast = _chunk_last(bcum)
    q_dec = q_ref[...] * (Dk ** -0.5) * jnp.exp(bcum)
    k_inv = (k_ref[...] * jnp.exp(-bcum)).astype(_bf16)
    k_end = (k_ref[...] * jnp.exp(blast - bcum)).astype(_bf16)
    per = LANE // Dk
    nchunk = REC_BLOCK // CHUNK
    chunk_rows = [slice(c * CHUNK, (c + 1) * CHUNK) for c in range(nchunk)]
    psl = [slice(pr * LANE, (pr + 1) * LANE) for pr in range(H // per)]
    qm = [jnp.where(lane // Dk == h % per, q_dec[:, psl[h // per]], 0.0).astype(_bf16) for h in range(H)]
    vh = [v_ref[:, h * Dv:(h + 1) * Dv].astype(_bf16) for h in range(H)]
    attn = [jnp.where(tril, _dot_nt(qm[h], k_inv[:, psl[h // per]]), 0.0).astype(_bf16) for h in range(H)]
    o_intra = [jnp.dot(attn[h], vh[h], preferred_element_type=_f32) for h in range(H)]
    local = []
    for rows in chunk_rows:
        per_pair = []
        for pr in range(H // per):
            loc = None
            for k in range(per):
                lk = _dot_tn(vh[pr * per + k][rows], k_end[rows, psl[pr]])
                loc = lk if loc is None else jnp.where(lane < k * Dk, loc, lk)
            per_pair.append(loc)
        local.append(per_pair)
    o_prev = [[] for _ in range(H)]
    for c, rows in enumerate(chunk_rows):
        for pr in range(H // per):
            st = state_ref[pr]
            st_b = st.astype(_bf16)
            for k in range(per):
                o_prev[pr * per + k].append(_dot_nt(qm[pr * per + k][rows], st_b))
            state_ref[pr] = st * jnp.exp(bcum[(c + 1) * CHUNK - 1:(c + 1) * CHUNK, psl[pr]]) + local[c][pr]
    for h in range(H):
        o = o_intra[h] + jnp.concatenate(o_prev[h], axis=0)
        o = _rms(o, nw_ref[...]) * _silu(go_ref[:, h * Dv:(h + 1) * Dv])
        o_ref[:, h * Dv:(h + 1) * Dv] = o.astype(o_ref.dtype)


def _rec_call(body, proj_b, col_blocks, params, out_cols, scratch, name, nseq=None):
    B = proj_b.shape[0] // SEQ
    nblk = SEQ // REC_BLOCK
    if nseq is None:
        src, grid = proj_b, (B, nblk)
        in_specs = [pl.BlockSpec((REC_BLOCK, w), (lambda b, t, c=c0 // w: (b * nblk + t, c))) for c0, w in col_blocks]
        out_shape = jax.ShapeDtypeStruct((B * SEQ, out_cols), _bf16)
        out_spec = pl.BlockSpec((REC_BLOCK, out_cols), lambda b, t: (b * nblk + t, 0))
    else:
        src, grid = proj_b.reshape(B, SEQ, proj_b.shape[1]), (B // nseq, nblk)
        in_specs = [pl.BlockSpec((nseq, REC_BLOCK, w), (lambda b, t, c=c0 // w: (b, t, c))) for c0, w in col_blocks]
        out_shape = jax.ShapeDtypeStruct((B, SEQ, out_cols), _bf16)
        out_spec = pl.BlockSpec((nseq, REC_BLOCK, out_cols), lambda b, t: (b, t, 0))
    for p in params:
        in_specs.append(pl.BlockSpec(p.shape, lambda b, t, nd=p.ndim: (0,) * nd))
    out = pl.pallas_call(
        body,
        out_shape=out_shape,
        grid=grid,
        in_specs=in_specs,
        out_specs=out_spec,
        scratch_shapes=scratch,
        compiler_params=pltpu.CompilerParams(dimension_semantics=("parallel", "arbitrary"),
                                             vmem_limit_bytes=VMEM_LIMIT),
        name=name,
    )(*([src] * len(col_blocks)), *params)
    return out.reshape(B * SEQ, out_cols)


def _lane_pad(v, lane0=0):
    return jnp.pad(v.astype(_f32), (lane0, LANE - lane0 - v.shape[0]))[None]


def _ssd_call(proj_b, conv_w, conv_b, dt_bias, a_log, d_skip, norm_w):
    cols = [(PB['sz'], SSD_INNER), (PB['sxbc'], SSD_XBC), (PB['misc'], LANE)]
    params = [conv_w, conv_b[None], _lane_pad(dt_bias, PBL['sdt']), _lane_pad(a_log, PBL['sdt']),
              jnp.repeat(d_skip, SSD_HEAD_DIM)[None], norm_w[None]]
    ns = REC_SEQS_PER_STEP['ssd']
    scratch = [pltpu.VMEM((ns, REC_BLOCK, SSD_XBC), _f32),
               pltpu.VMEM((ns, SSD_GROUPS, SSD_STATE, SSD_INNER // SSD_GROUPS), _f32)]
    return _rec_call(_ssd_kernel, proj_b, cols, params, SSD_INNER, scratch, "ssd", nseq=ns)


def _gdn_call(proj_b, conv_w, dt_bias, a_log, norm_w):
    W = GDN_WIDTH
    cols = [(PB['gq'], W), (PB['gk'], W), (PB['gv'], W), (PB['gz'], W), (PB['misc'], LANE)]
    decay_lane = PBL['gbeta'] + GDN_HEADS
    params = [conv_w, _lane_pad(dt_bias, decay_lane), _lane_pad(a_log, decay_lane), norm_w[None]]
    ns = REC_SEQS_PER_STEP['gdn']
    scratch = ([pltpu.VMEM((ns, REC_BLOCK, W), _f32)] * 3
               + [pltpu.VMEM((ns * GDN_HEADS, GDN_HEAD_DIM, GDN_HEAD_DIM), _f32)])
    return _rec_call(_gdn_kernel, proj_b, cols, params, W, scratch, "gdn", nseq=ns)


def _gla_call(proj_b, gate_w2, gate_b, norm_w):
    cols = [(PB['lq'], GLA_KEY), (PB['lk'], GLA_KEY), (PB['lv'], GLA_VAL), (PB['lg'], GLA_VAL), (PB['misc'], LANE)]
    w2 = jnp.pad(gate_w2, ((PBL['llr'], LANE - PBL['llr'] - GLA_GATE_RANK), (0, 0))).astype(_bf16)
    params = [w2, gate_b[None], norm_w[None]]
    ns = REC_SEQS_PER_STEP['gla']
    scratch = [pltpu.VMEM((ns, GLA_HEADS * GLA_DK // LANE, GLA_DV, LANE), _f32)]
    return _rec_call(_gla_kernel, proj_b, cols, params, GLA_VAL, scratch, "gla", nseq=ns)


def kernel(x, c, rel_bias, norm1_w, norm2_w, ada_w, ada_b, w_in, w_out, nsa_cmp_pos, nsa_cmp_w1, nsa_cmp_w2, ssd_conv_w, ssd_conv_b, ssd_dt_bias, ssd_a_log, ssd_d, ssd_norm_w, gdn_conv_w, gdn_dt_bias, gdn_a_log, gdn_norm_w, gla_gate_w2, gla_gate_b, gla_norm_w, mlp_w1, mlp_w2, final_norm_w):
    B, S, D = x.shape
    mod = _ada_all(c, ada_w, ada_b).reshape(DEPTH, B, 6, 1, D)
    w_a, w_b, cmp_w1_x, cmp_w2_x, cmp_pos_x = _nsa_weight_prep(w_in, nsa_cmp_pos, nsa_cmp_w1, nsa_cmp_w2)
    w_out_b = w_out.astype(_bf16)
    w1_b = mlp_w1.astype(_bf16)
    w2_b = mlp_w2.astype(_bf16)
    tb, cb = _nsa_bias_tiles(rel_bias)
    overlap_t = _nsa_overlap()
    xf = x.reshape(TOKENS, D)
    no_mod = jnp.zeros((B, 1, D), _f32)
    h = _norm_mod(xf, norm1_w[0][None], mod[0, :, 1], mod[0, :, 0])
    for l in range(DEPTH):
        sh1, sc1, g1, sh2, sc2, g2 = (mod[l, :, i] for i in range(6))
        proj_a = _matmul(h, w_a, l, out_dtype=_bf16)
        proj_b = _matmul(h, w_b, l, tn=PROJ_B_TN)
        kc, vc = _nsa_compress(proj_b, cmp_pos_x[l], cmp_w1_x[l], cmp_w2_x[l])
        y_nsa = _nsa_attention(proj_a, proj_b, kc, vc, tb, cb, overlap_t)
        y_ssd = _ssd_call(proj_b, ssd_conv_w[l], ssd_conv_b[l], ssd_dt_bias[l], ssd_a_log[l], ssd_d[l], ssd_norm_w[l])
        y_gdn = _gdn_call(proj_b, gdn_conv_w[l], gdn_dt_bias[l], gdn_a_log[l], gdn_norm_w[l])
        y_gla = _gla_call(proj_b, gla_gate_w2[l], gla_gate_b[l], gla_norm_w[l])
        xf, h2 = _out_proj((y_nsa, y_ssd, y_gdn, y_gla), w_out_b, xf, g1,
                           norm2_w[l][None], sc2, sh2, layer=l)
        if l + 1 < DEPTH:
            xf, h = _mlp(h2, w1_b, w2_b, xf, g2, norm1_w[l + 1][None], mod[l + 1, :, 1], mod[l + 1, :, 0], _bf16,
                         layer=l)
        else:
            _, out = _mlp(h2, w1_b, w2_b, xf, g2, final_norm_w[None], no_mod, no_mod, _f32, layer=l)
    return out.reshape(B, S, D)
```

```python
import math
from functools import partial

import jax
import jax.numpy as jnp
from jax import lax
from jax.experimental import pallas as pl
from jax.experimental.pallas import tpu as pltpu

D_MODEL = 2048
BATCH = 16
SEQ = 2048
DEPTH = 4

MIX_GROUP = D_MODEL // 4
NSA_HEAD_DIM = 64
NSA_HEADS = MIX_GROUP // NSA_HEAD_DIM
NSA_KV_GROUPS = max(1, NSA_HEADS // 4)
NSA_KV = NSA_KV_GROUPS * NSA_HEAD_DIM
CMP_BLOCK = 32
CMP_STRIDE = 16
SLC_BLOCK = 64
SLC_TOP_N = 8
WINDOW = 512
Q_BLOCK = 128
REL_BUCKETS = 32
REL_MAX_DIST = 128
SSD_HEAD_DIM = 64
SSD_HEADS = MIX_GROUP // SSD_HEAD_DIM
SSD_INNER = SSD_HEADS * SSD_HEAD_DIM
SSD_GROUPS = 2
SSD_STATE = 128
SSD_CONV = 4
SSD_XBC = SSD_INNER + 2 * SSD_GROUPS * SSD_STATE
GDN_HEAD_DIM = 128
GDN_HEADS = MIX_GROUP // GDN_HEAD_DIM
GDN_WIDTH = GDN_HEADS * GDN_HEAD_DIM
GDN_CONV = 4
GLA_DV = 128
GLA_HEADS = MIX_GROUP // GLA_DV
GLA_DK = GLA_DV // 2
GLA_KEY = GLA_HEADS * GLA_DK
GLA_VAL = GLA_HEADS * GLA_DV
GLA_GATE_RANK = 16
GLA_GATE_NORM = 16.0
CHUNK = 64
MLP_HIDDEN = 4 * D_MODEL
EPS = 1e-6
NEG_INF = -1e30
FORCE_SCORE = 1e9
IN_SPLITS = (NSA_HEADS * NSA_HEAD_DIM, NSA_KV, NSA_KV, NSA_KV, NSA_KV, NSA_KV, NSA_KV, NSA_HEADS * 3,
             SSD_INNER, SSD_XBC, SSD_HEADS,
             GDN_WIDTH, GDN_WIDTH, GDN_WIDTH, GDN_WIDTH, GDN_HEADS, GDN_HEADS,
             GLA_KEY, GLA_KEY, GLA_VAL, GLA_VAL, GLA_GATE_RANK)
IN_COLS = sum(IN_SPLITS)
MIX_OUT = NSA_HEADS * NSA_HEAD_DIM + SSD_INNER + GDN_WIDTH + GLA_VAL

LANE = 128
VMEM_LIMIT = 56 * 1024 * 1024
TOKENS = BATCH * SEQ
_IN_NAMES = ('nq', 'nkc', 'nvc', 'nks', 'nvs', 'nkw', 'nvw', 'ngate', 'sz', 'sxbc', 'sdt',
             'gq', 'gk', 'gv', 'gz', 'gbeta', 'ga', 'lq', 'lk', 'lv', 'lg', 'llr')
_IN_W = dict(zip(_IN_NAMES, IN_SPLITS))
_IN_OFF = {n: sum(IN_SPLITS[:i]) for i, n in enumerate(_IN_NAMES)}
_PB_SEGMENTS = (('sxbc', SSD_XBC), ('gq', GDN_WIDTH), ('gk', GDN_WIDTH), ('gv', GDN_WIDTH), ('gz', GDN_WIDTH),
                ('sz', SSD_INNER), ('lv', GLA_VAL), ('lg', GLA_VAL), ('lq', GLA_KEY), ('lk', GLA_KEY),
                ('nkc', NSA_KV), ('nvc', NSA_KV),
                ('ngate', NSA_HEADS * 3), ('sdt', SSD_HEADS), ('gbeta', 2 * GDN_HEADS), ('llr', GLA_GATE_RANK))
_PB_MISC = ('ngate', 'sdt', 'gbeta', 'llr')
PB, PBL = {}, {}
PROJ_B_USED = 0
for _n, _w in _PB_SEGMENTS:
    if _n in _PB_MISC:
        PB.setdefault('misc', PROJ_B_USED - PROJ_B_USED % LANE)
        PBL[_n] = PROJ_B_USED - PB['misc']
    else:
        assert PROJ_B_USED % _w == 0 and _w % LANE == 0
        PB[_n] = PROJ_B_USED
    PROJ_B_USED += _w
assert PROJ_B_USED - PB['misc'] <= LANE and PBL['ngate'] == 0 and _IN_OFF['ga'] == _IN_OFF['gbeta'] + GDN_HEADS
PROJ_B_TN = 512
PROJ_B_COLS = -(-PROJ_B_USED // PROJ_B_TN) * PROJ_B_TN

_bf16 = jnp.bfloat16
_f32 = jnp.float32


def _ada_kernel(c_ref, w_ref, b_ref, o_ref):
    c = c_ref[...]
    c_act = c * jax.nn.sigmoid(c)
    o_ref[0] = jnp.dot(c_act, w_ref[0], preferred_element_type=_f32) + b_ref[0]


def _ada_all(c, ada_w, ada_b):
    tn = 1024
    return pl.pallas_call(
        _ada_kernel,
        out_shape=jax.ShapeDtypeStruct((DEPTH, BATCH, 6 * D_MODEL), _f32),
        grid=(DEPTH, 6 * D_MODEL // tn),
        in_specs=[pl.BlockSpec((BATCH, D_MODEL), lambda l, j: (0, 0)),
                  pl.BlockSpec((1, D_MODEL, tn), lambda l, j: (l, 0, j)),
                  pl.BlockSpec((1, 1, tn), lambda l, j: (l, 0, j))],
        out_specs=pl.BlockSpec((1, BATCH, tn), lambda l, j: (l, 0, j)),
        compiler_params=pltpu.CompilerParams(dimension_semantics=("parallel", "parallel"),
                                             vmem_limit_bytes=VMEM_LIMIT),
        name="ada_mod",
    )(c, ada_w, ada_b.reshape(DEPTH, 1, 6 * D_MODEL))


def _rms_mod(x, w, sc, sh):
    y = x * lax.rsqrt(jnp.mean(x * x, axis=-1, keepdims=True) + EPS)
    return (y * w) * (1.0 + sc) + sh


def _norm_mod_kernel(x_ref, w_ref, sc_ref, sh_ref, o_ref):
    o_ref[...] = _rms_mod(x_ref[...], w_ref[...], sc_ref[0], sh_ref[0]).astype(o_ref.dtype)


def _norm_mod(x, w, sc, sh, tm=512):
    per_b = SEQ // tm
    return pl.pallas_call(
        _norm_mod_kernel,
        out_shape=jax.ShapeDtypeStruct((TOKENS, D_MODEL), _bf16),
        grid=(TOKENS // tm,),
        in_specs=[pl.BlockSpec((tm, D_MODEL), lambda i: (i, 0)),
                  pl.BlockSpec((1, D_MODEL), lambda i: (0, 0)),
                  pl.BlockSpec((1, 1, D_MODEL), lambda i: (i // per_b, 0, 0)),
                  pl.BlockSpec((1, 1, D_MODEL), lambda i: (i // per_b, 0, 0))],
        out_specs=pl.BlockSpec((tm, D_MODEL), lambda i: (i, 0)),
        compiler_params=pltpu.CompilerParams(dimension_semantics=("parallel",),
                                             vmem_limit_bytes=VMEM_LIMIT),
        name="norm_mod",
    )(x, w, sc, sh)


def _matmul_kernel(a_ref, w_ref, o_ref):
    o_ref[...] = jnp.dot(a_ref[...], w_ref[...], preferred_element_type=_f32).astype(o_ref.dtype)


def _wspec(layer, block, index):
    if layer is None:
        return pl.BlockSpec(block, index)
    return pl.BlockSpec((None,) + block, lambda *g: (layer,) + index(*g))


def _matmul(a, w, layer=None, tm=1024, tn=512, out_dtype=_f32):
    M, K = a.shape
    N = w.shape[-1]
    return pl.pallas_call(
        _matmul_kernel,
        out_shape=jax.ShapeDtypeStruct((M, N), out_dtype),
        grid=(M // tm, N // tn),
        in_specs=[pl.BlockSpec((tm, K), lambda i, j: (i, 0)),
                  _wspec(layer, (K, tn), lambda i, j: (0, j))],
        out_specs=pl.BlockSpec((tm, tn), lambda i, j: (i, j)),
        compiler_params=pltpu.CompilerParams(dimension_semantics=("parallel", "parallel"),
                                             vmem_limit_bytes=VMEM_LIMIT),
        name="in_proj",
    )(a, w)


def _out_proj_kernel(a0_ref, a1_ref, a2_ref, a3_ref, w_ref, x_ref, g_ref, nw_ref, sc_ref, sh_ref, xo_ref, ho_ref):
    y = None
    for i, a_ref in enumerate((a0_ref, a1_ref, a2_ref, a3_ref)):
        part = jnp.dot(a_ref[...], w_ref[i * MIX_GROUP:(i + 1) * MIX_GROUP, :], preferred_element_type=_f32)
        y = part if y is None else y + part
    xn = x_ref[...] + g_ref[0] * y
    xo_ref[...] = xn
    ho_ref[...] = _rms_mod(xn, nw_ref[...], sc_ref[0], sh_ref[0]).astype(ho_ref.dtype)


def _out_proj(mixed, w, x, g, nw, sc, sh, layer=None, tm=512):
    per_b = SEQ // tm
    bspec = pl.BlockSpec((1, 1, D_MODEL), lambda i: (i // per_b, 0, 0))
    aspec = pl.BlockSpec((tm, MIX_GROUP), lambda i: (i, 0))
    return pl.pallas_call(
        _out_proj_kernel,
        out_shape=(jax.ShapeDtypeStruct((TOKENS, D_MODEL), _f32),
                   jax.ShapeDtypeStruct((TOKENS, D_MODEL), _bf16)),
        grid=(TOKENS // tm,),
        in_specs=[aspec, aspec, aspec, aspec,
                  _wspec(layer, (MIX_OUT, D_MODEL), lambda i: (0, 0)),
                  pl.BlockSpec((tm, D_MODEL), lambda i: (i, 0)),
                  bspec,
                  pl.BlockSpec((1, D_MODEL), lambda i: (0, 0)),
                  bspec, bspec],
        out_specs=(pl.BlockSpec((tm, D_MODEL), lambda i: (i, 0)),
                   pl.BlockSpec((tm, D_MODEL), lambda i: (i, 0))),
        compiler_params=pltpu.CompilerParams(dimension_semantics=("parallel",),
                                             vmem_limit_bytes=VMEM_LIMIT),
        name="out_proj",
    )(*mixed, w, x, g, nw, sc, sh)


def _mlp_kernel(h_ref, w1_ref, w2_ref, x_ref, g_ref, nw_ref, sc_ref, sh_ref, o_ref, hn_ref, acc_ref):
    j = pl.program_id(1)

    @pl.when(j == 0)
    def _():
        acc_ref[...] = jnp.zeros_like(acc_ref)

    u = jnp.dot(h_ref[...], w1_ref[...], preferred_element_type=_f32)
    u = jnp.square(jnp.maximum(u, 0.0)).astype(_bf16)
    acc_ref[...] += jnp.dot(u, w2_ref[...], preferred_element_type=_f32)

    @pl.when(j == pl.num_programs(1) - 1)
    def _():
        xn = x_ref[...] + g_ref[0] * acc_ref[...]
        o_ref[...] = xn
        hn_ref[...] = _rms_mod(xn, nw_ref[...], sc_ref[0], sh_ref[0]).astype(hn_ref.dtype)


def _mlp(h, w1, w2, x, g, nw, sc, sh, next_dtype, layer=None, tm=512, th=1024):
    per_b = SEQ // tm
    bspec = pl.BlockSpec((1, 1, D_MODEL), lambda i, j: (i // per_b, 0, 0))
    xspec = pl.BlockSpec((tm, D_MODEL), lambda i, j: (i, 0))
    return pl.pallas_call(
        _mlp_kernel,
        out_shape=(jax.ShapeDtypeStruct((TOKENS, D_MODEL), _f32),
                   jax.ShapeDtypeStruct((TOKENS, D_MODEL), next_dtype)),
        grid=(TOKENS // tm, MLP_HIDDEN // th),
        in_specs=[xspec,
                  _wspec(layer, (D_MODEL, th), lambda i, j: (0, j)),
                  _wspec(layer, (th, D_MODEL), lambda i, j: (j, 0)),
                  xspec, bspec,
                  pl.BlockSpec((1, D_MODEL), lambda i, j: (0, 0)),
                  bspec, bspec],
        out_specs=(xspec, xspec),
        scratch_shapes=[pltpu.VMEM((tm, D_MODEL), _f32)],
        compiler_params=pltpu.CompilerParams(dimension_semantics=("parallel", "arbitrary"),
                                             vmem_limit_bytes=VMEM_LIMIT),
        name="mlp",
    )(h, w1, w2, x, g, nw, sc, sh)


NSA_R = NSA_HEADS // NSA_KV_GROUPS
NSA_CMP_ROWS = SEQ // CMP_STRIDE
NSA_NC = NSA_CMP_ROWS - CMP_BLOCK // CMP_STRIDE + 1
NSA_NSB = SEQ // SLC_BLOCK
NSA_NQB = SEQ // Q_BLOCK
NSA_SEQS_PER_STEP = 4
assert BATCH % NSA_SEQS_PER_STEP == 0
assert NSA_CMP_ROWS == LANE and Q_BLOCK == LANE and LANE % NSA_NSB == 0 and SLC_TOP_N <= NSA_NSB
assert NSA_KV_GROUPS * NSA_HEAD_DIM == LANE and CMP_BLOCK == 2 * CMP_STRIDE


def _bucket_value(tab_ref, h, rel):
    exact = REL_BUCKETS // 2
    n = jnp.maximum(rel, 0)
    large = exact + (jnp.log(jnp.maximum(n, 1).astype(_f32) / exact)
                     / math.log(REL_MAX_DIST / exact) * (REL_BUCKETS - exact)).astype(jnp.int32)
    bucket = jnp.where(n < exact, n, jnp.minimum(large, REL_BUCKETS - 1))
    val = jnp.full(rel.shape, tab_ref[0, h], _f32)
    for b in range(1, REL_BUCKETS):
        val = jnp.where(bucket == b, tab_ref[b, h], val)
    return val


def _nsa_bias_kernel(tab_ref, tb_ref, cb_ref):
    h = pl.program_id(0)
    kl = lax.broadcasted_iota(jnp.int32, (Q_BLOCK, LANE), 0)
    ql = lax.broadcasted_iota(jnp.int32, (Q_BLOCK, LANE), 1)
    for d in range(3):
        tb_ref[d, 0] = _bucket_value(tab_ref, h, ql - kl + d * Q_BLOCK)
    cmp_end = kl * CMP_STRIDE + (CMP_BLOCK - 1)
    for qb in range(NSA_NQB):
        cb_ref[qb, 0] = _bucket_value(tab_ref, h, qb * Q_BLOCK + ql - cmp_end)


def _nsa_bias_tiles(rel_bias):
    assert 2 * Q_BLOCK >= REL_MAX_DIST
    return pl.pallas_call(
        _nsa_bias_kernel,
        out_shape=(jax.ShapeDtypeStruct((3, NSA_HEADS, Q_BLOCK, LANE), _f32),
                   jax.ShapeDtypeStruct((NSA_NQB, NSA_HEADS, Q_BLOCK, LANE), _f32)),
        grid=(NSA_HEADS,),
        in_specs=[pl.BlockSpec(memory_space=pltpu.SMEM)],
        out_specs=(pl.BlockSpec((3, 1, Q_BLOCK, LANE), lambda h: (0, h, 0, 0)),
                   pl.BlockSpec((NSA_NQB, 1, Q_BLOCK, LANE), lambda h: (0, h, 0, 0))),
        compiler_params=pltpu.CompilerParams(dimension_semantics=("parallel",),
                                             vmem_limit_bytes=VMEM_LIMIT),
        name="nsa_bias_tiles",
    )(rel_bias)


def _nsa_cmp_kernel(tk_ref, tv_ref, pos_ref, w1_ref, w2_ref, kc_ref, vc_ref):
    rows = lax.broadcasted_iota(jnp.int32, (NSA_CMP_ROWS, LANE), 0)
    for idx, (t_ref, o_ref) in enumerate(((tk_ref, kc_ref), (tv_ref, vc_ref))):
        u = v = None
        for i in range(CMP_STRIDE):
            t_i = t_ref[pl.ds(i, NSA_CMP_ROWS, stride=CMP_STRIDE), :]
            ui = jnp.dot((t_i + pos_ref[idx, 0, i]).astype(_bf16), w1_ref[idx, 0, i], preferred_element_type=_f32)
            vi = jnp.dot((t_i + pos_ref[idx, 1, i]).astype(_bf16), w1_ref[idx, 1, i], preferred_element_type=_f32)
            u = ui if u is None else u + ui
            v = vi if v is None else v + vi
        pre = u + pltpu.roll(v, NSA_CMP_ROWS - 1, 0)
        act = pre * jax.nn.sigmoid(pre)
        out = jnp.dot(act.astype(_bf16), w2_ref[idx], preferred_element_type=_f32)
        o_ref[0] = jnp.where(rows < NSA_NC, out, 0.0).astype(_bf16)


def _nsa_compress(proj_b, pos_x, w1_x, w2_x):
    B = proj_b.shape[0] // SEQ
    ospec = pl.BlockSpec((1, NSA_CMP_ROWS, LANE), lambda b: (b, 0, 0))
    return pl.pallas_call(
        _nsa_cmp_kernel,
        out_shape=(jax.ShapeDtypeStruct((B, NSA_CMP_ROWS, LANE), _bf16),) * 2,
        grid=(B,),
        in_specs=[pl.BlockSpec((SEQ, LANE), lambda b: (b, PB['nkc'] // LANE)),
                  pl.BlockSpec((SEQ, LANE), lambda b: (b, PB['nvc'] // LANE)),
                  pl.BlockSpec(pos_x.shape, lambda b: (0,) * pos_x.ndim),
                  pl.BlockSpec(w1_x.shape, lambda b: (0,) * w1_x.ndim),
                  pl.BlockSpec((2, LANE, LANE), lambda b: (0, 0, 0))],
        out_specs=(ospec, ospec),
        compiler_params=pltpu.CompilerParams(dimension_semantics=("parallel",),
                                             vmem_limit_bytes=VMEM_LIMIT),
        name="nsa_compress",
    )(proj_b, proj_b, pos_x, w1_x, w2_x)


def _dot_nt(a, b):
    return lax.dot_general(a, b, (((1,), (1,)), ((), ())), preferred_element_type=_f32)


def _nsa_kernel(q_ref, ks_ref, vs_ref, kw_ref, vw_ref, kc_ref, vc_ref, gate_ref, tb_ref, cb_ref, ov_ref,
                o_ref, m_ref, l_ref, acc_ref, sel_ref):
    R, QB, hd = NSA_R, Q_BLOCK, NSA_HEAD_DIM
    qb = pl.program_id(1)
    kl = lax.broadcasted_iota(jnp.int32, (QB, LANE), 0)
    t_q = qb * QB + lax.broadcasted_iota(jnp.int32, (QB, LANE), 1)
    NSQ = q_ref.shape[0]
    G = NSA_KV_GROUPS
    KB = 2 * QB

    def split_r(x):
        return [x[:, r * LANE:(r + 1) * LANE] for r in range(R)]

    gates_t = [jax.nn.sigmoid(gate_ref[sq]).T for sq in range(NSQ)]
    qps = [[jnp.concatenate([q_ref[sq, :, (R * g + r) * LANE:(R * g + r + 1) * LANE] for r in range(R)], axis=0)
            for g in range(G)] for sq in range(NSQ)]

    krow = lax.broadcasted_iota(jnp.int32, (KB, LANE), 0)
    t_q2 = qb * QB + lax.broadcasted_iota(jnp.int32, (KB, LANE), 1)
    SEL, WIN = 0, 1
    chain = lambda sq, br, g: (sq * 2 + br) * G + g

    def flash_step(work):
        items, scores = [], []
        for p, br in work:
            off = pl.multiple_of(p * KB, KB)
            rel = t_q2 - (p * KB + krow)
            tidx = [jnp.clip(qb - (KB // QB) * p - i, 0, 2) for i in range(KB // QB)]
            for sq in range(NSQ):
                k_blk = (ks_ref, kw_ref)[br][sq, pl.ds(off, KB), :]
                for g in range(G):
                    items.append((p, br, sq, g, off, rel, tidx))
                    scores.append(split_r(_dot_nt(k_blk, qps[sq][g])))
        probs, alphas = [], []
        for (p, br, sq, g, off, rel, tidx), s_t in zip(items, scores):
            ch = chain(sq, br, g)
            if br == SEL:
                blocks = [jnp.broadcast_to(sel_ref[sq * G + g, pl.ds((KB // SLC_BLOCK) * p + i, 1), :],
                                           (SLC_BLOCK, LANE)) for i in range(KB // SLC_BLOCK)]
                mask = (rel >= 0) & (jnp.concatenate(blocks, axis=0) > 0.5)
            else:
                mask = (rel >= 0) & (rel < WINDOW)
            s_t = jnp.concatenate(
                [jnp.where(mask, s_t[r] + jnp.concatenate([tb_ref[ti, R * g + r] for ti in tidx], axis=0), NEG_INF)
                 for r in range(R)], axis=1)
            m_old = m_ref[ch]
            m_new = jnp.maximum(m_old, jnp.max(s_t, axis=0, keepdims=True))
            e = jnp.exp(s_t - m_new)
            alpha = jnp.exp(m_old - m_new)
            l_ref[ch] = alpha * l_ref[ch] + jnp.sum(e, axis=0, keepdims=True)
            m_ref[ch] = m_new
            probs.append(e.astype(_bf16))
            alphas.append(alpha)
        for (p, br, sq, g, off, rel, tidx), e, alpha in zip(items, probs, alphas):
            ch = chain(sq, br, g)
            pv = _dot_tn((vs_ref, vw_ref)[br][sq, pl.ds(off, KB), :], e)
            acc_ref[ch] = alpha * acc_ref[ch] + pv[g * hd:(g + 1) * hd]

    mask_c = (t_q - (kl * CMP_STRIDE + CMP_BLOCK - 1) >= 0) & (kl < NSA_NC)
    sg = [(sq, g) for sq in range(NSQ) for g in range(G)]
    cmp_scores = [split_r(_dot_nt(kc_ref[sq], qps[sq][g])) for sq, g in sg]
    cmp_probs = []
    for (sq, g), s_t in zip(sg, cmp_scores):
        s_t = jnp.concatenate([jnp.where(mask_c, s_t[r] + cb_ref[0, R * g + r], NEG_INF) for r in range(R)], axis=1)
        e = split_r(jnp.exp(s_t - jnp.max(s_t, axis=0, keepdims=True)))
        e = jnp.concatenate([jnp.where(mask_c, e[r], 0.0) for r in range(R)], axis=1)
        den = jnp.sum(e, axis=0, keepdims=True)
        cmp_probs.append((e / jnp.where(den > 0.0, den, 1.0)).astype(_bf16))
    o_cmp = [_dot_tn(vc_ref[sq], p)[g * hd:(g + 1) * hd] for (sq, g), p in zip(sg, cmp_probs)]
    importance = [sum(split_r(jnp.dot(ov_ref[...], p, preferred_element_type=_f32))) for p in cmp_probs]
    j = lax.broadcasted_iota(jnp.int32, (NSA_NSB, LANE), 0)
    cur = (qb * QB + lax.broadcasted_iota(jnp.int32, (NSA_NSB, LANE), 1)) // SLC_BLOCK
    for (sq, g), imp in zip(sg, importance):
        imp = jnp.where((j == 0) | (j == cur) | (j == cur - 1), FORCE_SCORE, imp)
        imp = jnp.where(j <= cur, imp, NEG_INF)
        cnt = jnp.zeros((NSA_NSB, LANE), _f32)
        for jo in range(NSA_NSB):
            other = imp[jo:jo + 1, :]
            beats = (other > imp) | ((other == imp) & (j > jo))
            cnt = cnt + jnp.where(beats, 1.0, 0.0)
        sel_ref[sq * G + g] = jnp.where(cnt < SLC_TOP_N, 1.0, 0.0)

    p_hi = qb // (KB // QB) + 1
    n_win = p_hi - jnp.maximum(qb - WINDOW // QB, 0) // (KB // QB)
    m_ref[...] = jnp.full(m_ref.shape, NEG_INF, _f32)
    l_ref[...] = jnp.zeros(l_ref.shape, _f32)
    acc_ref[...] = jnp.zeros(acc_ref.shape, _f32)

    def both(i, carry):
        p = p_hi - 1 - i
        flash_step([(p, SEL), (p, WIN)])
        return carry

    def selected_pair(i, carry):
        p = p_hi - 1 - n_win - 2 * i
        flash_step([(p, SEL), (p - 1, SEL)])
        return carry

    def selected_one(i, carry):
        flash_step([(0, SEL)])
        return carry

    n_sel = p_hi - n_win
    lax.fori_loop(0, n_win, both, 0)
    lax.fori_loop(0, n_sel // 2, selected_pair, 0)
    lax.fori_loop(0, n_sel % 2, selected_one, 0)
    for sq in range(NSQ):
        heads = [None] * NSA_HEADS
        for g in range(G):
            cs, cw = chain(sq, SEL, g), chain(sq, WIN, g)
            o_c, o_s, o_w = split_r(o_cmp[sq * G + g]), split_r(acc_ref[cs] / l_ref[cs]), split_r(acc_ref[cw] / l_ref[cw])
            for r in range(R):
                h = R * g + r
                gt = gates_t[sq]
                heads[h] = (gt[3 * h:3 * h + 1] * o_c[r] + gt[3 * h + 1:3 * h + 2] * o_s[r]
                            + gt[3 * h + 2:3 * h + 3] * o_w[r])
        for pk in range(NSA_HEADS * hd // LANE):
            per = LANE // hd
            pair = jnp.concatenate(heads[per * pk:per * (pk + 1)], axis=0)
            o_ref[sq, :, pk * LANE:(pk + 1) * LANE] = pair.T.astype(o_ref.dtype)


def _nsa_attention(proj_a, proj_b, kc, vc, tb, cb, overlap_t):
    B = kc.shape[0]
    nsq = NSA_SEQS_PER_STEP
    nqb = NSA_NQB
    qcols = NSA_HEADS * LANE
    out_cols = NSA_HEADS * NSA_HEAD_DIM
    pa = proj_a.reshape(B, SEQ, proj_a.shape[1])
    pb = proj_b.reshape(B, SEQ, proj_b.shape[1])
    kv = lambda blk: pl.BlockSpec((nsq, SEQ, LANE), lambda b, q: (b, 0, qcols // LANE + blk))
    cspec = pl.BlockSpec((nsq, NSA_CMP_ROWS, LANE), lambda b, q: (b, 0, 0))
    nchain = nsq * 2 * NSA_KV_GROUPS
    out = pl.pallas_call(
        _nsa_kernel,
        out_shape=jax.ShapeDtypeStruct((B, SEQ, out_cols), _bf16),
        grid=(B // nsq, nqb),
        in_specs=[pl.BlockSpec((nsq, Q_BLOCK, qcols), lambda b, q: (b, q, 0)),
                  kv(0), kv(1), kv(2), kv(3), cspec, cspec,
                  pl.BlockSpec((nsq, Q_BLOCK, LANE), lambda b, q: (b, q, PB['misc'] // LANE)),
                  pl.BlockSpec((3, NSA_HEADS, Q_BLOCK, LANE), lambda b, q: (0, 0, 0, 0)),
                  pl.BlockSpec((1, NSA_HEADS, Q_BLOCK, LANE), lambda b, q: (q, 0, 0, 0)),
                  pl.BlockSpec((NSA_NSB, LANE), lambda b, q: (0, 0))],
        out_specs=pl.BlockSpec((nsq, Q_BLOCK, out_cols), lambda b, q: (b, q, 0)),
        scratch_shapes=[pltpu.VMEM((nchain, 1, NSA_R * Q_BLOCK), _f32),
                        pltpu.VMEM((nchain, 1, NSA_R * Q_BLOCK), _f32),
                        pltpu.VMEM((nchain, NSA_HEAD_DIM, NSA_R * Q_BLOCK), _f32),
                        pltpu.VMEM((nsq * NSA_KV_GROUPS, NSA_NSB, Q_BLOCK), _f32)],
        compiler_params=pltpu.CompilerParams(dimension_semantics=("parallel", "arbitrary"),
                                             vmem_limit_bytes=VMEM_LIMIT),
        name="nsa_attention",
    )(pa, pa, pa, pa, pa, kc, vc, pb, tb, cb, overlap_t)
    return out.reshape(B * SEQ, out_cols)


def _nsa_overlap():
    n = jnp.arange(NSA_CMP_ROWS)[None, :]
    jj = jnp.arange(NSA_NSB)[:, None]
    return ((n * CMP_STRIDE <= jj * SLC_BLOCK + SLC_BLOCK - 1)
            & (n * CMP_STRIDE + CMP_BLOCK - 1 >= jj * SLC_BLOCK) & (n < NSA_NC)).astype(_bf16)


def _nsa_weight_prep(w_in, cmp_pos, cmp_w1, cmp_w2):
    G, hd = NSA_KV_GROUPS, NSA_HEAD_DIM
    eye = jnp.eye(G, dtype=_f32)
    nq = NSA_HEADS * hd
    w_in = w_in.astype(_bf16)
    zeros = jnp.zeros((DEPTH, D_MODEL, hd), _bf16)
    q_cols = []
    for h in range(NSA_HEADS):
        wq_h = w_in[:, :, h * hd:(h + 1) * hd] * (hd ** -0.5)
        q_cols += [wq_h, zeros] if h // NSA_R == 0 else [zeros, wq_h]
    slc_win = w_in[:, :, nq + 2 * NSA_KV:nq + 6 * NSA_KV]
    w_a = jnp.concatenate(q_cols + [slc_win], axis=-1)
    w_b = jnp.concatenate([w_in[:, :, _IN_OFF[n]:_IN_OFF[n] + w] for n, w in _PB_SEGMENTS]
                          + [jnp.zeros((DEPTH, D_MODEL, PROJ_B_COLS - PROJ_B_USED), _bf16)], axis=-1)
    half = CMP_BLOCK // 2
    w1 = cmp_w1.reshape(DEPTH, 2, CMP_BLOCK, hd, hd)
    w1_x = jnp.einsum('lxide,gh->lxigdhe', w1, eye).reshape(DEPTH, 2, 2, half, G * hd, G * hd).astype(_bf16)
    w2_x = jnp.einsum('lxde,gh->lxgdhe', cmp_w2, eye).reshape(DEPTH, 2, G * hd, G * hd).astype(_bf16)
    pos_x = jnp.broadcast_to(cmp_pos[:, :, :, None, :], (DEPTH, 2, CMP_BLOCK, G, hd))
    pos_x = pos_x.reshape(DEPTH, 2, 2, half, 1, G * hd)
    return w_a, w_b, w1_x, w2_x, pos_x


REC_BLOCK = 2 * CHUNK
REC_SEQS_PER_STEP = {'ssd': 2, 'gdn': 4, 'gla': 4}
assert REC_BLOCK == LANE and SEQ % REC_BLOCK == 0 and all(BATCH % n == 0 for n in REC_SEQS_PER_STEP.values())


def _split3(x):
    hi = x.astype(_bf16)
    r = x - hi.astype(_f32)
    mid = r.astype(_bf16)
    lo = (r - mid.astype(_f32)).astype(_bf16)
    return hi, mid, lo


def _dot_exact_rhs(m, x):
    return sum(jnp.dot(m, p, preferred_element_type=_f32) for p in _split3(x))


def _dot_exact_lhs(x, m):
    return sum(jnp.dot(p, m, preferred_element_type=_f32) for p in _split3(x))


def _dot_tn(a, b):
    return lax.dot_general(a, b, (((0,), (0,)), ((), ())), preferred_element_type=_f32)


def _mm3(x, y):
    xh = x.astype(_bf16)
    xl = (x - xh.astype(_f32)).astype(_bf16)
    yh = y.astype(_bf16)
    yl = (y - yh.astype(_f32)).astype(_bf16)
    return (jnp.dot(xh, yh, preferred_element_type=_f32) + jnp.dot(xh, yl, preferred_element_type=_f32)
            + jnp.dot(xl, yh, preferred_element_type=_f32))


def _chunk_masks():
    ri = lax.broadcasted_iota(jnp.int32, (REC_BLOCK, REC_BLOCK), 0)
    ci = lax.broadcasted_iota(jnp.int32, (REC_BLOCK, REC_BLOCK), 1)
    same = (ri // CHUNK) == (ci // CHUNK)
    return ri, ci, (ci <= ri) & same, (ci < ri) & same, (ri <= ci) & same


def _as_mxu(mask):
    return jnp.where(mask, 1.0, 0.0).astype(_bf16)


def _softplus(x):
    return jnp.maximum(x, 0.0) + jnp.log1p(jnp.exp(-jnp.abs(x)))


def _silu(x):
    return x * jax.nn.sigmoid(x)


def _chunk_last(x):
    ri = lax.broadcasted_iota(jnp.int32, x.shape, 0)
    return jnp.where(ri < CHUNK, x[CHUNK - 1:CHUNK], x[2 * CHUNK - 1:2 * CHUNK])


def _expand_heads(x, lane0, nheads, width):
    per = LANE // width
    lane = lax.broadcasted_iota(jnp.int32, (x.shape[0], LANE), 1)
    pieces = []
    for p0 in range(lane0, lane0 + nheads, per):
        piece = jnp.broadcast_to(x[:, p0:p0 + 1], (x.shape[0], LANE))
        for k in range(1, per):
            piece = jnp.where(lane < k * width, piece, jnp.broadcast_to(x[:, p0 + k:p0 + k + 1], (x.shape[0], LANE)))
        pieces.append(piece)
    return jnp.concatenate(pieces, axis=1) if len(pieces) > 1 else pieces[0]


def _conv_silu(x, prev, w_ref, c0, bias=None):
    n, C = x.shape
    ntap = w_ref.shape[0]
    rows = lax.broadcasted_iota(jnp.int32, (n, C), 0)
    acc = x * w_ref[ntap - 1:ntap, c0:c0 + C]
    for s in range(1, ntap):
        xs = jnp.where(rows < s, pltpu.roll(prev, s, 0), pltpu.roll(x, s, 0))
        acc = acc + xs * w_ref[ntap - 1 - s:ntap - s, c0:c0 + C]
    if bias is not None:
        acc = acc + bias
    return _silu(acc)


def _rms(x, w):
    return x * lax.rsqrt(jnp.mean(x * x, axis=-1, keepdims=True) + EPS) * w


def _ssd_kernel(z_ref, xbc_ref, dt_ref, cw_ref, cb_ref, dtb_ref, alog_ref, dskip_ref, nw_ref, o_ref,
                prev_ref, state_ref):
    @pl.when(pl.program_id(1) == 0)
    def _():
        prev_ref[...] = jnp.zeros(prev_ref.shape, _f32)
        state_ref[...] = jnp.zeros(state_ref.shape, _f32)

    for ns in range(z_ref.shape[0]):
        _ssd_block(z_ref.at[ns], xbc_ref.at[ns], dt_ref.at[ns], cw_ref, cb_ref, dtb_ref, alog_ref, dskip_ref, nw_ref,
                   o_ref.at[ns], prev_ref.at[ns], state_ref.at[ns])


def _ssd_block(z_ref, xbc_ref, dt_ref, cw_ref, cb_ref, dtb_ref, alog_ref, dskip_ref, nw_ref, o_ref,
               prev_ref, state_ref):
    G, R, P, N = SSD_GROUPS, SSD_HEADS // SSD_GROUPS, SSD_HEAD_DIM, SSD_STATE
    x_in = xbc_ref[...]
    xc = _conv_silu(x_in, prev_ref[...], cw_ref, 0, cb_ref[...])
    prev_ref[...] = x_in
    _, _, tril, _, triu = _chunk_masks()
    lane = lax.broadcasted_iota(jnp.int32, (REC_BLOCK, LANE), 1)
    L0 = PBL['sdt']
    dt = _softplus(dt_ref[...] + dtb_ref[...])
    da = dt * (-jnp.exp(alog_ref[...]))
    a_cum = _dot_exact_rhs(_as_mxu(tril), da)
    a_cum_t = _dot_exact_lhs(da.T, _as_mxu(triu))
    a_last = _chunk_last(a_cum)
    xs = xc[:, :SSD_INNER]
    xdt = xs * _expand_heads(dt, L0, SSD_HEADS, P)
    xdtd = (xdt * _expand_heads(jnp.exp(a_last - a_cum), L0, SSD_HEADS, P)).astype(_bf16)
    xdt_b = xdt.astype(_bf16)
    ea = _expand_heads(jnp.exp(a_cum), L0, SSD_HEADS, P)
    y_groups = []
    for g in range(G):
        bm = xc[:, SSD_INNER + g * N:SSD_INNER + (g + 1) * N].astype(_bf16)
        cm = xc[:, SSD_INNER + (G + g) * N:SSD_INNER + (G + g + 1) * N].astype(_bf16)
        cbm = _dot_nt(cm, bm)
        intra = []
        for pr in range(R // 2):
            both = []
            for k in range(2):
                h = g * R + 2 * pr + k
                seg = jnp.exp(jnp.where(tril, a_cum[:, L0 + h:L0 + h + 1] - a_cum_t[L0 + h:L0 + h + 1, :], -jnp.inf))
                both.append(jnp.dot((cbm * seg).astype(_bf16), xdt_b[:, (h - k) * P:(h - k + 2) * P],
                                    preferred_element_type=_f32))
            intra.append(jnp.where(lane < P, both[0], both[1]))
        y_intra = jnp.concatenate(intra, axis=1)
        prev_rows = []
        for c in range(REC_BLOCK // CHUNK):
            rows = slice(c * CHUNK, (c + 1) * CHUNK)
            st = state_ref[g]
            prev_rows.append(jnp.dot(cm[rows], st.astype(_bf16), preferred_element_type=_f32))
            dec = _expand_heads(jnp.exp(a_cum[(c + 1) * CHUNK - 1:(c + 1) * CHUNK]), L0, SSD_HEADS, P)
            state_ref[g] = (st * dec[:, g * R * P:(g + 1) * R * P]
                            + _dot_tn(bm[rows], xdtd[rows, g * R * P:(g + 1) * R * P]))
        y_groups.append(y_intra + jnp.concatenate(prev_rows, axis=0) * ea[:, g * R * P:(g + 1) * R * P])
    y = jnp.concatenate(y_groups, axis=1) + xs * dskip_ref[...]
    y = y * _silu(z_ref[...])
    gw = SSD_INNER // G
    for g in range(G):
        o_ref[:, g * gw:(g + 1) * gw] = _rms(y[:, g * gw:(g + 1) * gw], nw_ref[:, g * gw:(g + 1) * gw]).astype(o_ref.dtype)


def _gdn_kernel(q_ref, k_ref, v_ref, z_ref, ba_ref, cw_ref, dtb_ref, alog_ref, nw_ref, o_ref,
                pq_ref, pk_ref, pv_ref, state_ref):
    H, Dh = GDN_HEADS, GDN_HEAD_DIM

    @pl.when(pl.program_id(1) == 0)
    def _():
        for r in (pq_ref, pk_ref, pv_ref, state_ref):
            r[...] = jnp.zeros(r.shape, _f32)

    NS = q_ref.shape[0]
    ri, ci, tril, strict, triu = _chunk_masks()
    q, k, v, beta, gcum, gcum_t, glast = [], [], [], [], [], [], []
    for ns in range(NS):
        for dst, x_ref, p_ref, c0 in ((q, q_ref, pq_ref, 0), (k, k_ref, pk_ref, GDN_WIDTH), (v, v_ref, pv_ref, 2 * GDN_WIDTH)):
            x_in = x_ref[ns]
            dst.append(_conv_silu(x_in, p_ref[ns], cw_ref, c0))
            p_ref[ns] = x_in
        ba = ba_ref[ns]
        beta.append(jax.nn.sigmoid(ba))
        gl = -jnp.exp(alog_ref[...]) * _softplus(ba + dtb_ref[...])
        gcum.append(_dot_exact_rhs(_as_mxu(tril), gl))
        gcum_t.append(_dot_exact_lhs(gl.T, _as_mxu(triu)))
        glast.append(_chunk_last(gcum[ns]))
    zero_rows = jnp.zeros((CHUNK, Dh), _f32)
    eye = jnp.where(tril & jnp.logical_not(strict), 1.0, 0.0)
    hs = range(NS * H)
    seq = [i // H for i in hs]
    sls = [slice((i % H) * Dh, (i % H + 1) * Dh) for i in hs]
    qh = [q[seq[i]][:, sls[i]] for i in hs]
    kh = [k[seq[i]][:, sls[i]] for i in hs]
    qq = [qh[i] * lax.rsqrt(jnp.sum(qh[i] * qh[i], axis=-1, keepdims=True) + EPS) * (Dh ** -0.5) for i in hs]
    kk = [kh[i] * lax.rsqrt(jnp.sum(kh[i] * kh[i], axis=-1, keepdims=True) + EPS) for i in hs]
    lb = [PBL['gbeta'] + i % H for i in hs]
    lg = [PBL['gbeta'] + H + i % H for i in hs]
    bcol = [beta[seq[h]][:, lb[h]:lb[h] + 1] for h in hs]
    gcol = [gcum[seq[h]][:, lg[h]:lg[h] + 1] for h in hs]
    decay = [jnp.exp(jnp.where(tril, gcol[h] - gcum_t[seq[h]][lg[h]:lg[h] + 1, :], -jnp.inf)) for h in hs]
    kb = [kk[h] * bcol[h] for h in hs]
    s = [_dot_nt(jnp.concatenate([kb[h], qq[h]], axis=0).astype(_bf16), kk[h].astype(_bf16)) for h in hs]
    a_mat = [jnp.where(strict, s[h][:REC_BLOCK] * decay[h], 0.0) for h in hs]
    aqk = [(s[h][REC_BLOCK:] * decay[h]).astype(_bf16) for h in hs]
    SUB = 8
    same = lambda n: (ri // n) == (ci // n)
    a_sub = [jnp.where(same(SUB), a_mat[h], 0.0) for h in hs]
    tinv = [eye - a_sub[h] for h in hs]
    pw = a_sub
    for _ in range(SUB.bit_length() - 2):
        pw = [_mm3(pw[h], pw[h]) for h in hs]
        tinv = [tinv[h] + _mm3(tinv[h], pw[h]) for h in hs]
    n = SUB
    while n < CHUNK:
        enclosed = same(2 * n) & jnp.logical_not(same(n))
        tc = [_mm3(tinv[h], jnp.where(enclosed, a_mat[h], 0.0)) for h in hs]
        tinv = [tinv[h] - _mm3(tc[h], tinv[h]) for h in hs]
        n *= 2
    sol = [_mm3(tinv[h], jnp.concatenate([v[seq[h]][:, sls[h]] * bcol[h], kb[h] * jnp.exp(gcol[h])], axis=1))
           for h in hs]
    u = [sol[h][:, :Dh] for h in hs]
    w = [sol[h][:, Dh:].astype(_bf16) for h in hs]
    q_dec = [(qq[h] * jnp.exp(gcol[h])).astype(_bf16) for h in hs]
    k_end = [(kk[h] * jnp.exp(glast[seq[h]][:, lg[h]:lg[h] + 1] - gcol[h])).astype(_bf16) for h in hs]
    o_rows = [[] for _ in hs]
    for c in range(REC_BLOCK // CHUNK):
        rows = slice(c * CHUNK, (c + 1) * CHUNK)
        st = [state_ref[h] for h in hs]
        st_b = [st[h].astype(_bf16) for h in hs]
        v_new = [u[h][rows] - jnp.dot(w[h][rows], st_b[h], preferred_element_type=_f32) for h in hs]
        v_full = [jnp.concatenate([v_new[h], zero_rows] if c == 0 else [zero_rows, v_new[h]], axis=0).astype(_bf16)
                  for h in hs]
        for h in hs:
            o_rows[h].append(jnp.dot(q_dec[h][rows], st_b[h], preferred_element_type=_f32)
                             + jnp.dot(aqk[h][rows], v_full[h], preferred_element_type=_f32))
            d_last = jnp.exp(gcum[seq[h]][(c + 1) * CHUNK - 1:(c + 1) * CHUNK, lg[h]:lg[h] + 1])
            state_ref[h] = st[h] * d_last + _dot_tn(k_end[h][rows], v_new[h].astype(_bf16))
    for h in hs:
        o = _rms(jnp.concatenate(o_rows[h], axis=0), nw_ref[...]) * _silu(z_ref[seq[h], :, sls[h]])
        o_ref[seq[h], :, sls[h]] = o.astype(o_ref.dtype)


def _gla_kernel(q_ref, k_ref, v_ref, go_ref, lr_ref, w2_ref, gb_ref, nw_ref, o_ref, state_ref):
    @pl.when(pl.program_id(1) == 0)
    def _():
        state_ref[...] = jnp.zeros(state_ref.shape, _f32)

    for ns in range(q_ref.shape[0]):
        _gla_block(q_ref.at[ns], k_ref.at[ns], v_ref.at[ns], go_ref.at[ns], lr_ref.at[ns], w2_ref, gb_ref, nw_ref,
                   o_ref.at[ns], state_ref.at[ns])


def _gla_block(q_ref, k_ref, v_ref, go_ref, lr_ref, w2_ref, gb_ref, nw_ref, o_ref, state_ref):
    H, Dk, Dv = GLA_HEADS, GLA_DK, GLA_DV
    _, _, tril, _, _ = _chunk_masks()
    lane = lax.broadcasted_iota(jnp.int32, (REC_BLOCK, LANE), 1)
    pre = jnp.dot(lr_ref[...].astype(_bf16), w2_ref[...], preferred_element_type=_f32) + gb_ref[...]
    gk = (jnp.minimum(pre, 0.0) - jnp.log1p(jnp.exp(-jnp.abs(pre)))) / GLA_GATE_NORM
    bcum = _dot_exact_rhs(_as_mxu(tril), gk)
    blast = _chunk_last(bcum)
    q_dec = q_ref[...] * (Dk ** -0.5) * jnp.exp(bcum)
    k_inv = (k_ref[...] * jnp.exp(-bcum)).astype(_bf16)
    k_end = (k_ref[...] * jnp.exp(blast - bcum)).astype(_bf16)
    per = LANE // Dk
    nchunk = REC_BLOCK // CHUNK
    chunk_rows = [slice(c * CHUNK, (c + 1) * CHUNK) for c in range(nchunk)]
    psl = [slice(pr * LANE, (pr + 1) * LANE) for pr in range(H // per)]
    qm = [jnp.where(lane // Dk == h % per, q_dec[:, psl[h // per]], 0.0).astype(_bf16) for h in range(H)]
    vh = [v_ref[:, h * Dv:(h + 1) * Dv].astype(_bf16) for h in range(H)]
    attn = [jnp.where(tril, _dot_nt(qm[h], k_inv[:, psl[h // per]]), 0.0).astype(_bf16) for h in range(H)]
    o_intra = [jnp.dot(attn[h], vh[h], preferred_element_type=_f32) for h in range(H)]
    local = []
    for rows in chunk_rows:
        per_pair = []
        for pr in range(H // per):
            loc = None
            for k in range(per):
                lk = _dot_tn(vh[pr * per + k][rows], k_end[rows, psl[pr]])
                loc = lk if loc is None else jnp.where(lane < k * Dk, loc, lk)
            per_pair.append(loc)
        local.append(per_pair)
    o_prev = [[] for _ in range(H)]
    for c, rows in enumerate(chunk_rows):
        for pr in range(H // per):
            st = state_ref[pr]
            st_b = st.astype(_bf16)
            for k in range(per):
                o_prev[pr * per + k].append(_dot_nt(qm[pr * per + k][rows], st_b))
            state_ref[pr] = st * jnp.exp(bcum[(c + 1) * CHUNK - 1:(c + 1) * CHUNK, psl[pr]]) + local[c][pr]
    for h in range(H):
        o = o_intra[h] + jnp.concatenate(o_prev[h], axis=0)
        o = _rms(o, nw_ref[...]) * _silu(go_ref[:, h * Dv:(h + 1) * Dv])
        o_ref[:, h * Dv:(h + 1) * Dv] = o.astype(o_ref.dtype)


def _rec_call(body, proj_b, col_blocks, params, out_cols, scratch, name, nseq=None):
    B = proj_b.shape[0] // SEQ
    nblk = SEQ // REC_BLOCK
    if nseq is None:
        src, grid = proj_b, (B, nblk)
        in_specs = [pl.BlockSpec((REC_BLOCK, w), (lambda b, t, c=c0 // w: (b * nblk + t, c))) for c0, w in col_blocks]
        out_shape = jax.ShapeDtypeStruct((B * SEQ, out_cols), _bf16)
        out_spec = pl.BlockSpec((REC_BLOCK, out_cols), lambda b, t: (b * nblk + t, 0))
    else:
        src, grid = proj_b.reshape(B, SEQ, proj_b.shape[1]), (B // nseq, nblk)
        in_specs = [pl.BlockSpec((nseq, REC_BLOCK, w), (lambda b, t, c=c0 // w: (b, t, c))) for c0, w in col_blocks]
        out_shape = jax.ShapeDtypeStruct((B, SEQ, out_cols), _bf16)
        out_spec = pl.BlockSpec((nseq, REC_BLOCK, out_cols), lambda b, t: (b, t, 0))
    for p in params:
        in_specs.append(pl.BlockSpec(p.shape, lambda b, t, nd=p.ndim: (0,) * nd))
    out = pl.pallas_call(
        body,
        out_shape=out_shape,
        grid=grid,
        in_specs=in_specs,
        out_specs=out_spec,
        scratch_shapes=scratch,
        compiler_params=pltpu.CompilerParams(dimension_semantics=("parallel", "arbitrary"),
                                             vmem_limit_bytes=VMEM_LIMIT),
        name=name,
    )(*([src] * len(col_blocks)), *params)
    return out.reshape(B * SEQ, out_cols)


def _lane_pad(v, lane0=0):
    return jnp.pad(v.astype(_f32), (lane0, LANE - lane0 - v.shape[0]))[None]


def _ssd_call(proj_b, conv_w, conv_b, dt_bias, a_log, d_skip, norm_w):
    cols = [(PB['sz'], SSD_INNER), (PB['sxbc'], SSD_XBC), (PB['misc'], LANE)]
    params = [conv_w, conv_b[None], _lane_pad(dt_bias, PBL['sdt']), _lane_pad(a_log, PBL['sdt']),
              jnp.repeat(d_skip, SSD_HEAD_DIM)[None], norm_w[None]]
    ns = REC_SEQS_PER_STEP['ssd']
    scratch = [pltpu.VMEM((ns, REC_BLOCK, SSD_XBC), _f32),
               pltpu.VMEM((ns, SSD_GROUPS, SSD_STATE, SSD_INNER // SSD_GROUPS), _f32)]
    return _rec_call(_ssd_kernel, proj_b, cols, params, SSD_INNER, scratch, "ssd", nseq=ns)


def _gdn_call(proj_b, conv_w, dt_bias, a_log, norm_w):
    W = GDN_WIDTH
    cols = [(PB['gq'], W), (PB['gk'], W), (PB['gv'], W), (PB['gz'], W), (PB['misc'], LANE)]
    decay_lane = PBL['gbeta'] + GDN_HEADS
    params = [conv_w, _lane_pad(dt_bias, decay_lane), _lane_pad(a_log, decay_lane), norm_w[None]]
    ns = REC_SEQS_PER_STEP['gdn']
    scratch = ([pltpu.VMEM((ns, REC_BLOCK, W), _f32)] * 3
               + [pltpu.VMEM((ns * GDN_HEADS, GDN_HEAD_DIM, GDN_HEAD_DIM), _f32)])
    return _rec_call(_gdn_kernel, proj_b, cols, params, W, scratch, "gdn", nseq=ns)


def _gla_call(proj_b, gate_w2, gate_b, norm_w):
    cols = [(PB['lq'], GLA_KEY), (PB['lk'], GLA_KEY), (PB['lv'], GLA_VAL), (PB['lg'], GLA_VAL), (PB['misc'], LANE)]
    w2 = jnp.pad(gate_w2, ((PBL['llr'], LANE - PBL['llr'] - GLA_GATE_RANK), (0, 0))).astype(_bf16)
    params = [w2, gate_b[None], norm_w[None]]
    ns = REC_SEQS_PER_STEP['gla']
    scratch = [pltpu.VMEM((ns, GLA_HEADS * GLA_DK // LANE, GLA_DV, LANE), _f32)]
    return _rec_call(_gla_kernel, proj_b, cols, params, GLA_VAL, scratch, "gla", nseq=ns)


def kernel(x, c, rel_bias, norm1_w, norm2_w, ada_w, ada_b, w_in, w_out, nsa_cmp_pos, nsa_cmp_w1, nsa_cmp_w2, ssd_conv_w, ssd_conv_b, ssd_dt_bias, ssd_a_log, ssd_d, ssd_norm_w, gdn_conv_w, gdn_dt_bias, gdn_a_log, gdn_norm_w, gla_gate_w2, gla_gate_b, gla_norm_w, mlp_w1, mlp_w2, final_norm_w):
    B, S, D = x.shape
    mod = _ada_all(c, ada_w, ada_b).reshape(DEPTH, B, 6, 1, D)
    w_a, w_b, cmp_w1_x, cmp_w2_x, cmp_pos_x = _nsa_weight_prep(w_in, nsa_cmp_pos, nsa_cmp_w1, nsa_cmp_w2)
    w_out_b = w_out.astype(_bf16)
    w1_b = mlp_w1.astype(_bf16)
    w2_b = mlp_w2.astype(_bf16)
    tb, cb = _nsa_bias_tiles(rel_bias)
    overlap_t = _nsa_overlap()
    xf = x.reshape(TOKENS, D)
    no_mod = jnp.zeros((B, 1, D), _f32)
    h = _norm_mod(xf, norm1_w[0][None], mod[0, :, 1], mod[0, :, 0])
    for l in range(DEPTH):
        sh1, sc1, g1, sh2, sc2, g2 = (mod[l, :, i] for i in range(6))
        proj_a = _matmul(h, w_a, l, out_dtype=_bf16)
        proj_b = _matmul(h, w_b, l, tn=PROJ_B_TN)
        kc, vc = _nsa_compress(proj_b, cmp_pos_x[l], cmp_w1_x[l], cmp_w2_x[l])
        y_nsa = _nsa_attention(proj_a, proj_b, kc, vc, tb, cb, overlap_t)
        y_ssd = _ssd_call(proj_b, ssd_conv_w[l], ssd_conv_b[l], ssd_dt_bias[l], ssd_a_log[l], ssd_d[l], ssd_norm_w[l])
        y_gdn = _gdn_call(proj_b, gdn_conv_w[l], gdn_dt_bias[l], gdn_a_log[l], gdn_norm_w[l])
        y_gla = _gla_call(proj_b, gla_gate_w2[l], gla_gate_b[l], gla_norm_w[l])
        xf, h2 = _out_proj((y_nsa, y_ssd, y_gdn, y_gla), w_out_b, xf, g1,
                           norm2_w[l][None], sc2, sh2, layer=l)
        if l + 1 < DEPTH:
            xf, h = _mlp(h2, w1_b, w2_b, xf, g2, norm1_w[l + 1][None], mod[l + 1, :, 1], mod[l + 1, :, 0], _bf16,
                         layer=l)
        else:
            _, out = _mlp(h2, w1_b, w2_b, xf, g2, final_norm_w[None], no_mod, no_mod, _f32, layer=l)
    return out.reshape(B, S, D)
```

```python
import math
from functools import partial

import jax
import jax.numpy as jnp
from jax import lax
from jax.experimental import pallas as pl
from jax.experimental.pallas import tpu as pltpu

D_MODEL = 2048
BATCH = 16
SEQ = 2048
DEPTH = 4

MIX_GROUP = D_MODEL // 4
NSA_HEAD_DIM = 64
NSA_HEADS = MIX_GROUP // NSA_HEAD_DIM
NSA_KV_GROUPS = max(1, NSA_HEADS // 4)
NSA_KV = NSA_KV_GROUPS * NSA_HEAD_DIM
CMP_BLOCK = 32
CMP_STRIDE = 16
SLC_BLOCK = 64
SLC_TOP_N = 8
WINDOW = 512
Q_BLOCK = 128
REL_BUCKETS = 32
REL_MAX_DIST = 128
SSD_HEAD_DIM = 64
SSD_HEADS = MIX_GROUP // SSD_HEAD_DIM
SSD_INNER = SSD_HEADS * SSD_HEAD_DIM
SSD_GROUPS = 2
SSD_STATE = 128
SSD_CONV = 4
SSD_XBC = SSD_INNER + 2 * SSD_GROUPS * SSD_STATE
GDN_HEAD_DIM = 128
GDN_HEADS = MIX_GROUP // GDN_HEAD_DIM
GDN_WIDTH = GDN_HEADS * GDN_HEAD_DIM
GDN_CONV = 4
GLA_DV = 128
GLA_HEADS = MIX_GROUP // GLA_DV
GLA_DK = GLA_DV // 2
GLA_KEY = GLA_HEADS * GLA_DK
GLA_VAL = GLA_HEADS * GLA_DV
GLA_GATE_RANK = 16
GLA_GATE_NORM = 16.0
CHUNK = 64
MLP_HIDDEN = 4 * D_MODEL
EPS = 1e-6
NEG_INF = -1e30
FORCE_SCORE = 1e9
IN_SPLITS = (NSA_HEADS * NSA_HEAD_DIM, NSA_KV, NSA_KV, NSA_KV, NSA_KV, NSA_KV, NSA_KV, NSA_HEADS * 3,
             SSD_INNER, SSD_XBC, SSD_HEADS,
             GDN_WIDTH, GDN_WIDTH, GDN_WIDTH, GDN_WIDTH, GDN_HEADS, GDN_HEADS,
             GLA_KEY, GLA_KEY, GLA_VAL, GLA_VAL, GLA_GATE_RANK)
IN_COLS = sum(IN_SPLITS)
MIX_OUT = NSA_HEADS * NSA_HEAD_DIM + SSD_INNER + GDN_WIDTH + GLA_VAL

LANE = 128
VMEM_LIMIT = 56 * 1024 * 1024
TOKENS = BATCH * SEQ
_IN_NAMES = ('nq', 'nkc', 'nvc', 'nks', 'nvs', 'nkw', 'nvw', 'ngate', 'sz', 'sxbc', 'sdt',
             'gq', 'gk', 'gv', 'gz', 'gbeta', 'ga', 'lq', 'lk', 'lv', 'lg', 'llr')
_IN_W = dict(zip(_IN_NAMES, IN_SPLITS))
_IN_OFF = {n: sum(IN_SPLITS[:i]) for i, n in enumerate(_IN_NAMES)}
_PB_SEGMENTS = (('sxbc', SSD_XBC), ('gq', GDN_WIDTH), ('gk', GDN_WIDTH), ('gv', GDN_WIDTH), ('gz', GDN_WIDTH),
                ('sz', SSD_INNER), ('lv', GLA_VAL), ('lg', GLA_VAL), ('lq', GLA_KEY), ('lk', GLA_KEY),
                ('nkc', NSA_KV), ('nvc', NSA_KV),
                ('ngate', NSA_HEADS * 3), ('sdt', SSD_HEADS), ('gbeta', 2 * GDN_HEADS), ('llr', GLA_GATE_RANK))
_PB_MISC = ('ngate', 'sdt', 'gbeta', 'llr')
PB, PBL = {}, {}
PROJ_B_USED = 0
for _n, _w in _PB_SEGMENTS:
    if _n in _PB_MISC:
        PB.setdefault('misc', PROJ_B_USED - PROJ_B_USED % LANE)
        PBL[_n] = PROJ_B_USED - PB['misc']
    else:
        assert PROJ_B_USED % _w == 0 and _w % LANE == 0
        PB[_n] = PROJ_B_USED
    PROJ_B_USED += _w
assert PROJ_B_USED - PB['misc'] <= LANE and PBL['ngate'] == 0 and _IN_OFF['ga'] == _IN_OFF['gbeta'] + GDN_HEADS
PROJ_B_TN = 512
PROJ_B_COLS = -(-PROJ_B_USED // PROJ_B_TN) * PROJ_B_TN

_bf16 = jnp.bfloat16
_f32 = jnp.float32


def _ada_kernel(c_ref, w_ref, b_ref, o_ref):
    c = c_ref[...]
    c_act = c * jax.nn.sigmoid(c)
    o_ref[0] = jnp.dot(c_act, w_ref[0], preferred_element_type=_f32) + b_ref[0]


def _ada_all(c, ada_w, ada_b):
    tn = 1024
    return pl.pallas_call(
        _ada_kernel,
        out_shape=jax.ShapeDtypeStruct((DEPTH, BATCH, 6 * D_MODEL), _f32),
        grid=(DEPTH, 6 * D_MODEL // tn),
        in_specs=[pl.BlockSpec((BATCH, D_MODEL), lambda l, j: (0, 0)),
                  pl.BlockSpec((1, D_MODEL, tn), lambda l, j: (l, 0, j)),
                  pl.BlockSpec((1, 1, tn), lambda l, j: (l, 0, j))],
        out_specs=pl.BlockSpec((1, BATCH, tn), lambda l, j: (l, 0, j)),
        compiler_params=pltpu.CompilerParams(dimension_semantics=("parallel", "parallel"),
                                             vmem_limit_bytes=VMEM_LIMIT),
        name="ada_mod",
    )(c, ada_w, ada_b.reshape(DEPTH, 1, 6 * D_MODEL))


def _rms_mod(x, w, sc, sh):
    y = x * lax.rsqrt(jnp.mean(x * x, axis=-1, keepdims=True) + EPS)
    return (y * w) * (1.0 + sc) + sh


def _norm_mod_kernel(x_ref, w_ref, sc_ref, sh_ref, o_ref):
    o_ref[...] = _rms_mod(x_ref[...], w_ref[...], sc_ref[0], sh_ref[0]).astype(o_ref.dtype)


def _norm_mod(x, w, sc, sh, tm=512):
    per_b = SEQ // tm
    return pl.pallas_call(
        _norm_mod_kernel,
        out_shape=jax.ShapeDtypeStruct((TOKENS, D_MODEL), _bf16),
        grid=(TOKENS // tm,),
        in_specs=[pl.BlockSpec((tm, D_MODEL), lambda i: (i, 0)),
                  pl.BlockSpec((1, D_MODEL), lambda i: (0, 0)),
                  pl.BlockSpec((1, 1, D_MODEL), lambda i: (i // per_b, 0, 0)),
                  pl.BlockSpec((1, 1, D_MODEL), lambda i: (i // per_b, 0, 0))],
        out_specs=pl.BlockSpec((tm, D_MODEL), lambda i: (i, 0)),
        compiler_params=pltpu.CompilerParams(dimension_semantics=("parallel",),
                                             vmem_limit_bytes=VMEM_LIMIT),
        name="norm_mod",
    )(x, w, sc, sh)


def _matmul_kernel(a_ref, w_ref, o_ref):
    o_ref[...] = jnp.dot(a_ref[...], w_ref[...], preferred_element_type=_f32).astype(o_ref.dtype)


def _wspec(layer, block, index):
    if layer is None:
        return pl.BlockSpec(block, index)
    return pl.BlockSpec((None,) + block, lambda *g: (layer,) + index(*g))


def _matmul(a, w, layer=None, tm=2048, tn=512, out_dtype=_f32):
    M, K = a.shape
    N = w.shape[-1]
    return pl.pallas_call(
        _matmul_kernel,
        out_shape=jax.ShapeDtypeStruct((M, N), out_dtype),
        grid=(M // tm, N // tn),
        in_specs=[pl.BlockSpec((tm, K), lambda i, j: (i, 0)),
                  _wspec(layer, (K, tn), lambda i, j: (0, j))],
        out_specs=pl.BlockSpec((tm, tn), lambda i, j: (i, j)),
        compiler_params=pltpu.CompilerParams(dimension_semantics=("parallel", "parallel"),
                                             vmem_limit_bytes=VMEM_LIMIT),
        name="in_proj",
    )(a, w)


def _out_proj_kernel(a0_ref, a1_ref, a2_ref, a3_ref, w_ref, x_ref, g_ref, nw_ref, sc_ref, sh_ref, xo_ref, ho_ref):
    y = None
    for i, a_ref in enumerate((a0_ref, a1_ref, a2_ref, a3_ref)):
        part = jnp.dot(a_ref[...], w_ref[i * MIX_GROUP:(i + 1) * MIX_GROUP, :], preferred_element_type=_f32)
        y = part if y is None else y + part
    xn = x_ref[...] + g_ref[0] * y
    xo_ref[...] = xn
    ho_ref[...] = _rms_mod(xn, nw_ref[...], sc_ref[0], sh_ref[0]).astype(ho_ref.dtype)


def _out_proj(mixed, w, x, g, nw, sc, sh, layer=None, tm=512):
    per_b = SEQ // tm
    bspec = pl.BlockSpec((1, 1, D_MODEL), lambda i: (i // per_b, 0, 0))
    aspec = pl.BlockSpec((tm, MIX_GROUP), lambda i: (i, 0))
    return pl.pallas_call(
        _out_proj_kernel,
        out_shape=(jax.ShapeDtypeStruct((TOKENS, D_MODEL), _f32),
                   jax.ShapeDtypeStruct((TOKENS, D_MODEL), _bf16)),
        grid=(TOKENS // tm,),
        in_specs=[aspec, aspec, aspec, aspec,
                  _wspec(layer, (MIX_OUT, D_MODEL), lambda i: (0, 0)),
                  pl.BlockSpec((tm, D_MODEL), lambda i: (i, 0)),
                  bspec,
                  pl.BlockSpec((1, D_MODEL), lambda i: (0, 0)),
                  bspec, bspec],
        out_specs=(pl.BlockSpec((tm, D_MODEL), lambda i: (i, 0)),
                   pl.BlockSpec((tm, D_MODEL), lambda i: (i, 0))),
        compiler_params=pltpu.CompilerParams(dimension_semantics=("parallel",),
                                             vmem_limit_bytes=VMEM_LIMIT),
        name="out_proj",
    )(*mixed, w, x, g, nw, sc, sh)


def _mlp_kernel(h_ref, w1_ref, w2_ref, x_ref, g_ref, nw_ref, sc_ref, sh_ref, o_ref, hn_ref, acc_ref):
    j = pl.program_id(1)

    @pl.when(j == 0)
    def _():
        acc_ref[...] = jnp.zeros_like(acc_ref)

    u = jnp.dot(h_ref[...], w1_ref[...], preferred_element_type=_f32)
    u = jnp.square(jnp.maximum(u, 0.0)).astype(_bf16)
    acc_ref[...] += jnp.dot(u, w2_ref[...], preferred_element_type=_f32)

    @pl.when(j == pl.num_programs(1) - 1)
    def _():
        xn = x_ref[...] + g_ref[0] * acc_ref[...]
        o_ref[...] = xn
        hn_ref[...] = _rms_mod(xn, nw_ref[...], sc_ref[0], sh_ref[0]).astype(hn_ref.dtype)


def _mlp(h, w1, w2, x, g, nw, sc, sh, next_dtype, layer=None, tm=512, th=1024):
    per_b = SEQ // tm
    bspec = pl.BlockSpec((1, 1, D_MODEL), lambda i, j: (i // per_b, 0, 0))
    xspec = pl.BlockSpec((tm, D_MODEL), lambda i, j: (i, 0))
    return pl.pallas_call(
        _mlp_kernel,
        out_shape=(jax.ShapeDtypeStruct((TOKENS, D_MODEL), _f32),
                   jax.ShapeDtypeStruct((TOKENS, D_MODEL), next_dtype)),
        grid=(TOKENS // tm, MLP_HIDDEN // th),
        in_specs=[xspec,
                  _wspec(layer, (D_MODEL, th), lambda i, j: (0, j)),
                  _wspec(layer, (th, D_MODEL), lambda i, j: (j, 0)),
                  xspec, bspec,
                  pl.BlockSpec((1, D_MODEL), lambda i, j: (0, 0)),
                  bspec, bspec],
        out_specs=(xspec, xspec),
        scratch_shapes=[pltpu.VMEM((tm, D_MODEL), _f32)],
        compiler_params=pltpu.CompilerParams(dimension_semantics=("parallel", "arbitrary"),
                                             vmem_limit_bytes=VMEM_LIMIT),
        name="mlp",
    )(h, w1, w2, x, g, nw, sc, sh)


NSA_R = NSA_HEADS // NSA_KV_GROUPS
NSA_CMP_ROWS = SEQ // CMP_STRIDE
NSA_NC = NSA_CMP_ROWS - CMP_BLOCK // CMP_STRIDE + 1
NSA_NSB = SEQ // SLC_BLOCK
NSA_NQB = SEQ // Q_BLOCK
NSA_SEQS_PER_STEP = 4
assert BATCH % NSA_SEQS_PER_STEP == 0
assert NSA_CMP_ROWS == LANE and Q_BLOCK == LANE and LANE % NSA_NSB == 0 and SLC_TOP_N <= NSA_NSB
assert NSA_KV_GROUPS * NSA_HEAD_DIM == LANE and CMP_BLOCK == 2 * CMP_STRIDE


def _bucket_value(tab_ref, h, rel):
    exact = REL_BUCKETS // 2
    n = jnp.maximum(rel, 0)
    large = exact + (jnp.log(jnp.maximum(n, 1).astype(_f32) / exact)
                     / math.log(REL_MAX_DIST / exact) * (REL_BUCKETS - exact)).astype(jnp.int32)
    bucket = jnp.where(n < exact, n, jnp.minimum(large, REL_BUCKETS - 1))
    val = jnp.full(rel.shape, tab_ref[0, h], _f32)
    for b in range(1, REL_BUCKETS):
        val = jnp.where(bucket == b, tab_ref[b, h], val)
    return val


def _nsa_bias_kernel(tab_ref, tb_ref, cb_ref):
    h = pl.program_id(0)
    kl = lax.broadcasted_iota(jnp.int32, (Q_BLOCK, LANE), 0)
    ql = lax.broadcasted_iota(jnp.int32, (Q_BLOCK, LANE), 1)
    for d in range(3):
        tb_ref[d, 0] = _bucket_value(tab_ref, h, ql - kl + d * Q_BLOCK)
    cmp_end = kl * CMP_STRIDE + (CMP_BLOCK - 1)
    for qb in range(NSA_NQB):
        cb_ref[qb, 0] = _bucket_value(tab_ref, h, qb * Q_BLOCK + ql - cmp_end)


def _nsa_bias_tiles(rel_bias):
    assert 2 * Q_BLOCK >= REL_MAX_DIST
    return pl.pallas_call(
        _nsa_bias_kernel,
        out_shape=(jax.ShapeDtypeStruct((3, NSA_HEADS, Q_BLOCK, LANE), _f32),
                   jax.ShapeDtypeStruct((NSA_NQB, NSA_HEADS, Q_BLOCK, LANE), _f32)),
        grid=(NSA_HEADS,),
        in_specs=[pl.BlockSpec(memory_space=pltpu.SMEM)],
        out_specs=(pl.BlockSpec((3, 1, Q_BLOCK, LANE), lambda h: (0, h, 0, 0)),
                   pl.BlockSpec((NSA_NQB, 1, Q_BLOCK, LANE), lambda h: (0, h, 0, 0))),
        compiler_params=pltpu.CompilerParams(dimension_semantics=("parallel",),
                                             vmem_limit_bytes=VMEM_LIMIT),
        name="nsa_bias_tiles",
    )(rel_bias)


def _nsa_cmp_kernel(tk_ref, tv_ref, pos_ref, w1_ref, w2_ref, kc_ref, vc_ref):
    rows = lax.broadcasted_iota(jnp.int32, (NSA_CMP_ROWS, LANE), 0)
    for idx, (t_ref, o_ref) in enumerate(((tk_ref, kc_ref), (tv_ref, vc_ref))):
        u = v = None
        for i in range(CMP_STRIDE):
            t_i = t_ref[pl.ds(i, NSA_CMP_ROWS, stride=CMP_STRIDE), :]
            ui = jnp.dot((t_i + pos_ref[idx, 0, i]).astype(_bf16), w1_ref[idx, 0, i], preferred_element_type=_f32)
            vi = jnp.dot((t_i + pos_ref[idx, 1, i]).astype(_bf16), w1_ref[idx, 1, i], preferred_element_type=_f32)
            u = ui if u is None else u + ui
            v = vi if v is None else v + vi
        pre = u + pltpu.roll(v, NSA_CMP_ROWS - 1, 0)
        act = pre * jax.nn.sigmoid(pre)
        out = jnp.dot(act.astype(_bf16), w2_ref[idx], preferred_element_type=_f32)
        o_ref[0] = jnp.where(rows < NSA_NC, out, 0.0).astype(_bf16)


def _nsa_compress(proj_b, pos_x, w1_x, w2_x):
    B = proj_b.shape[0] // SEQ
    ospec = pl.BlockSpec((1, NSA_CMP_ROWS, LANE), lambda b: (b, 0, 0))
    return pl.pallas_call(
        _nsa_cmp_kernel,
        out_shape=(jax.ShapeDtypeStruct((B, NSA_CMP_ROWS, LANE), _bf16),) * 2,
        grid=(B,),
        in_specs=[pl.BlockSpec((SEQ, LANE), lambda b: (b, PB['nkc'] // LANE)),
                  pl.BlockSpec((SEQ, LANE), lambda b: (b, PB['nvc'] // LANE)),
                  pl.BlockSpec(pos_x.shape, lambda b: (0,) * pos_x.ndim),
                  pl.BlockSpec(w1_x.shape, lambda b: (0,) * w1_x.ndim),
                  pl.BlockSpec((2, LANE, LANE), lambda b: (0, 0, 0))],
        out_specs=(ospec, ospec),
        compiler_params=pltpu.CompilerParams(dimension_semantics=("parallel",),
                                             vmem_limit_bytes=VMEM_LIMIT),
        name="nsa_compress",
    )(proj_b, proj_b, pos_x, w1_x, w2_x)


def _dot_nt(a, b):
    return lax.dot_general(a, b, (((1,), (1,)), ((), ())), preferred_element_type=_f32)


def _nsa_kernel(q_ref, ks_ref, vs_ref, kw_ref, vw_ref, kc_ref, vc_ref, gate_ref, tb_ref, cb_ref, ov_ref,
                o_ref, m_ref, l_ref, acc_ref, sel_ref):
    R, QB, hd = NSA_R, Q_BLOCK, NSA_HEAD_DIM
    qb = pl.program_id(1)
    kl = lax.broadcasted_iota(jnp.int32, (QB, LANE), 0)
    t_q = qb * QB + lax.broadcasted_iota(jnp.int32, (QB, LANE), 1)
    NSQ = q_ref.shape[0]
    G = NSA_KV_GROUPS
    KB = 2 * QB

    def split_r(x):
        return [x[:, r * LANE:(r + 1) * LANE] for r in range(R)]

    gates_t = [jax.nn.sigmoid(gate_ref[sq]).T for sq in range(NSQ)]
    qps = [[jnp.concatenate([q_ref[sq, :, (R * g + r) * LANE:(R * g + r + 1) * LANE] for r in range(R)], axis=0)
            for g in range(G)] for sq in range(NSQ)]

    krow = lax.broadcasted_iota(jnp.int32, (KB, LANE), 0)
    t_q2 = qb * QB + lax.broadcasted_iota(jnp.int32, (KB, LANE), 1)
    SEL, WIN = 0, 1
    chain = lambda sq, br, g: (sq * 2 + br) * G + g

    def flash_step(work):
        items, scores = [], []
        for p, br in work:
            off = pl.multiple_of(p * KB, KB)
            rel = t_q2 - (p * KB + krow)
            tidx = [jnp.clip(qb - (KB // QB) * p - i, 0, 2) for i in range(KB // QB)]
            for sq in range(NSQ):
                k_blk = (ks_ref, kw_ref)[br][sq, pl.ds(off, KB), :]
                for g in range(G):
                    items.append((p, br, sq, g, off, rel, tidx))
                    scores.append(split_r(_dot_nt(k_blk, qps[sq][g])))
        probs, alphas = [], []
        for (p, br, sq, g, off, rel, tidx), s_t in zip(items, scores):
            ch = chain(sq, br, g)
            if br == SEL:
                blocks = [jnp.broadcast_to(sel_ref[sq * G + g, pl.ds((KB // SLC_BLOCK) * p + i, 1), :],
                                           (SLC_BLOCK, LANE)) for i in range(KB // SLC_BLOCK)]
                mask = (rel >= 0) & (jnp.concatenate(blocks, axis=0) > 0.5)
            else:
                mask = (rel >= 0) & (rel < WINDOW)
            s_t = jnp.concatenate(
                [jnp.where(mask, s_t[r] + jnp.concatenate([tb_ref[ti, R * g + r] for ti in tidx], axis=0), NEG_INF)
                 for r in range(R)], axis=1)
            m_old = m_ref[ch]
            m_new = jnp.maximum(m_old, jnp.max(s_t, axis=0, keepdims=True))
            e = jnp.exp(s_t - m_new)
            alpha = jnp.exp(m_old - m_new)
            l_ref[ch] = alpha * l_ref[ch] + jnp.sum(e, axis=0, keepdims=True)
            m_ref[ch] = m_new
            probs.append(e.astype(_bf16))
            alphas.append(alpha)
        for (p, br, sq, g, off, rel, tidx), e, alpha in zip(items, probs, alphas):
            ch = chain(sq, br, g)
            pv = _dot_tn((vs_ref, vw_ref)[br][sq, pl.ds(off, KB), :], e)
            acc_ref[ch] = alpha * acc_ref[ch] + pv[g * hd:(g + 1) * hd]

    mask_c = (t_q - (kl * CMP_STRIDE + CMP_BLOCK - 1) >= 0) & (kl < NSA_NC)
    sg = [(sq, g) for sq in range(NSQ) for g in range(G)]
    cmp_scores = [split_r(_dot_nt(kc_ref[sq], qps[sq][g])) for sq, g in sg]
    cmp_probs = []
    for (sq, g), s_t in zip(sg, cmp_scores):
        s_t = jnp.concatenate([jnp.where(mask_c, s_t[r] + cb_ref[0, R * g + r], NEG_INF) for r in range(R)], axis=1)
        e = split_r(jnp.exp(s_t - jnp.max(s_t, axis=0, keepdims=True)))
        e = jnp.concatenate([jnp.where(mask_c, e[r], 0.0) for r in range(R)], axis=1)
        den = jnp.sum(e, axis=0, keepdims=True)
        cmp_probs.append((e / jnp.where(den > 0.0, den, 1.0)).astype(_bf16))
    o_cmp = [_dot_tn(vc_ref[sq], p)[g * hd:(g + 1) * hd] for (sq, g), p in zip(sg, cmp_probs)]
    importance = [sum(split_r(jnp.dot(ov_ref[...], p, preferred_element_type=_f32))) for p in cmp_probs]
    j = lax.broadcasted_iota(jnp.int32, (NSA_NSB, LANE), 0)
    cur = (qb * QB + lax.broadcasted_iota(jnp.int32, (NSA_NSB, LANE), 1)) // SLC_BLOCK
    for (sq, g), imp in zip(sg, importance):
        imp = jnp.where((j == 0) | (j == cur) | (j == cur - 1), FORCE_SCORE, imp)
        imp = jnp.where(j <= cur, imp, NEG_INF)
        cnt = jnp.zeros((NSA_NSB, LANE), _f32)
        for jo in range(NSA_NSB):
            other = imp[jo:jo + 1, :]
            beats = (other > imp) | ((other == imp) & (j > jo))
            cnt = cnt + jnp.where(beats, 1.0, 0.0)
        sel_ref[sq * G + g] = jnp.where(cnt < SLC_TOP_N, 1.0, 0.0)

    p_hi = qb // (KB // QB) + 1
    n_win = p_hi - jnp.maximum(qb - WINDOW // QB, 0) // (KB // QB)
    m_ref[...] = jnp.full(m_ref.shape, NEG_INF, _f32)
    l_ref[...] = jnp.zeros(l_ref.shape, _f32)
    acc_ref[...] = jnp.zeros(acc_ref.shape, _f32)

    def both(i, carry):
        p = p_hi - 1 - i
        flash_step([(p, SEL), (p, WIN)])
        return carry

    def selected_pair(i, carry):
        p = p_hi - 1 - n_win - 2 * i
        flash_step([(p, SEL), (p - 1, SEL)])
        return carry

    def selected_one(i, carry):
        flash_step([(0, SEL)])
        return carry

    n_sel = p_hi - n_win
    lax.fori_loop(0, n_win, both, 0)
    lax.fori_loop(0, n_sel // 2, selected_pair, 0)
    lax.fori_loop(0, n_sel % 2, selected_one, 0)
    for sq in range(NSQ):
        heads = [None] * NSA_HEADS
        for g in range(G):
            cs, cw = chain(sq, SEL, g), chain(sq, WIN, g)
            o_c, o_s, o_w = split_r(o_cmp[sq * G + g]), split_r(acc_ref[cs] / l_ref[cs]), split_r(acc_ref[cw] / l_ref[cw])
            for r in range(R):
                h = R * g + r
                gt = gates_t[sq]
                heads[h] = (gt[3 * h:3 * h + 1] * o_c[r] + gt[3 * h + 1:3 * h + 2] * o_s[r]
                            + gt[3 * h + 2:3 * h + 3] * o_w[r])
        for pk in range(NSA_HEADS * hd // LANE):
            per = LANE // hd
            pair = jnp.concatenate(heads[per * pk:per * (pk + 1)], axis=0)
            o_ref[sq, :, pk * LANE:(pk + 1) * LANE] = pair.T.astype(o_ref.dtype)


def _nsa_attention(proj_a, proj_b, kc, vc, tb, cb, overlap_t):
    B = kc.shape[0]
    nsq = NSA_SEQS_PER_STEP
    nqb = NSA_NQB
    qcols = NSA_HEADS * LANE
    out_cols = NSA_HEADS * NSA_HEAD_DIM
    pa = proj_a.reshape(B, SEQ, proj_a.shape[1])
    pb = proj_b.reshape(B, SEQ, proj_b.shape[1])
    kv = lambda blk: pl.BlockSpec((nsq, SEQ, LANE), lambda b, q: (b, 0, qcols // LANE + blk))
    cspec = pl.BlockSpec((nsq, NSA_CMP_ROWS, LANE), lambda b, q: (b, 0, 0))
    nchain = nsq * 2 * NSA_KV_GROUPS
    out = pl.pallas_call(
        _nsa_kernel,
        out_shape=jax.ShapeDtypeStruct((B, SEQ, out_cols), _bf16),
        grid=(B // nsq, nqb),
        in_specs=[pl.BlockSpec((nsq, Q_BLOCK, qcols), lambda b, q: (b, q, 0)),
                  kv(0), kv(1), kv(2), kv(3), cspec, cspec,
                  pl.BlockSpec((nsq, Q_BLOCK, LANE), lambda b, q: (b, q, PB['misc'] // LANE)),
                  pl.BlockSpec((3, NSA_HEADS, Q_BLOCK, LANE), lambda b, q: (0, 0, 0, 0)),
                  pl.BlockSpec((1, NSA_HEADS, Q_BLOCK, LANE), lambda b, q: (q, 0, 0, 0)),
                  pl.BlockSpec((NSA_NSB, LANE), lambda b, q: (0, 0))],
        out_specs=pl.BlockSpec((nsq, Q_BLOCK, out_cols), lambda b, q: (b, q, 0)),
        scratch_shapes=[pltpu.VMEM((nchain, 1, NSA_R * Q_BLOCK), _f32),
                        pltpu.VMEM((nchain, 1, NSA_R * Q_BLOCK), _f32),
                        pltpu.VMEM((nchain, NSA_HEAD_DIM, NSA_R * Q_BLOCK), _f32),
                        pltpu.VMEM((nsq * NSA_KV_GROUPS, NSA_NSB, Q_BLOCK), _f32)],
        compiler_params=pltpu.CompilerParams(dimension_semantics=("parallel", "arbitrary"),
                                             vmem_limit_bytes=VMEM_LIMIT),
        name="nsa_attention",
    )(pa, pa, pa, pa, pa, kc, vc, pb, tb, cb, overlap_t)
    return out.reshape(B * SEQ, out_cols)


def _nsa_overlap():
    n = jnp.arange(NSA_CMP_ROWS)[None, :]
    jj = jnp.arange(NSA_NSB)[:, None]
    return ((n * CMP_STRIDE <= jj * SLC_BLOCK + SLC_BLOCK - 1)
            & (n * CMP_STRIDE + CMP_BLOCK - 1 >= jj * SLC_BLOCK) & (n < NSA_NC)).astype(_bf16)


def _nsa_weight_prep(w_in, cmp_pos, cmp_w1, cmp_w2):
    G, hd = NSA_KV_GROUPS, NSA_HEAD_DIM
    eye = jnp.eye(G, dtype=_f32)
    nq = NSA_HEADS * hd
    w_in = w_in.astype(_bf16)
    zeros = jnp.zeros((DEPTH, D_MODEL, hd), _bf16)
    q_cols = []
    for h in range(NSA_HEADS):
        wq_h = w_in[:, :, h * hd:(h + 1) * hd] * (hd ** -0.5)
        q_cols += [wq_h, zeros] if h // NSA_R == 0 else [zeros, wq_h]
    slc_win = w_in[:, :, nq + 2 * NSA_KV:nq + 6 * NSA_KV]
    w_a = jnp.concatenate(q_cols + [slc_win], axis=-1)
    w_b = jnp.concatenate([w_in[:, :, _IN_OFF[n]:_IN_OFF[n] + w] for n, w in _PB_SEGMENTS]
                          + [jnp.zeros((DEPTH, D_MODEL, PROJ_B_COLS - PROJ_B_USED), _bf16)], axis=-1)
    half = CMP_BLOCK // 2
    w1 = cmp_w1.reshape(DEPTH, 2, CMP_BLOCK, hd, hd)
    w1_x = jnp.einsum('lxide,gh->lxigdhe', w1, eye).reshape(DEPTH, 2, 2, half, G * hd, G * hd).astype(_bf16)
    w2_x = jnp.einsum('lxde,gh->lxgdhe', cmp_w2, eye).reshape(DEPTH, 2, G * hd, G * hd).astype(_bf16)
    pos_x = jnp.broadcast_to(cmp_pos[:, :, :, None, :], (DEPTH, 2, CMP_BLOCK, G, hd))
    pos_x = pos_x.reshape(DEPTH, 2, 2, half, 1, G * hd)
    return w_a, w_b, w1_x, w2_x, pos_x


REC_BLOCK = 2 * CHUNK
REC_SEQS_PER_STEP = {'ssd': 2, 'gdn': 4, 'gla': 4}
assert REC_BLOCK == LANE and SEQ % REC_BLOCK == 0 and all(BATCH % n == 0 for n in REC_SEQS_PER_STEP.values())


def _split3(x):
    hi = x.astype(_bf16)
    r = x - hi.astype(_f32)
    mid = r.astype(_bf16)
    lo = (r - mid.astype(_f32)).astype(_bf16)
    return hi, mid, lo


def _dot_exact_rhs(m, x):
    return sum(jnp.dot(m, p, preferred_element_type=_f32) for p in _split3(x))


def _dot_exact_lhs(x, m):
    return sum(jnp.dot(p, m, preferred_element_type=_f32) for p in _split3(x))


def _dot_tn(a, b):
    return lax.dot_general(a, b, (((0,), (0,)), ((), ())), preferred_element_type=_f32)


def _mm3(x, y):
    xh = x.astype(_bf16)
    xl = (x - xh.astype(_f32)).astype(_bf16)
    yh = y.astype(_bf16)
    yl = (y - yh.astype(_f32)).astype(_bf16)
    return (jnp.dot(xh, yh, preferred_element_type=_f32) + jnp.dot(xh, yl, preferred_element_type=_f32)
            + jnp.dot(xl, yh, preferred_element_type=_f32))


def _chunk_masks():
    ri = lax.broadcasted_iota(jnp.int32, (REC_BLOCK, REC_BLOCK), 0)
    ci = lax.broadcasted_iota(jnp.int32, (REC_BLOCK, REC_BLOCK), 1)
    same = (ri // CHUNK) == (ci // CHUNK)
    return ri, ci, (ci <= ri) & same, (ci < ri) & same, (ri <= ci) & same


def _as_mxu(mask):
    return jnp.where(mask, 1.0, 0.0).astype(_bf16)


def _softplus(x):
    return jnp.maximum(x, 0.0) + jnp.log1p(jnp.exp(-jnp.abs(x)))


def _silu(x):
    return x * jax.nn.sigmoid(x)


def _chunk_last(x):
    ri = lax.broadcasted_iota(jnp.int32, x.shape, 0)
    return jnp.where(ri < CHUNK, x[CHUNK - 1:CHUNK], x[2 * CHUNK - 1:2 * CHUNK])


def _expand_heads(x, lane0, nheads, width):
    per = LANE // width
    lane = lax.broadcasted_iota(jnp.int32, (x.shape[0], LANE), 1)
    pieces = []
    for p0 in range(lane0, lane0 + nheads, per):
        piece = jnp.broadcast_to(x[:, p0:p0 + 1], (x.shape[0], LANE))
        for k in range(1, per):
            piece = jnp.where(lane < k * width, piece, jnp.broadcast_to(x[:, p0 + k:p0 + k + 1], (x.shape[0], LANE)))
        pieces.append(piece)
    return jnp.concatenate(pieces, axis=1) if len(pieces) > 1 else pieces[0]


def _conv_silu(x, prev, w_ref, c0, bias=None):
    n, C = x.shape
    ntap = w_ref.shape[0]
    rows = lax.broadcasted_iota(jnp.int32, (n, C), 0)
    acc = x * w_ref[ntap - 1:ntap, c0:c0 + C]
    for s in range(1, ntap):
        xs = jnp.where(rows < s, pltpu.roll(prev, s, 0), pltpu.roll(x, s, 0))
        acc = acc + xs * w_ref[ntap - 1 - s:ntap - s, c0:c0 + C]
    if bias is not None:
        acc = acc + bias
    return _silu(acc)


def _rms(x, w):
    return x * lax.rsqrt(jnp.mean(x * x, axis=-1, keepdims=True) + EPS) * w


def _ssd_kernel(z_ref, xbc_ref, dt_ref, cw_ref, cb_ref, dtb_ref, alog_ref, dskip_ref, nw_ref, o_ref,
                prev_ref, state_ref):
    @pl.when(pl.program_id(1) == 0)
    def _():
        prev_ref[...] = jnp.zeros(prev_ref.shape, _f32)
        state_ref[...] = jnp.zeros(state_ref.shape, _f32)

    for ns in range(z_ref.shape[0]):
        _ssd_block(z_ref.at[ns], xbc_ref.at[ns], dt_ref.at[ns], cw_ref, cb_ref, dtb_ref, alog_ref, dskip_ref, nw_ref,
                   o_ref.at[ns], prev_ref.at[ns], state_ref.at[ns])


def _ssd_block(z_ref, xbc_ref, dt_ref, cw_ref, cb_ref, dtb_ref, alog_ref, dskip_ref, nw_ref, o_ref,
               prev_ref, state_ref):
    G, R, P, N = SSD_GROUPS, SSD_HEADS // SSD_GROUPS, SSD_HEAD_DIM, SSD_STATE
    x_in = xbc_ref[...]
    xc = _conv_silu(x_in, prev_ref[...], cw_ref, 0, cb_ref[...])
    prev_ref[...] = x_in
    _, _, tril, _, triu = _chunk_masks()
    lane = lax.broadcasted_iota(jnp.int32, (REC_BLOCK, LANE), 1)
    L0 = PBL['sdt']
    dt = _softplus(dt_ref[...] + dtb_ref[...])
    da = dt * (-jnp.exp(alog_ref[...]))
    a_cum = _dot_exact_rhs(_as_mxu(tril), da)
    a_cum_t = _dot_exact_lhs(da.T, _as_mxu(triu))
    a_last = _chunk_last(a_cum)
    xs = xc[:, :SSD_INNER]
    xdt = xs * _expand_heads(dt, L0, SSD_HEADS, P)
    xdtd = (xdt * _expand_heads(jnp.exp(a_last - a_cum), L0, SSD_HEADS, P)).astype(_bf16)
    xdt_b = xdt.astype(_bf16)
    ea = _expand_heads(jnp.exp(a_cum), L0, SSD_HEADS, P)
    y_groups = []
    for g in range(G):
        bm = xc[:, SSD_INNER + g * N:SSD_INNER + (g + 1) * N].astype(_bf16)
        cm = xc[:, SSD_INNER + (G + g) * N:SSD_INNER + (G + g + 1) * N].astype(_bf16)
        cbm = _dot_nt(cm, bm)
        intra = []
        for pr in range(R // 2):
            both = []
            for k in range(2):
                h = g * R + 2 * pr + k
                seg = jnp.exp(jnp.where(tril, a_cum[:, L0 + h:L0 + h + 1] - a_cum_t[L0 + h:L0 + h + 1, :], -jnp.inf))
                both.append(jnp.dot((cbm * seg).astype(_bf16), xdt_b[:, (h - k) * P:(h - k + 2) * P],
                                    preferred_element_type=_f32))
            intra.append(jnp.where(lane < P, both[0], both[1]))
        y_intra = jnp.concatenate(intra, axis=1)
        prev_rows = []
        for c in range(REC_BLOCK // CHUNK):
            rows = slice(c * CHUNK, (c + 1) * CHUNK)
            st = state_ref[g]
            prev_rows.append(jnp.dot(cm[rows], st.astype(_bf16), preferred_element_type=_f32))
            dec = _expand_heads(jnp.exp(a_cum[(c + 1) * CHUNK - 1:(c + 1) * CHUNK]), L0, SSD_HEADS, P)
            state_ref[g] = (st * dec[:, g * R * P:(g + 1) * R * P]
                            + _dot_tn(bm[rows], xdtd[rows, g * R * P:(g + 1) * R * P]))
        y_groups.append(y_intra + jnp.concatenate(prev_rows, axis=0) * ea[:, g * R * P:(g + 1) * R * P])
    y = jnp.concatenate(y_groups, axis=1) + xs * dskip_ref[...]
    y = y * _silu(z_ref[...])
    gw = SSD_INNER // G
    for g in range(G):
        o_ref[:, g * gw:(g + 1) * gw] = _rms(y[:, g * gw:(g + 1) * gw], nw_ref[:, g * gw:(g + 1) * gw]).astype(o_ref.dtype)


def _gdn_kernel(q_ref, k_ref, v_ref, z_ref, ba_ref, cw_ref, dtb_ref, alog_ref, nw_ref, o_ref,
                pq_ref, pk_ref, pv_ref, state_ref):
    H, Dh = GDN_HEADS, GDN_HEAD_DIM

    @pl.when(pl.program_id(1) == 0)
    def _():
        for r in (pq_ref, pk_ref, pv_ref, state_ref):
            r[...] = jnp.zeros(r.shape, _f32)

    NS = q_ref.shape[0]
    ri, ci, tril, strict, triu = _chunk_masks()
    q, k, v, beta, gcum, gcum_t, glast = [], [], [], [], [], [], []
    for ns in range(NS):
        for dst, x_ref, p_ref, c0 in ((q, q_ref, pq_ref, 0), (k, k_ref, pk_ref, GDN_WIDTH), (v, v_ref, pv_ref, 2 * GDN_WIDTH)):
            x_in = x_ref[ns]
            dst.append(_conv_silu(x_in, p_ref[ns], cw_ref, c0))
            p_ref[ns] = x_in
        ba = ba_ref[ns]
        beta.append(jax.nn.sigmoid(ba))
        gl = -jnp.exp(alog_ref[...]) * _softplus(ba + dtb_ref[...])
        gcum.append(_dot_exact_rhs(_as_mxu(tril), gl))
        gcum_t.append(_dot_exact_lhs(gl.T, _as_mxu(triu)))
        glast.append(_chunk_last(gcum[ns]))
    zero_rows = jnp.zeros((CHUNK, Dh), _f32)
    eye = jnp.where(tril & jnp.logical_not(strict), 1.0, 0.0)
    hs = range(NS * H)
    seq = [i // H for i in hs]
    sls = [slice((i % H) * Dh, (i % H + 1) * Dh) for i in hs]
    qh = [q[seq[i]][:, sls[i]] for i in hs]
    kh = [k[seq[i]][:, sls[i]] for i in hs]
    qq = [qh[i] * lax.rsqrt(jnp.sum(qh[i] * qh[i], axis=-1, keepdims=True) + EPS) * (Dh ** -0.5) for i in hs]
    kk = [kh[i] * lax.rsqrt(jnp.sum(kh[i] * kh[i], axis=-1, keepdims=True) + EPS) for i in hs]
    lb = [PBL['gbeta'] + i % H for i in hs]
    lg = [PBL['gbeta'] + H + i % H for i in hs]
    bcol = [beta[seq[h]][:, lb[h]:lb[h] + 1] for h in hs]
    gcol = [gcum[seq[h]][:, lg[h]:lg[h] + 1] for h in hs]
    decay = [jnp.exp(jnp.where(tril, gcol[h] - gcum_t[seq[h]][lg[h]:lg[h] + 1, :], -jnp.inf)) for h in hs]
    kb = [kk[h] * bcol[h] for h in hs]
    s = [_dot_nt(jnp.concatenate([kb[h], qq[h]], axis=0).astype(_bf16), kk[h].astype(_bf16)) for h in hs]
    a_mat = [jnp.where(strict, s[h][:REC_BLOCK] * decay[h], 0.0) for h in hs]
    aqk = [(s[h][REC_BLOCK:] * decay[h]).astype(_bf16) for h in hs]
    SUB = 8
    same = lambda n: (ri // n) == (ci // n)
    a_sub = [jnp.where(same(SUB), a_mat[h], 0.0) for h in hs]
    tinv = [eye - a_sub[h] for h in hs]
    pw = a_sub
    for _ in range(SUB.bit_length() - 2):
        pw = [_mm3(pw[h], pw[h]) for h in hs]
        tinv = [tinv[h] + _mm3(tinv[h], pw[h]) for h in hs]
    n = SUB
    while n < CHUNK:
        enclosed = same(2 * n) & jnp.logical_not(same(n))
        tc = [_mm3(tinv[h], jnp.where(enclosed, a_mat[h], 0.0)) for h in hs]
        tinv = [tinv[h] - _mm3(tc[h], tinv[h]) for h in hs]
        n *= 2
    sol = [_mm3(tinv[h], jnp.concatenate([v[seq[h]][:, sls[h]] * bcol[h], kb[h] * jnp.exp(gcol[h])], axis=1))
           for h in hs]
    u = [sol[h][:, :Dh] for h in hs]
    w = [sol[h][:, Dh:].astype(_bf16) for h in hs]
    q_dec = [(qq[h] * jnp.exp(gcol[h])).astype(_bf16) for h in hs]
    k_end = [(kk[h] * jnp.exp(glast[seq[h]][:, lg[h]:lg[h] + 1] - gcol[h])).astype(_bf16) for h in hs]
    o_rows = [[] for _ in hs]
    for c in range(REC_BLOCK // CHUNK):
        rows = slice(c * CHUNK, (c + 1) * CHUNK)
        st = [state_ref[h] for h in hs]
        st_b = [st[h].astype(_bf16) for h in hs]
        v_new = [u[h][rows] - jnp.dot(w[h][rows], st_b[h], preferred_element_type=_f32) for h in hs]
        v_full = [jnp.concatenate([v_new[h], zero_rows] if c == 0 else [zero_rows, v_new[h]], axis=0).astype(_bf16)
                  for h in hs]
        for h in hs:
            o_rows[h].append(jnp.dot(q_dec[h][rows], st_b[h], preferred_element_type=_f32)
                             + jnp.dot(aqk[h][rows], v_full[h], preferred_element_type=_f32))
            d_last = jnp.exp(gcum[seq[h]][(c + 1) * CHUNK - 1:(c + 1) * CHUNK, lg[h]:lg[h] + 1])
            state_ref[h] = st[h] * d_last + _dot_tn(k_end[h][rows], v_new[h].astype(_bf16))
    for h in hs:
        o = _rms(jnp.concatenate(o_rows[h], axis=0), nw_ref[...]) * _silu(z_ref[seq[h], :, sls[h]])
        o_ref[seq[h], :, sls[h]] = o.astype(o_ref.dtype)


def _gla_kernel(q_ref, k_ref, v_ref, go_ref, lr_ref, w2_ref, gb_ref, nw_ref, o_ref, state_ref):
    @pl.when(pl.program_id(1) == 0)
    def _():
        state_ref[...] = jnp.zeros(state_ref.shape, _f32)

    for ns in range(q_ref.shape[0]):
        _gla_block(q_ref.at[ns], k_ref.at[ns], v_ref.at[ns], go_ref.at[ns], lr_ref.at[ns], w2_ref, gb_ref, nw_ref,
                   o_ref.at[ns], state_ref.at[ns])


def _gla_block(q_ref, k_ref, v_ref, go_ref, lr_ref, w2_ref, gb_ref, nw_ref, o_ref, state_ref):
    H, Dk, Dv = GLA_HEADS, GLA_DK, GLA_DV
    _, _, tril, _, _ = _chunk_masks()
    lane = lax.broadcasted_iota(jnp.int32, (REC_BLOCK, LANE), 1)
    pre = jnp.dot(lr_ref[...].astype(_bf16), w2_ref[...], preferred_element_type=_f32) + gb_ref[...]
    gk = (jnp.minimum(pre, 0.0) - jnp.log1p(jnp.exp(-jnp.abs(pre)))) / GLA_GATE_NORM
    bcum = _dot_exact_rhs(_as_mxu(tril), gk)
    blast = _chunk_last(bcum)
    q_dec = q_ref[...] * (Dk ** -0.5) * jnp.exp(bcum)
    k_inv = (k_ref[...] * jnp.exp(-bcum)).astype(_bf16)
    k_end = (k_ref[...] * jnp.exp(blast - bcum)).astype(_bf16)
    per = LANE // Dk
    nchunk = REC_BLOCK // CHUNK
    chunk_rows = [slice(c * CHUNK, (c + 1) * CHUNK) for c in range(nchunk)]
    psl = [slice(pr * LANE, (pr + 1) * LANE) for pr in range(H // per)]
    qm = [jnp.where(lane // Dk == h % per, q_dec[:, psl[h // per]], 0.0).astype(_bf16) for h in range(H)]
    vh = [v_ref[:, h * Dv:(h + 1) * Dv].astype(_bf16) for h in range(H)]
    attn = [jnp.where(tril, _dot_nt(qm[h], k_inv[:, psl[h // per]]), 0.0).astype(_bf16) for h in range(H)]
    o_intra = [jnp.dot(attn[h], vh[h], preferred_element_type=_f32) for h in range(H)]
    local = []
    for rows in chunk_rows:
        per_pair = []
        for pr in range(H // per):
            loc = None
            for k in range(per):
                lk = _dot_tn(vh[pr * per + k][rows], k_end[rows, psl[pr]])
                loc = lk if loc is None else jnp.where(lane < k * Dk, loc, lk)
            per_pair.append(loc)
        local.append(per_pair)
    o_prev = [[] for _ in range(H)]
    for c, rows in enumerate(chunk_rows):
        for pr in range(H // per):
            st = state_ref[pr]
            st_b = st.astype(_bf16)
            for k in range(per):
                o_prev[pr * per + k].append(_dot_nt(qm[pr * per + k][rows], st_b))
            state_ref[pr] = st * jnp.exp(bcum[(c + 1) * CHUNK - 1:(c + 1) * CHUNK, psl[pr]]) + local[c][pr]
    for h in range(H):
        o = o_intra[h] + jnp.concatenate(o_prev[h], axis=0)
        o = _rms(o, nw_ref[...]) * _silu(go_ref[:, h * Dv:(h + 1) * Dv])
        o_ref[:, h * Dv:(h + 1) * Dv] = o.astype(o_ref.dtype)


def _rec_call(body, proj_b, col_blocks, params, out_cols, scratch, name, nseq=None):
    B = proj_b.shape[0] // SEQ
    nblk = SEQ // REC_BLOCK
    if nseq is None:
        src, grid = proj_b, (B, nblk)
        in_specs = [pl.BlockSpec((REC_BLOCK, w), (lambda b, t, c=c0 // w: (b * nblk + t, c))) for c0, w in col_blocks]
        out_shape = jax.ShapeDtypeStruct((B * SEQ, out_cols), _bf16)
        out_spec = pl.BlockSpec((REC_BLOCK, out_cols), lambda b, t: (b * nblk + t, 0))
    else:
        src, grid = proj_b.reshape(B, SEQ, proj_b.shape[1]), (B // nseq, nblk)
        in_specs = [pl.BlockSpec((nseq, REC_BLOCK, w), (lambda b, t, c=c0 // w: (b, t, c))) for c0, w in col_blocks]
        out_shape = jax.ShapeDtypeStruct((B, SEQ, out_cols), _bf16)
        out_spec = pl.BlockSpec((nseq, REC_BLOCK, out_cols), lambda b, t: (b, t, 0))
    for p in params:
        in_specs.append(pl.BlockSpec(p.shape, lambda b, t, nd=p.ndim: (0,) * nd))
    out = pl.pallas_call(
        body,
        out_shape=out_shape,
        grid=grid,
        in_specs=in_specs,
        out_specs=out_spec,
        scratch_shapes=scratch,
        compiler_params=pltpu.CompilerParams(dimension_semantics=("parallel", "arbitrary"),
                                             vmem_limit_bytes=VMEM_LIMIT),
        name=name,
    )(*([src] * len(col_blocks)), *params)
    return out.reshape(B * SEQ, out_cols)


def _lane_pad(v, lane0=0):
    return jnp.pad(v.astype(_f32), (lane0, LANE - lane0 - v.shape[0]))[None]


def _ssd_call(proj_b, conv_w, conv_b, dt_bias, a_log, d_skip, norm_w):
    cols = [(PB['sz'], SSD_INNER), (PB['sxbc'], SSD_XBC), (PB['misc'], LANE)]
    params = [conv_w, conv_b[None], _lane_pad(dt_bias, PBL['sdt']), _lane_pad(a_log, PBL['sdt']),
              jnp.repeat(d_skip, SSD_HEAD_DIM)[None], norm_w[None]]
    ns = REC_SEQS_PER_STEP['ssd']
    scratch = [pltpu.VMEM((ns, REC_BLOCK, SSD_XBC), _f32),
               pltpu.VMEM((ns, SSD_GROUPS, SSD_STATE, SSD_INNER // SSD_GROUPS), _f32)]
    return _rec_call(_ssd_kernel, proj_b, cols, params, SSD_INNER, scratch, "ssd", nseq=ns)


def _gdn_call(proj_b, conv_w, dt_bias, a_log, norm_w):
    W = GDN_WIDTH
    cols = [(PB['gq'], W), (PB['gk'], W), (PB['gv'], W), (PB['gz'], W), (PB['misc'], LANE)]
    decay_lane = PBL['gbeta'] + GDN_HEADS
    params = [conv_w, _lane_pad(dt_bias, decay_lane), _lane_pad(a_log, decay_lane), norm_w[None]]
    ns = REC_SEQS_PER_STEP['gdn']
    scratch = ([pltpu.VMEM((ns, REC_BLOCK, W), _f32)] * 3
               + [pltpu.VMEM((ns * GDN_HEADS, GDN_HEAD_DIM, GDN_HEAD_DIM), _f32)])
    return _rec_call(_gdn_kernel, proj_b, cols, params, W, scratch, "gdn", nseq=ns)


def _gla_call(proj_b, gate_w2, gate_b, norm_w):
    cols = [(PB['lq'], GLA_KEY), (PB['lk'], GLA_KEY), (PB['lv'], GLA_VAL), (PB['lg'], GLA_VAL), (PB['misc'], LANE)]
    w2 = jnp.pad(gate_w2, ((PBL['llr'], LANE - PBL['llr'] - GLA_GATE_RANK), (0, 0))).astype(_bf16)
    params = [w2, gate_b[None], norm_w[None]]
    ns = REC_SEQS_PER_STEP['gla']
    scratch = [pltpu.VMEM((ns, GLA_HEADS * GLA_DK // LANE, GLA_DV, LANE), _f32)]
    return _rec_call(_gla_kernel, proj_b, cols, params, GLA_VAL, scratch, "gla", nseq=ns)


def kernel(x, c, rel_bias, norm1_w, norm2_w, ada_w, ada_b, w_in, w_out, nsa_cmp_pos, nsa_cmp_w1, nsa_cmp_w2, ssd_conv_w, ssd_conv_b, ssd_dt_bias, ssd_a_log, ssd_d, ssd_norm_w, gdn_conv_w, gdn_dt_bias, gdn_a_log, gdn_norm_w, gla_gate_w2, gla_gate_b, gla_norm_w, mlp_w1, mlp_w2, final_norm_w):
    B, S, D = x.shape
    mod = _ada_all(c, ada_w, ada_b).reshape(DEPTH, B, 6, 1, D)
    w_a, w_b, cmp_w1_x, cmp_w2_x, cmp_pos_x = _nsa_weight_prep(w_in, nsa_cmp_pos, nsa_cmp_w1, nsa_cmp_w2)
    w_out_b = w_out.astype(_bf16)
    w1_b = mlp_w1.astype(_bf16)
    w2_b = mlp_w2.astype(_bf16)
    tb, cb = _nsa_bias_tiles(rel_bias)
    overlap_t = _nsa_overlap()
    xf = x.reshape(TOKENS, D)
    no_mod = jnp.zeros((B, 1, D), _f32)
    h = _norm_mod(xf, norm1_w[0][None], mod[0, :, 1], mod[0, :, 0])
    for l in range(DEPTH):
        sh1, sc1, g1, sh2, sc2, g2 = (mod[l, :, i] for i in range(6))
        proj_a = _matmul(h, w_a, l, out_dtype=_bf16)
        proj_b = _matmul(h, w_b, l, tn=PROJ_B_TN)
        kc, vc = _nsa_compress(proj_b, cmp_pos_x[l], cmp_w1_x[l], cmp_w2_x[l])
        y_nsa = _nsa_attention(proj_a, proj_b, kc, vc, tb, cb, overlap_t)
        y_ssd = _ssd_call(proj_b, ssd_conv_w[l], ssd_conv_b[l], ssd_dt_bias[l], ssd_a_log[l], ssd_d[l], ssd_norm_w[l])
        y_gdn = _gdn_call(proj_b, gdn_conv_w[l], gdn_dt_bias[l], gdn_a_log[l], gdn_norm_w[l])
        y_gla = _gla_call(proj_b, gla_gate_w2[l], gla_gate_b[l], gla_norm_w[l])
        xf, h2 = _out_proj((y_nsa, y_ssd, y_gdn, y_gla), w_out_b, xf, g1,
                           norm2_w[l][None], sc2, sh2, layer=l)
        if l + 1 < DEPTH:
            xf, h = _mlp(h2, w1_b, w2_b, xf, g2, norm1_w[l + 1][None], mod[l + 1, :, 1], mod[l + 1, :, 0], _bf16,
                         layer=l)
        else:
            _, out = _mlp(h2, w1_b, w2_b, xf, g2, final_norm_w[None], no_mod, no_mod, _f32, layer=l)
    return out.reshape(B, S, D)
```
